```python
import math
import jax
import jax.numpy as jnp
from jax import lax
import numpy as np

D_MODEL = 1024
BATCH = 4
SEQ = 8192
DEPTH = 4

N_MIXERS = 4
EPS = 1e-6

GDN_HEADS = 8
GDN_DK = 128
GDN_DV = 128
GDN_CONV = 5
GDN_CHUNK = 64

HGRN_EXPAND = 128
HGRN_HEADS = D_MODEL // HGRN_EXPAND
HGRN_DF = HGRN_EXPAND
HGRN_DI = D_MODEL // HGRN_HEADS
HGRN_F = HGRN_HEADS * HGRN_DF
HGRN_CHUNK = 64

SC_WIDTH = 3

MLA_HEADS = 16
MLA_NOPE = 64
MLA_ROPE = 32
MLA_V = 64
MLA_Q_LORA = 384
MLA_KV_LORA = 256
ROPE_THETA = 10000.0
Q_BLOCK = 128

N_EXPERTS = 16
D_EXPERT = 2048
CAPACITY_FACTOR = 2

kernel_name = 'hybrid_gdn_hgrn2_shortconv_mla_ecmoe'


def rms_norm(x, g):
    xf = x.astype(jnp.float32)
    y = xf * lax.rsqrt(jnp.mean(xf * xf, axis=-1, keepdims=True) + EPS)
    return (y * g.astype(jnp.float32)).astype(x.dtype)


def l2_normalize(x):
    xf = x.astype(jnp.float32)
    return xf * lax.rsqrt(jnp.sum(xf * xf, axis=-1, keepdims=True) + EPS)


def centred_depthwise_conv(x, w):
    K, C = w.shape
    return lax.conv_general_dilated(
        x, w[:, None, :], window_strides=(1,), padding=[(K // 2, K // 2)],
        dimension_numbers=('NWC', 'WIO', 'NWC'), feature_group_count=C)


def gated_delta_chunk(q, k, v, g, beta):
    f32 = jnp.float32
    B, H, S, dk = q.shape
    dv = v.shape[-1]
    C = GDN_CHUNK
    N = S // C
    q = q.astype(f32).reshape(B, H, N, C, dk) * (dk ** -0.5)
    k = k.astype(f32).reshape(B, H, N, C, dk)
    v = v.astype(f32).reshape(B, H, N, C, dv)
    g = jnp.cumsum(g.astype(f32).reshape(B, H, N, C), axis=-1)
    beta = beta.astype(f32).reshape(B, H, N, C)
    tril = jnp.tril(jnp.ones((C, C), bool))
    eye = jnp.eye(C, dtype=f32)
    decay = jnp.exp(jnp.where(tril, g[..., :, None] - g[..., None, :], -jnp.inf))
    kb = k * beta[..., None]
    lower = jnp.einsum('bhnid,bhnjd->bhnij', kb, k) * decay * (1.0 - eye)
    T = lax.linalg.triangular_solve(eye + lower, jnp.broadcast_to(eye, lower.shape),
                                    left_side=True, lower=True, unit_diagonal=True)
    u = T @ (v * beta[..., None])
    w = T @ (kb * jnp.exp(g)[..., None])
    intra = jnp.einsum('bhnid,bhnjd->bhnij', q, k) * decay
    q_dec = q * jnp.exp(g)[..., None]
    k_dec = k * jnp.exp(g[..., -1:] - g)[..., None]
    g_last = jnp.exp(g[..., -1])

    def step(state, inp):
        intra_c, u_c, w_c, qd_c, kd_c, gl_c = inp
        v_new = u_c - w_c @ state
        o = qd_c @ state + intra_c @ v_new
        state = state * gl_c[..., None, None] + jnp.swapaxes(kd_c, -1, -2) @ v_new
        return state, o

    xs = tuple(jnp.moveaxis(t, 2, 0) for t in (intra, u, w, q_dec, k_dec, g_last))
    _, o = lax.scan(step, jnp.zeros((B, H, dk, dv), f32), xs)
    return jnp.moveaxis(o, 0, 2).reshape(B, H, S, dv)


def hgrn2_chunk(q, k, v, logf):
    f32 = jnp.float32
    B, H, S, dF = q.shape
    dI = v.shape[-1]
    C = HGRN_CHUNK
    N = S // C
    to_chunks = lambda t: jnp.moveaxis(t.astype(f32).reshape(B, H, N, C, t.shape[-1]), 2, 0)
    q, k, v, logf = (to_chunks(t) for t in (q, k, v, logf))
    b = jnp.cumsum(logf, axis=-2)
    tril = jnp.tril(jnp.ones((C, C), bool))[:, :, None]

    def step(state, inp):
        q_c, k_c, v_c, b_c = inp
        rel = jnp.exp(jnp.where(tril, b_c[..., :, None, :] - b_c[..., None, :, :], -jnp.inf))
        scores = jnp.einsum('bhid,bhijd->bhij', q_c, rel * k_c[..., None, :, :])
        b_last = b_c[..., -1:, :]
        o = jnp.einsum('bhid,bhde->bhie', q_c * jnp.exp(b_c), state) + scores @ v_c
        state = (state * jnp.exp(b_last)[..., 0, :, None]
                 + jnp.einsum('bhjd,bhje->bhde', k_c * jnp.exp(b_last - b_c), v_c))
        return state, o

    _, o = lax.scan(step, jnp.zeros((B, H, dF, dI), f32), (q, k, v, b))
    return jnp.moveaxis(o, 0, 2).reshape(B, H, S, dI)


def _flip(t):
    return jnp.flip(t, axis=2)


def gdn_mixer(h, w_in, conv_w, A_log, dt_bias, norm_g, w_out):
    B, S, _ = h.shape
    H, dk, dv = GDN_HEADS, GDN_DK, GDN_DV
    qkv, z, a, b = jnp.split(h @ w_in, [2 * H * dk + H * dv, 2 * H * dk + 2 * H * dv,
                                         2 * H * dk + 2 * H * dv + 2 * H], axis=-1)
    qkv = jax.nn.silu(centred_depthwise_conv(qkv, conv_w))
    q, k, v = jnp.split(qkv, [H * dk, 2 * H * dk], axis=-1)
    heads = lambda t, d: t.reshape(B, S, H, d).transpose(0, 2, 1, 3)
    q, k, v = l2_normalize(heads(q, dk)), l2_normalize(heads(k, dk)), heads(v, dv)
    a = a.astype(jnp.float32).reshape(B, S, 2, H).transpose(2, 0, 3, 1)
    b = b.astype(jnp.float32).reshape(B, S, 2, H).transpose(2, 0, 3, 1)
    g = -jnp.exp(A_log.astype(jnp.float32))[:, None, :, None] * jax.nn.softplus(
        a + dt_bias.astype(jnp.float32)[:, None, :, None])
    beta = jax.nn.sigmoid(b)
    o_fwd = gated_delta_chunk(q, k, v, g[0], beta[0])
    o_bwd = _flip(gated_delta_chunk(_flip(q), _flip(k), _flip(v), _flip(g[1]), _flip(beta[1])))
    o = (o_fwd + o_bwd).astype(h.dtype).transpose(0, 2, 1, 3)
    o = rms_norm(o, norm_g) * jax.nn.silu(z.reshape(B, S, H, dv))
    return o.reshape(B, S, H * dv) @ w_out


def hgrn2_mixer(h, layer, w_in, lb_table, norm_g, w_out):
    B, S, _ = h.shape
    H, dF, dI = HGRN_HEADS, HGRN_DF, HGRN_DI
    q, v, gate, f_logit = jnp.split(h @ w_in, [HGRN_F, HGRN_F + H * dI, HGRN_F + 2 * H * dI], axis=-1)
    lb_w = jax.nn.softmax(lb_table.astype(jnp.float32), axis=1)
    lb = (jnp.cumsum(lb_w, axis=1) - lb_w[:, :1])[:, layer]
    f_logit = f_logit.astype(jnp.float32).reshape(B, S, 2, HGRN_F)
    logf = jnp.logaddexp(jnp.log(lb), jnp.log1p(-lb) + jax.nn.log_sigmoid(f_logit))
    k = -jnp.expm1(logf)
    heads = lambda t, d: t.reshape(B, S, H, d).transpose(0, 2, 1, 3)
    qh, vh = heads(q, dF), heads(v, dI)
    o_fwd = hgrn2_chunk(qh, heads(k[:, :, 0], dF), vh, heads(logf[:, :, 0], dF))
    o_bwd = _flip(hgrn2_chunk(_flip(qh), _flip(heads(k[:, :, 1], dF)), _flip(vh),
                              _flip(heads(logf[:, :, 1], dF))))
    o = (o_fwd + o_bwd).astype(h.dtype).transpose(0, 2, 1, 3)
    o = rms_norm(o, norm_g) * jax.nn.silu(gate.reshape(B, S, H, dI))
    return o.reshape(B, S, H * dI) @ w_out


def shortconv_mixer(h, w_in, conv_w, w_out):
    gb, gc, u = jnp.split(h @ w_in, 3, axis=-1)
    return (gb * centred_depthwise_conv(gc * u, conv_w)) @ w_out


def rope(x, pos):
    half = x.shape[-1] // 2
    inv = ROPE_THETA ** (-jnp.arange(half, dtype=jnp.float32) / half)
    ang = pos.astype(jnp.float32)[:, :, None, None] * inv
    cos, sin = jnp.cos(ang), jnp.sin(ang)
    xf = x.astype(jnp.float32)
    x1, x2 = xf[..., :half], xf[..., half:]
    return jnp.concatenate([x1 * cos - x2 * sin, x1 * sin + x2 * cos], axis=-1).astype(x.dtype)


def mla_attention(q_nope, q_rope, k_nope, k_rope, v):
    B, S, H, _ = q_nope.shape
    nb = S // Q_BLOCK
    blocks = lambda t: jnp.moveaxis(t.reshape(B, nb, Q_BLOCK, *t.shape[2:]), 1, 0)
    scale = (MLA_NOPE + MLA_ROPE) ** -0.5

    def one_block(qs):
        qn, qr = qs
        s = (jnp.einsum('bqhd,bkhd->bhqk', qn, k_nope)
             + jnp.einsum('bqhr,bkr->bhqk', qr, k_rope)).astype(jnp.float32) * scale
        p = jax.nn.softmax(s, axis=-1).astype(v.dtype)
        return jnp.einsum('bhqk,bkhd->bqhd', p, v)

    o = lax.map(one_block, (blocks(q_nope), blocks(q_rope)))
    return jnp.moveaxis(o, 0, 1).reshape(B, S, H, MLA_V)


def mla_mixer(h, pos, w_in, q_norm, w_uq, kv_norm, w_ukv, w_o):
    B, S, _ = h.shape
    H = MLA_HEADS
    cq, ckv, kr = jnp.split(h @ w_in, [MLA_Q_LORA, MLA_Q_LORA + MLA_KV_LORA], axis=-1)
    q = (rms_norm(cq, q_norm) @ w_uq).reshape(B, S, H, MLA_NOPE + MLA_ROPE)
    q_nope, q_rope = q[..., :MLA_NOPE], rope(q[..., MLA_NOPE:], pos)
    kv = (rms_norm(ckv, kv_norm) @ w_ukv).reshape(B, S, H, MLA_NOPE + MLA_V)
    k_nope, v = kv[..., :MLA_NOPE], kv[..., MLA_NOPE:]
    k_rope = rope(kr[:, :, None, :], pos)[:, :, 0, :]
    o = mla_attention(q_nope, q_rope, k_nope, k_rope, v)
    return o.reshape(B, S, H * MLA_V) @ w_o


def expert_choice_ffn(h, w_router, w_gate, w_up, w_down):
    B, S, _ = h.shape
    cap = CAPACITY_FACTOR * S // N_EXPERTS
    aff = jax.nn.softmax(jnp.einsum('bsd,de->bse', h, w_router).astype(jnp.float32), axis=-1)
    gate, idx = lax.top_k(jnp.swapaxes(aff, 1, 2), cap)
    bi = jnp.arange(B)[:, None, None]
    xs = h[bi, idx]
    hid = jax.nn.silu(jnp.einsum('becd,edf->becf', xs, w_gate)) * jnp.einsum('becd,edf->becf', xs, w_up)
    ys = jnp.einsum('becf,efd->becd', hid, w_down) * gate[..., None].astype(h.dtype)
    return jnp.zeros_like(h).at[bi, idx].add(ys)


def setup_inputs(seed: int = 0) -> dict:
    key = jax.random.key(seed)
    ks = iter(jax.random.split(key, 48))
    f32 = jnp.float32
    nrm = lambda shape, scale: jax.random.normal(next(ks), shape, f32) * scale
    gain = lambda shape: 1.0 + nrm(shape, 0.02)
    nA, nB, nC, nD = (len(range(m, DEPTH, N_MIXERS)) for m in range(N_MIXERS))
    D = D_MODEL
    gdn_cols = 2 * GDN_HEADS * GDN_DK + 2 * GDN_HEADS * GDN_DV + 4 * GDN_HEADS
    dt = jnp.exp(jax.random.uniform(next(ks), (nA, 2, GDN_HEADS), f32,
                                    minval=math.log(1e-3), maxval=math.log(1e-1)))
    return {
        'x': nrm((BATCH, SEQ, D), 1.0),
        'positions': (jnp.arange(SEQ, dtype=jnp.int32)[None, :]
                      + jax.random.randint(next(ks), (BATCH, 1), 0, 1024, dtype=jnp.int32)),
        'norm_mix': gain((DEPTH, D)),
        'norm_ffn': gain((DEPTH, D)),
        'norm_final': gain((D,)),
        'gdn_w_in': nrm((nA, D, gdn_cols), D ** -0.5),
        'gdn_conv': nrm((nA, GDN_CONV, 2 * GDN_HEADS * GDN_DK + GDN_HEADS * GDN_DV), GDN_CONV ** -0.5),
        'gdn_A_log': jnp.log(jax.random.uniform(next(ks), (nA, 2, GDN_HEADS), f32, minval=1.0, maxval=16.0)),
        'gdn_dt_bias': dt + jnp.log(-jnp.expm1(-dt)),
        'gdn_norm': gain((nA, GDN_DV)),
        'gdn_w_out': nrm((nA, GDN_HEADS * GDN_DV, D), (GDN_HEADS * GDN_DV) ** -0.5),
        'hgrn_w_in': nrm((nB, D, 2 * HGRN_F + 2 * HGRN_HEADS * HGRN_DI + HGRN_F), D ** -0.5),
        'hgrn_lb': nrm((2, DEPTH, HGRN_F), 0.5),
        'hgrn_norm': gain((nB, HGRN_DI)),
        'hgrn_w_out': nrm((nB, HGRN_HEADS * HGRN_DI, D), (HGRN_HEADS * HGRN_DI) ** -0.5),
        'sc_w_in': nrm((nC, D, 3 * D), D ** -0.5),
        'sc_conv': nrm((nC, SC_WIDTH, D), SC_WIDTH ** -0.5),
        'sc_w_out': nrm((nC, D, D), D ** -0.5),
        'mla_w_in': nrm((nD, D, MLA_Q_LORA + MLA_KV_LORA + MLA_ROPE), D ** -0.5),
        'mla_q_norm': gain((nD, MLA_Q_LORA)),
        'mla_w_uq': nrm((nD, MLA_Q_LORA, MLA_HEADS * (MLA_NOPE + MLA_ROPE)), MLA_Q_LORA ** -0.5),
        'mla_kv_norm': gain((nD, MLA_KV_LORA)),
        'mla_w_ukv': nrm((nD, MLA_KV_LORA, MLA_HEADS * (MLA_NOPE + MLA_V)), MLA_KV_LORA ** -0.5),
        'mla_w_o': nrm((nD, MLA_HEADS * MLA_V, D), (MLA_HEADS * MLA_V) ** -0.5),
        'moe_router': nrm((DEPTH, D, N_EXPERTS), D ** -0.5),
        'moe_w_gate': nrm((DEPTH, N_EXPERTS, D, D_EXPERT), D ** -0.5),
        'moe_w_up': nrm((DEPTH, N_EXPERTS, D, D_EXPERT), D ** -0.5),
        'moe_w_down': nrm((DEPTH, N_EXPERTS, D_EXPERT, D), D_EXPERT ** -0.5),
    }


def reference(x, positions, norm_mix, norm_ffn, norm_final,
              gdn_w_in, gdn_conv, gdn_A_log, gdn_dt_bias, gdn_norm, gdn_w_out,
              hgrn_w_in, hgrn_lb, hgrn_norm, hgrn_w_out,
              sc_w_in, sc_conv, sc_w_out,
              mla_w_in, mla_q_norm, mla_w_uq, mla_kv_norm, mla_w_ukv, mla_w_o,
              moe_router, moe_w_gate, moe_w_up, moe_w_down):
    for i in range(DEPTH):
        m, j = i % N_MIXERS, i // N_MIXERS
        hn = rms_norm(x, norm_mix[i])
        if m == 0:
            y = gdn_mixer(hn, gdn_w_in[j], gdn_conv[j], gdn_A_log[j], gdn_dt_bias[j], gdn_norm[j], gdn_w_out[j])
        elif m == 1:
            y = hgrn2_mixer(hn, i, hgrn_w_in[j], hgrn_lb, hgrn_norm[j], hgrn_w_out[j])
        elif m == 2:
            y = shortconv_mixer(hn, sc_w_in[j], sc_conv[j], sc_w_out[j])
        else:
            y = mla_mixer(hn, positions, mla_w_in[j], mla_q_norm[j], mla_w_uq[j],
                          mla_kv_norm[j], mla_w_ukv[j], mla_w_o[j])
        x = x + y
        x = x + expert_choice_ffn(rms_norm(x, norm_ffn[i]), moe_router[i], moe_w_gate[i],
                                  moe_w_up[i], moe_w_down[i])
    return rms_norm(x, norm_final)
```

```python
import functools
import math

import jax
import jax.numpy as jnp
from jax import lax
from jax.experimental import pallas as pl
from jax.experimental.pallas import tpu as pltpu

D_MODEL = 1024
BATCH = 4
SEQ = 8192
DEPTH = 4
N_MIXERS = 4
EPS = 1e-6
GDN_HEADS = 8
GDN_DK = 128
GDN_DV = 128
GDN_CONV = 5
GDN_CHUNK = 64
HGRN_EXPAND = 128
HGRN_HEADS = D_MODEL // HGRN_EXPAND
HGRN_DF = HGRN_EXPAND
HGRN_DI = D_MODEL // HGRN_HEADS
HGRN_F = HGRN_HEADS * HGRN_DF
HGRN_CHUNK = 64
SC_WIDTH = 3
MLA_HEADS = 16
MLA_NOPE = 64
MLA_ROPE = 32
MLA_V = 64
MLA_Q_LORA = 384
MLA_KV_LORA = 256
ROPE_THETA = 10000.0
Q_BLOCK = 128
N_EXPERTS = 16
D_EXPERT = 2048
CAPACITY_FACTOR = 2


def rms_norm(x, g):
    xf = x.astype(jnp.float32)
    y = xf * lax.rsqrt(jnp.mean(xf * xf, axis=-1, keepdims=True) + EPS)
    return (y * g.astype(jnp.float32)).astype(x.dtype)


def l2_normalize(x):
    xf = x.astype(jnp.float32)
    return xf * lax.rsqrt(jnp.sum(xf * xf, axis=-1, keepdims=True) + EPS)


def centred_depthwise_conv(x, w):
    K, C = w.shape
    return lax.conv_general_dilated(
        x, w[:, None, :], window_strides=(1,), padding=[(K // 2, K // 2)],
        dimension_numbers=('NWC', 'WIO', 'NWC'), feature_group_count=C)


def gated_delta_chunk(q, k, v, g, beta):
    f32 = jnp.float32
    B, H, S, dk = q.shape
    dv = v.shape[-1]
    C = GDN_CHUNK
    N = S // C
    q = q.astype(f32).reshape(B, H, N, C, dk) * (dk ** -0.5)
    k = k.astype(f32).reshape(B, H, N, C, dk)
    v = v.astype(f32).reshape(B, H, N, C, dv)
    g = jnp.cumsum(g.astype(f32).reshape(B, H, N, C), axis=-1)
    beta = beta.astype(f32).reshape(B, H, N, C)
    tril = jnp.tril(jnp.ones((C, C), bool))
    eye = jnp.eye(C, dtype=f32)
    decay = jnp.exp(jnp.where(tril, g[..., :, None] - g[..., None, :], -jnp.inf))
    kb = k * beta[..., None]
    lower = jnp.einsum('bhnid,bhnjd->bhnij', kb, k) * decay * (1.0 - eye)
    T = lax.linalg.triangular_solve(eye + lower, jnp.broadcast_to(eye, lower.shape),
                                    left_side=True, lower=True, unit_diagonal=True)
    u = T @ (v * beta[..., None])
    w = T @ (kb * jnp.exp(g)[..., None])
    intra = jnp.einsum('bhnid,bhnjd->bhnij', q, k) * decay
    q_dec = q * jnp.exp(g)[..., None]
    k_dec = k * jnp.exp(g[..., -1:] - g)[..., None]
    g_last = jnp.exp(g[..., -1])

    def step(state, inp):
        intra_c, u_c, w_c, qd_c, kd_c, gl_c = inp
        v_new = u_c - w_c @ state
        o = qd_c @ state + intra_c @ v_new
        state = state * gl_c[..., None, None] + jnp.swapaxes(kd_c, -1, -2) @ v_new
        return state, o

    xs = tuple(jnp.moveaxis(t, 2, 0) for t in (intra, u, w, q_dec, k_dec, g_last))
    _, o = lax.scan(step, jnp.zeros((B, H, dk, dv), f32), xs)
    return jnp.moveaxis(o, 0, 2).reshape(B, H, S, dv)


def hgrn2_chunk(q, k, v, logf):
    f32 = jnp.float32
    B, H, S, dF = q.shape
    dI = v.shape[-1]
    C = HGRN_CHUNK
    N = S // C
    to_chunks = lambda t: jnp.moveaxis(t.astype(f32).reshape(B, H, N, C, t.shape[-1]), 2, 0)
    q, k, v, logf = (to_chunks(t) for t in (q, k, v, logf))
    b = jnp.cumsum(logf, axis=-2)
    tril = jnp.tril(jnp.ones((C, C), bool))[:, :, None]

    def step(state, inp):
        q_c, k_c, v_c, b_c = inp
        rel = jnp.exp(jnp.where(tril, b_c[..., :, None, :] - b_c[..., None, :, :], -jnp.inf))
        scores = jnp.einsum('bhid,bhijd->bhij', q_c, rel * k_c[..., None, :, :])
        b_last = b_c[..., -1:, :]
        o = jnp.einsum('bhid,bhde->bhie', q_c * jnp.exp(b_c), state) + scores @ v_c
        state = (state * jnp.exp(b_last)[..., 0, :, None]
                 + jnp.einsum('bhjd,bhje->bhde', k_c * jnp.exp(b_last - b_c), v_c))
        return state, o

    _, o = lax.scan(step, jnp.zeros((B, H, dF, dI), f32), (q, k, v, b))
    return jnp.moveaxis(o, 0, 2).reshape(B, H, S, dI)


def _flip(t):
    return jnp.flip(t, axis=2)


def gdn_mixer(h, w_in, conv_w, A_log, dt_bias, norm_g, w_out):
    B, S, _ = h.shape
    H, dk, dv = GDN_HEADS, GDN_DK, GDN_DV
    qkv, z, a, b = jnp.split(h @ w_in, [2 * H * dk + H * dv, 2 * H * dk + 2 * H * dv,
                                         2 * H * dk + 2 * H * dv + 2 * H], axis=-1)
    qkv = jax.nn.silu(centred_depthwise_conv(qkv, conv_w))
    q, k, v = jnp.split(qkv, [H * dk, 2 * H * dk], axis=-1)
    heads = lambda t, d: t.reshape(B, S, H, d).transpose(0, 2, 1, 3)
    q, k, v = l2_normalize(heads(q, dk)), l2_normalize(heads(k, dk)), heads(v, dv)
    a = a.astype(jnp.float32).reshape(B, S, 2, H).transpose(2, 0, 3, 1)
    b = b.astype(jnp.float32).reshape(B, S, 2, H).transpose(2, 0, 3, 1)
    g = -jnp.exp(A_log.astype(jnp.float32))[:, None, :, None] * jax.nn.softplus(
        a + dt_bias.astype(jnp.float32)[:, None, :, None])
    beta = jax.nn.sigmoid(b)
    o_fwd = gated_delta_chunk(q, k, v, g[0], beta[0])
    o_bwd = _flip(gated_delta_chunk(_flip(q), _flip(k), _flip(v), _flip(g[1]), _flip(beta[1])))
    o = (o_fwd + o_bwd).astype(h.dtype).transpose(0, 2, 1, 3)
    o = rms_norm(o, norm_g) * jax.nn.silu(z.reshape(B, S, H, dv))
    return o.reshape(B, S, H * dv) @ w_out


def hgrn2_mixer(h, layer, w_in, lb_table, norm_g, w_out):
    B, S, _ = h.shape
    H, dF, dI = HGRN_HEADS, HGRN_DF, HGRN_DI
    q, v, gate, f_logit = jnp.split(h @ w_in, [HGRN_F, HGRN_F + H * dI, HGRN_F + 2 * H * dI], axis=-1)
    lb_w = jax.nn.softmax(lb_table.astype(jnp.float32), axis=1)
    lb = (jnp.cumsum(lb_w, axis=1) - lb_w[:, :1])[:, layer]
    f_logit = f_logit.astype(jnp.float32).reshape(B, S, 2, HGRN_F)
    logf = jnp.logaddexp(jnp.log(lb), jnp.log1p(-lb) + jax.nn.log_sigmoid(f_logit))
    k = -jnp.expm1(logf)
    heads = lambda t, d: t.reshape(B, S, H, d).transpose(0, 2, 1, 3)
    qh, vh = heads(q, dF), heads(v, dI)
    o_fwd = hgrn2_chunk(qh, heads(k[:, :, 0], dF), vh, heads(logf[:, :, 0], dF))
    o_bwd = _flip(hgrn2_chunk(_flip(qh), _flip(heads(k[:, :, 1], dF)), _flip(vh),
                              _flip(heads(logf[:, :, 1], dF))))
    o = (o_fwd + o_bwd).astype(h.dtype).transpose(0, 2, 1, 3)
    o = rms_norm(o, norm_g) * jax.nn.silu(gate.reshape(B, S, H, dI))
    return o.reshape(B, S, H * dI) @ w_out


def shortconv_mixer(h, w_in, conv_w, w_out):
    gb, gc, u = jnp.split(h @ w_in, 3, axis=-1)
    return (gb * centred_depthwise_conv(gc * u, conv_w)) @ w_out


def rope(x, pos):
    half = x.shape[-1] // 2
    inv = ROPE_THETA ** (-jnp.arange(half, dtype=jnp.float32) / half)
    ang = pos.astype(jnp.float32)[:, :, None, None] * inv
    cos, sin = jnp.cos(ang), jnp.sin(ang)
    xf = x.astype(jnp.float32)
    x1, x2 = xf[..., :half], xf[..., half:]
    return jnp.concatenate([x1 * cos - x2 * sin, x1 * sin + x2 * cos], axis=-1).astype(x.dtype)


def mla_attention(q_nope, q_rope, k_nope, k_rope, v):
    B, S, H, _ = q_nope.shape
    nb = S // Q_BLOCK
    blocks = lambda t: jnp.moveaxis(t.reshape(B, nb, Q_BLOCK, *t.shape[2:]), 1, 0)
    scale = (MLA_NOPE + MLA_ROPE) ** -0.5

    def one_block(qs):
        qn, qr = qs
        s = (jnp.einsum('bqhd,bkhd->bhqk', qn, k_nope)
             + jnp.einsum('bqhr,bkr->bhqk', qr, k_rope)).astype(jnp.float32) * scale
        p = jax.nn.softmax(s, axis=-1).astype(v.dtype)
        return jnp.einsum('bhqk,bkhd->bqhd', p, v)

    o = lax.map(one_block, (blocks(q_nope), blocks(q_rope)))
    return jnp.moveaxis(o, 0, 1).reshape(B, S, H, MLA_V)


def mla_mixer(h, pos, w_in, q_norm, w_uq, kv_norm, w_ukv, w_o):
    B, S, _ = h.shape
    H = MLA_HEADS
    cq, ckv, kr = jnp.split(h @ w_in, [MLA_Q_LORA, MLA_Q_LORA + MLA_KV_LORA], axis=-1)
    q = (rms_norm(cq, q_norm) @ w_uq).reshape(B, S, H, MLA_NOPE + MLA_ROPE)
    q_nope, q_rope = q[..., :MLA_NOPE], rope(q[..., MLA_NOPE:], pos)
    kv = (rms_norm(ckv, kv_norm) @ w_ukv).reshape(B, S, H, MLA_NOPE + MLA_V)
    k_nope, v = kv[..., :MLA_NOPE], kv[..., MLA_NOPE:]
    k_rope = rope(kr[:, :, None, :], pos)[:, :, 0, :]
    o = mla_attention(q_nope, q_rope, k_nope, k_rope, v)
    return o.reshape(B, S, H * MLA_V) @ w_o


def expert_choice_ffn(h, w_router, w_gate, w_up, w_down):
    B, S, _ = h.shape
    cap = CAPACITY_FACTOR * S // N_EXPERTS
    aff = jax.nn.softmax(jnp.einsum('bsd,de->bse', h, w_router).astype(jnp.float32), axis=-1)
    gate, idx = lax.top_k(jnp.swapaxes(aff, 1, 2), cap)
    bi = jnp.arange(B)[:, None, None]
    xs = h[bi, idx]
    hid = jax.nn.silu(jnp.einsum('becd,edf->becf', xs, w_gate)) * jnp.einsum('becd,edf->becf', xs, w_up)
    ys = jnp.einsum('becf,efd->becd', hid, w_down) * gate[..., None].astype(h.dtype)
    return jnp.zeros_like(h).at[bi, idx].add(ys)


def _final_norm_kernel(x_ref, g_ref, o_ref):
    x = x_ref[...]
    y = x * lax.rsqrt(jnp.mean(x * x, axis=-1, keepdims=True) + EPS)
    o_ref[...] = y * g_ref[...]


def final_norm(x, g):
    B, S, D = x.shape
    x2 = x.reshape(B * S, D)
    tm = 1024
    out = pl.pallas_call(
        _final_norm_kernel,
        grid=(B * S // tm,),
        in_specs=[pl.BlockSpec((tm, D), lambda i: (i, 0)),
                  pl.BlockSpec((1, D), lambda i: (0, 0))],
        out_specs=pl.BlockSpec((tm, D), lambda i: (i, 0)),
        out_shape=jax.ShapeDtypeStruct((B * S, D), x.dtype),
        name="final_norm",
    )(x2, g.reshape(1, D))
    return out.reshape(B, S, D)


def kernel(x, positions, norm_mix, norm_ffn, norm_final,
           gdn_w_in, gdn_conv, gdn_A_log, gdn_dt_bias, gdn_norm, gdn_w_out,
           hgrn_w_in, hgrn_lb, hgrn_norm, hgrn_w_out,
           sc_w_in, sc_conv, sc_w_out,
           mla_w_in, mla_q_norm, mla_w_uq, mla_kv_norm, mla_w_ukv, mla_w_o,
           moe_router, moe_w_gate, moe_w_up, moe_w_down):
    for i in range(DEPTH):
        m, j = i % N_MIXERS, i // N_MIXERS
        hn = rms_norm(x, norm_mix[i])
        if m == 0:
            y = gdn_mixer(hn, gdn_w_in[j], gdn_conv[j], gdn_A_log[j], gdn_dt_bias[j], gdn_norm[j], gdn_w_out[j])
        elif m == 1:
            y = hgrn2_mixer(hn, i, hgrn_w_in[j], hgrn_lb, hgrn_norm[j], hgrn_w_out[j])
        elif m == 2:
            y = shortconv_mixer(hn, sc_w_in[j], sc_conv[j], sc_w_out[j])
        else:
            y = mla_mixer(hn, positions, mla_w_in[j], mla_q_norm[j], mla_w_uq[j],
                          mla_kv_norm[j], mla_w_ukv[j], mla_w_o[j])
        x = x + y
        x = x + expert_choice_ffn(rms_norm(x, norm_ffn[i]), moe_router[i], moe_w_gate[i],
                                  moe_w_up[i], moe_w_down[i])
    return final_norm(x, norm_final)
```

```python
import functools
import math

import jax
import jax.numpy as jnp
from jax import lax
from jax.experimental import pallas as pl
from jax.experimental.pallas import tpu as pltpu

D_MODEL = 1024
BATCH = 4
SEQ = 8192
DEPTH = 4
N_MIXERS = 4
EPS = 1e-6
GDN_HEADS = 8
GDN_DK = 128
GDN_DV = 128
GDN_CONV = 5
GDN_CHUNK = 64
HGRN_EXPAND = 128
HGRN_HEADS = D_MODEL // HGRN_EXPAND
HGRN_DF = HGRN_EXPAND
HGRN_DI = D_MODEL // HGRN_HEADS
HGRN_F = HGRN_HEADS * HGRN_DF
HGRN_CHUNK = 64
SC_WIDTH = 3
MLA_HEADS = 16
MLA_NOPE = 64
MLA_ROPE = 32
MLA_V = 64
MLA_Q_LORA = 384
MLA_KV_LORA = 256
ROPE_THETA = 10000.0
Q_BLOCK = 128
N_EXPERTS = 16
D_EXPERT = 2048
CAPACITY_FACTOR = 2


def rms_norm(x, g):
    xf = x.astype(jnp.float32)
    y = xf * lax.rsqrt(jnp.mean(xf * xf, axis=-1, keepdims=True) + EPS)
    return (y * g.astype(jnp.float32)).astype(x.dtype)


def l2_normalize(x):
    xf = x.astype(jnp.float32)
    return xf * lax.rsqrt(jnp.sum(xf * xf, axis=-1, keepdims=True) + EPS)


def centred_depthwise_conv(x, w):
    K, C = w.shape
    return lax.conv_general_dilated(
        x, w[:, None, :], window_strides=(1,), padding=[(K // 2, K // 2)],
        dimension_numbers=('NWC', 'WIO', 'NWC'), feature_group_count=C)


def gated_delta_chunk(q, k, v, g, beta):
    f32 = jnp.float32
    B, H, S, dk = q.shape
    dv = v.shape[-1]
    C = GDN_CHUNK
    N = S // C
    q = q.astype(f32).reshape(B, H, N, C, dk) * (dk ** -0.5)
    k = k.astype(f32).reshape(B, H, N, C, dk)
    v = v.astype(f32).reshape(B, H, N, C, dv)
    g = jnp.cumsum(g.astype(f32).reshape(B, H, N, C), axis=-1)
    beta = beta.astype(f32).reshape(B, H, N, C)
    tril = jnp.tril(jnp.ones((C, C), bool))
    eye = jnp.eye(C, dtype=f32)
    decay = jnp.exp(jnp.where(tril, g[..., :, None] - g[..., None, :], -jnp.inf))
    kb = k * beta[..., None]
    lower = jnp.einsum('bhnid,bhnjd->bhnij', kb, k) * decay * (1.0 - eye)
    T = lax.linalg.triangular_solve(eye + lower, jnp.broadcast_to(eye, lower.shape),
                                    left_side=True, lower=True, unit_diagonal=True)
    u = T @ (v * beta[..., None])
    w = T @ (kb * jnp.exp(g)[..., None])
    intra = jnp.einsum('bhnid,bhnjd->bhnij', q, k) * decay
    q_dec = q * jnp.exp(g)[..., None]
    k_dec = k * jnp.exp(g[..., -1:] - g)[..., None]
    g_last = jnp.exp(g[..., -1])

    def step(state, inp):
        intra_c, u_c, w_c, qd_c, kd_c, gl_c = inp
        v_new = u_c - w_c @ state
        o = qd_c @ state + intra_c @ v_new
        state = state * gl_c[..., None, None] + jnp.swapaxes(kd_c, -1, -2) @ v_new
        return state, o

    xs = tuple(jnp.moveaxis(t, 2, 0) for t in (intra, u, w, q_dec, k_dec, g_last))
    _, o = lax.scan(step, jnp.zeros((B, H, dk, dv), f32), xs)
    return jnp.moveaxis(o, 0, 2).reshape(B, H, S, dv)


def hgrn2_chunk(q, k, v, logf):
    f32 = jnp.float32
    B, H, S, dF = q.shape
    dI = v.shape[-1]
    C = HGRN_CHUNK
    N = S // C
    to_chunks = lambda t: jnp.moveaxis(t.astype(f32).reshape(B, H, N, C, t.shape[-1]), 2, 0)
    q, k, v, logf = (to_chunks(t) for t in (q, k, v, logf))
    b = jnp.cumsum(logf, axis=-2)
    tril = jnp.tril(jnp.ones((C, C), bool))[:, :, None]

    def step(state, inp):
        q_c, k_c, v_c, b_c = inp
        rel = jnp.exp(jnp.where(tril, b_c[..., :, None, :] - b_c[..., None, :, :], -jnp.inf))
        scores = jnp.einsum('bhid,bhijd->bhij', q_c, rel * k_c[..., None, :, :])
        b_last = b_c[..., -1:, :]
        o = jnp.einsum('bhid,bhde->bhie', q_c * jnp.exp(b_c), state) + scores @ v_c
        state = (state * jnp.exp(b_last)[..., 0, :, None]
                 + jnp.einsum('bhjd,bhje->bhde', k_c * jnp.exp(b_last - b_c), v_c))
        return state, o

    _, o = lax.scan(step, jnp.zeros((B, H, dF, dI), f32), (q, k, v, b))
    return jnp.moveaxis(o, 0, 2).reshape(B, H, S, dI)


def _flip(t):
    return jnp.flip(t, axis=2)


def gdn_mixer(h, w_in, conv_w, A_log, dt_bias, norm_g, w_out):
    B, S, _ = h.shape
    H, dk, dv = GDN_HEADS, GDN_DK, GDN_DV
    qkv, z, a, b = jnp.split(h @ w_in, [2 * H * dk + H * dv, 2 * H * dk + 2 * H * dv,
                                         2 * H * dk + 2 * H * dv + 2 * H], axis=-1)
    qkv = jax.nn.silu(centred_depthwise_conv(qkv, conv_w))
    q, k, v = jnp.split(qkv, [H * dk, 2 * H * dk], axis=-1)
    heads = lambda t, d: t.reshape(B, S, H, d).transpose(0, 2, 1, 3)
    q, k, v = l2_normalize(heads(q, dk)), l2_normalize(heads(k, dk)), heads(v, dv)
    a = a.astype(jnp.float32).reshape(B, S, 2, H).transpose(2, 0, 3, 1)
    b = b.astype(jnp.float32).reshape(B, S, 2, H).transpose(2, 0, 3, 1)
    g = -jnp.exp(A_log.astype(jnp.float32))[:, None, :, None] * jax.nn.softplus(
        a + dt_bias.astype(jnp.float32)[:, None, :, None])
    beta = jax.nn.sigmoid(b)
    o_fwd = gated_delta_chunk(q, k, v, g[0], beta[0])
    o_bwd = _flip(gated_delta_chunk(_flip(q), _flip(k), _flip(v), _flip(g[1]), _flip(beta[1])))
    o = (o_fwd + o_bwd).astype(h.dtype).transpose(0, 2, 1, 3)
    o = rms_norm(o, norm_g) * jax.nn.silu(z.reshape(B, S, H, dv))
    return o.reshape(B, S, H * dv) @ w_out


def hgrn2_mixer(h, layer, w_in, lb_table, norm_g, w_out):
    B, S, _ = h.shape
    H, dF, dI = HGRN_HEADS, HGRN_DF, HGRN_DI
    q, v, gate, f_logit = jnp.split(h @ w_in, [HGRN_F, HGRN_F + H * dI, HGRN_F + 2 * H * dI], axis=-1)
    lb_w = jax.nn.softmax(lb_table.astype(jnp.float32), axis=1)
    lb = (jnp.cumsum(lb_w, axis=1) - lb_w[:, :1])[:, layer]
    f_logit = f_logit.astype(jnp.float32).reshape(B, S, 2, HGRN_F)
    logf = jnp.logaddexp(jnp.log(lb), jnp.log1p(-lb) + jax.nn.log_sigmoid(f_logit))
    k = -jnp.expm1(logf)
    heads = lambda t, d: t.reshape(B, S, H, d).transpose(0, 2, 1, 3)
    qh, vh = heads(q, dF), heads(v, dI)
    o_fwd = hgrn2_chunk(qh, heads(k[:, :, 0], dF), vh, heads(logf[:, :, 0], dF))
    o_bwd = _flip(hgrn2_chunk(_flip(qh), _flip(heads(k[:, :, 1], dF)), _flip(vh),
                              _flip(heads(logf[:, :, 1], dF))))
    o = (o_fwd + o_bwd).astype(h.dtype).transpose(0, 2, 1, 3)
    o = rms_norm(o, norm_g) * jax.nn.silu(gate.reshape(B, S, H, dI))
    return o.reshape(B, S, H * dI) @ w_out


def shortconv_mixer(h, w_in, conv_w, w_out):
    gb, gc, u = jnp.split(h @ w_in, 3, axis=-1)
    return (gb * centred_depthwise_conv(gc * u, conv_w)) @ w_out


def rope(x, pos):
    half = x.shape[-1] // 2
    inv = ROPE_THETA ** (-jnp.arange(half, dtype=jnp.float32) / half)
    ang = pos.astype(jnp.float32)[:, :, None, None] * inv
    cos, sin = jnp.cos(ang), jnp.sin(ang)
    xf = x.astype(jnp.float32)
    x1, x2 = xf[..., :half], xf[..., half:]
    return jnp.concatenate([x1 * cos - x2 * sin, x1 * sin + x2 * cos], axis=-1).astype(x.dtype)


MLA_QK_PAD = 128
MLA_TQ = 512
MLA_TK = 1024


def _mla_attn_kernel(q_ref, k_ref, v_ref, o_ref, *, tk):
    tq = q_ref.shape[1]
    nk = k_ref.shape[1] // tk
    head_lanes = [slice(hh * MLA_QK_PAD, (hh + 1) * MLA_QK_PAD) for hh in range(2)]
    qs = [q_ref[0, :, lanes] for lanes in head_lanes]

    def body(j, carry):
        rows = pl.ds(pl.multiple_of(j * tk, tk), tk)
        v = v_ref[0, rows, :]
        new = []
        for hh in range(2):
            m, l, acc = carry[hh]
            k = k_ref[0, rows, head_lanes[hh]]
            s = lax.dot_general(qs[hh], k, (((1,), (1,)), ((), ())), preferred_element_type=jnp.float32)
            m_new = jnp.maximum(m, jnp.max(s, axis=-1, keepdims=True))
            p = jnp.exp(s - m_new)
            alpha = jnp.exp(m - m_new)
            l = alpha * l + jnp.sum(p, axis=-1, keepdims=True)
            acc = alpha * acc + jnp.dot(p.astype(jnp.bfloat16), v, preferred_element_type=jnp.float32)
            new.append((m_new, l, acc))
        return tuple(new)

    init = (jnp.full((tq, 1), -jnp.inf, jnp.float32), jnp.zeros((tq, 1), jnp.float32),
            jnp.zeros((tq, 2 * MLA_V), jnp.float32))
    (_, l0, acc0), (_, l1, acc1) = lax.fori_loop(0, nk, body, (init, init))
    lane = lax.broadcasted_iota(jnp.int32, (tq, 2 * MLA_V), 1)
    o_ref[0] = jnp.where(lane < MLA_V, acc0 / l0, acc1 / l1).astype(o_ref.dtype)


def mla_attention(q_nope, q_rope, k_nope, k_rope, v):
    B, S, H, _ = q_nope.shape
    scale = (MLA_NOPE + MLA_ROPE) ** -0.5
    pad = MLA_QK_PAD - MLA_NOPE - MLA_ROPE
    bf16 = jnp.bfloat16
    qf = jnp.concatenate([q_nope, q_rope, jnp.zeros((B, S, H, pad), q_nope.dtype)], axis=-1) * scale
    kf = jnp.concatenate([k_nope, jnp.broadcast_to(k_rope[:, :, None, :], (B, S, H, MLA_ROPE)),
                          jnp.zeros((B, S, H, pad), k_nope.dtype)], axis=-1)
    qf = qf.astype(bf16).reshape(B, S, H * MLA_QK_PAD)
    kf = kf.astype(bf16).reshape(B, S, H * MLA_QK_PAD)
    vf = v.astype(bf16).reshape(B, S, H * MLA_V)
    tq, tk = min(MLA_TQ, S), min(MLA_TK, S)
    out = pl.pallas_call(
        functools.partial(_mla_attn_kernel, tk=tk),
        grid=(B, H // 2, S // tq),
        in_specs=[pl.BlockSpec((1, tq, 2 * MLA_QK_PAD), lambda b, h, i: (b, i, h)),
                  pl.BlockSpec((1, S, 2 * MLA_QK_PAD), lambda b, h, i: (b, 0, h)),
                  pl.BlockSpec((1, S, 2 * MLA_V), lambda b, h, i: (b, 0, h))],
        out_specs=pl.BlockSpec((1, tq, 2 * MLA_V), lambda b, h, i: (b, i, h)),
        out_shape=jax.ShapeDtypeStruct((B, S, H * MLA_V), jnp.float32),
        compiler_params=pltpu.CompilerParams(
            dimension_semantics=("parallel", "parallel", "arbitrary"),
            vmem_limit_bytes=48 * 1024 * 1024),
        name="mla_attention",
    )(qf, kf, vf)
    return out.reshape(B, S, H, MLA_V)


def mla_mixer(h, pos, w_in, q_norm, w_uq, kv_norm, w_ukv, w_o):
    B, S, _ = h.shape
    H = MLA_HEADS
    cq, ckv, kr = jnp.split(h @ w_in, [MLA_Q_LORA, MLA_Q_LORA + MLA_KV_LORA], axis=-1)
    q = (rms_norm(cq, q_norm) @ w_uq).reshape(B, S, H, MLA_NOPE + MLA_ROPE)
    q_nope, q_rope = q[..., :MLA_NOPE], rope(q[..., MLA_NOPE:], pos)
    kv = (rms_norm(ckv, kv_norm) @ w_ukv).reshape(B, S, H, MLA_NOPE + MLA_V)
    k_nope, v = kv[..., :MLA_NOPE], kv[..., MLA_NOPE:]
    k_rope = rope(kr[:, :, None, :], pos)[:, :, 0, :]
    o = mla_attention(q_nope, q_rope, k_nope, k_rope, v)
    return o.reshape(B, S, H * MLA_V) @ w_o


def expert_choice_ffn(h, w_router, w_gate, w_up, w_down):
    B, S, _ = h.shape
    cap = CAPACITY_FACTOR * S // N_EXPERTS
    aff = jax.nn.softmax(jnp.einsum('bsd,de->bse', h, w_router).astype(jnp.float32), axis=-1)
    gate, idx = lax.top_k(jnp.swapaxes(aff, 1, 2), cap)
    bi = jnp.arange(B)[:, None, None]
    xs = h[bi, idx]
    ys = expert_ffn(xs, gate, w_gate, w_up, w_down)
    return jnp.zeros_like(h).at[bi, idx].add(ys)


FFN_BATCH_ROWS = 2
FFN_TF = 512


def _expert_ffn_kernel(x_ref, g_ref, wg_ref, wu_ref, wd_ref, o_ref):
    f = pl.program_id(2)
    bb, _, cap, d = x_ref.shape
    x = x_ref[...].reshape(bb * cap, d)
    bf16 = jnp.bfloat16
    a = jnp.dot(x, wg_ref[0].astype(bf16), preferred_element_type=jnp.float32)
    u = jnp.dot(x, wu_ref[0].astype(bf16), preferred_element_type=jnp.float32)
    hid = (a * jax.nn.sigmoid(a) * u).astype(bf16)
    y = jnp.dot(hid, wd_ref[0].astype(bf16), preferred_element_type=jnp.float32).reshape(bb, 1, cap, d)

    @pl.when(f == 0)
    def _():
        o_ref[...] = y

    @pl.when(f > 0)
    def _():
        o_ref[...] += y

    @pl.when(f == pl.num_programs(2) - 1)
    def _():
        o_ref[...] = o_ref[...] * g_ref[...]


def expert_ffn(xs, gate, w_gate, w_up, w_down):
    B, E, cap, D = xs.shape
    F = w_gate.shape[-1]
    bb = min(FFN_BATCH_ROWS, B)
    tf = min(FFN_TF, F)
    return pl.pallas_call(
        _expert_ffn_kernel,
        grid=(E, B // bb, F // tf),
        in_specs=[pl.BlockSpec((bb, 1, cap, D), lambda e, b, f: (b, e, 0, 0)),
                  pl.BlockSpec((bb, 1, cap, 1), lambda e, b, f: (b, e, 0, 0)),
                  pl.BlockSpec((1, D, tf), lambda e, b, f: (e, 0, f)),
                  pl.BlockSpec((1, D, tf), lambda e, b, f: (e, 0, f)),
                  pl.BlockSpec((1, tf, D), lambda e, b, f: (e, f, 0))],
        out_specs=pl.BlockSpec((bb, 1, cap, D), lambda e, b, f: (b, e, 0, 0)),
        out_shape=jax.ShapeDtypeStruct((B, E, cap, D), jnp.float32),
        compiler_params=pltpu.CompilerParams(
            dimension_semantics=("parallel", "parallel", "arbitrary"),
            vmem_limit_bytes=56 * 1024 * 1024),
        name="expert_ffn",
    )(xs.astype(jnp.bfloat16), gate[..., None].astype(jnp.float32), w_gate, w_up, w_down)


def _final_norm_kernel(x_ref, g_ref, o_ref):
    x = x_ref[...]
    y = x * lax.rsqrt(jnp.mean(x * x, axis=-1, keepdims=True) + EPS)
    o_ref[...] = y * g_ref[...]


def final_norm(x, g):
    B, S, D = x.shape
    x2 = x.reshape(B * S, D)
    tm = 1024
    out = pl.pallas_call(
        _final_norm_kernel,
        grid=(B * S // tm,),
        in_specs=[pl.BlockSpec((tm, D), lambda i: (i, 0)),
                  pl.BlockSpec((1, D), lambda i: (0, 0))],
        out_specs=pl.BlockSpec((tm, D), lambda i: (i, 0)),
        out_shape=jax.ShapeDtypeStruct((B * S, D), x.dtype),
        name="final_norm",
    )(x2, g.reshape(1, D))
    return out.reshape(B, S, D)


def kernel(x, positions, norm_mix, norm_ffn, norm_final,
           gdn_w_in, gdn_conv, gdn_A_log, gdn_dt_bias, gdn_norm, gdn_w_out,
           hgrn_w_in, hgrn_lb, hgrn_norm, hgrn_w_out,
           sc_w_in, sc_conv, sc_w_out,
           mla_w_in, mla_q_norm, mla_w_uq, mla_kv_norm, mla_w_ukv, mla_w_o,
           moe_router, moe_w_gate, moe_w_up, moe_w_down):
    for i in range(DEPTH):
        m, j = i % N_MIXERS, i // N_MIXERS
        hn = rms_norm(x, norm_mix[i])
        if m == 0:
            y = gdn_mixer(hn, gdn_w_in[j], gdn_conv[j], gdn_A_log[j], gdn_dt_bias[j], gdn_norm[j], gdn_w_out[j])
        elif m == 1:
            y = hgrn2_mixer(hn, i, hgrn_w_in[j], hgrn_lb, hgrn_norm[j], hgrn_w_out[j])
        elif m == 2:
            y = shortconv_mixer(hn, sc_w_in[j], sc_conv[j], sc_w_out[j])
        else:
            y = mla_mixer(hn, positions, mla_w_in[j], mla_q_norm[j], mla_w_uq[j],
                          mla_kv_norm[j], mla_w_ukv[j], mla_w_o[j])
        x = x + y
        x = x + expert_choice_ffn(rms_norm(x, norm_ffn[i]), moe_router[i], moe_w_gate[i],
                                  moe_w_up[i], moe_w_down[i])
    return final_norm(x, norm_final)
```

```python
import functools
import math

import jax
import jax.numpy as jnp
from jax import lax
from jax.experimental import pallas as pl
from jax.experimental.pallas import tpu as pltpu

D_MODEL = 1024
BATCH = 4
SEQ = 8192
DEPTH = 4
N_MIXERS = 4
EPS = 1e-6
GDN_HEADS = 8
GDN_DK = 128
GDN_DV = 128
GDN_CONV = 5
GDN_CHUNK = 64
HGRN_EXPAND = 128
HGRN_HEADS = D_MODEL // HGRN_EXPAND
HGRN_DF = HGRN_EXPAND
HGRN_DI = D_MODEL // HGRN_HEADS
HGRN_F = HGRN_HEADS * HGRN_DF
HGRN_CHUNK = 64
SC_WIDTH = 3
MLA_HEADS = 16
MLA_NOPE = 64
MLA_ROPE = 32
MLA_V = 64
MLA_Q_LORA = 384
MLA_KV_LORA = 256
ROPE_THETA = 10000.0
Q_BLOCK = 128
N_EXPERTS = 16
D_EXPERT = 2048
CAPACITY_FACTOR = 2


def rms_norm(x, g):
    xf = x.astype(jnp.float32)
    y = xf * lax.rsqrt(jnp.mean(xf * xf, axis=-1, keepdims=True) + EPS)
    return (y * g.astype(jnp.float32)).astype(x.dtype)


def l2_normalize(x):
    xf = x.astype(jnp.float32)
    return xf * lax.rsqrt(jnp.sum(xf * xf, axis=-1, keepdims=True) + EPS)


def centred_depthwise_conv(x, w):
    K, C = w.shape
    return lax.conv_general_dilated(
        x, w[:, None, :], window_strides=(1,), padding=[(K // 2, K // 2)],
        dimension_numbers=('NWC', 'WIO', 'NWC'), feature_group_count=C)


def gated_delta_chunk(q, k, v, g, beta):
    f32 = jnp.float32
    B, H, S, dk = q.shape
    dv = v.shape[-1]
    C = GDN_CHUNK
    N = S // C
    q = q.astype(f32).reshape(B, H, N, C, dk) * (dk ** -0.5)
    k = k.astype(f32).reshape(B, H, N, C, dk)
    v = v.astype(f32).reshape(B, H, N, C, dv)
    g = jnp.cumsum(g.astype(f32).reshape(B, H, N, C), axis=-1)
    beta = beta.astype(f32).reshape(B, H, N, C)
    tril = jnp.tril(jnp.ones((C, C), bool))
    eye = jnp.eye(C, dtype=f32)
    decay = jnp.exp(jnp.where(tril, g[..., :, None] - g[..., None, :], -jnp.inf))
    kb = k * beta[..., None]
    lower = jnp.einsum('bhnid,bhnjd->bhnij', kb, k) * decay * (1.0 - eye)
    T = lax.linalg.triangular_solve(eye + lower, jnp.broadcast_to(eye, lower.shape),
                                    left_side=True, lower=True, unit_diagonal=True)
    u = T @ (v * beta[..., None])
    w = T @ (kb * jnp.exp(g)[..., None])
    intra = jnp.einsum('bhnid,bhnjd->bhnij', q, k) * decay
    q_dec = q * jnp.exp(g)[..., None]
    k_dec = k * jnp.exp(g[..., -1:] - g)[..., None]
    g_last = jnp.exp(g[..., -1])

    def step(state, inp):
        intra_c, u_c, w_c, qd_c, kd_c, gl_c = inp
        v_new = u_c - w_c @ state
        o = qd_c @ state + intra_c @ v_new
        state = state * gl_c[..., None, None] + jnp.swapaxes(kd_c, -1, -2) @ v_new
        return state, o

    xs = tuple(jnp.moveaxis(t, 2, 0) for t in (intra, u, w, q_dec, k_dec, g_last))
    _, o = lax.scan(step, jnp.zeros((B, H, dk, dv), f32), xs)
    return jnp.moveaxis(o, 0, 2).reshape(B, H, S, dv)


def hgrn2_chunk(q, k, v, logf):
    f32 = jnp.float32
    B, H, S, dF = q.shape
    dI = v.shape[-1]
    C = HGRN_CHUNK
    N = S // C
    to_chunks = lambda t: jnp.moveaxis(t.astype(f32).reshape(B, H, N, C, t.shape[-1]), 2, 0)
    q, k, v, logf = (to_chunks(t) for t in (q, k, v, logf))
    b = jnp.cumsum(logf, axis=-2)
    tril = jnp.tril(jnp.ones((C, C), bool))[:, :, None]

    def step(state, inp):
        q_c, k_c, v_c, b_c = inp
        rel = jnp.exp(jnp.where(tril, b_c[..., :, None, :] - b_c[..., None, :, :], -jnp.inf))
        scores = jnp.einsum('bhid,bhijd->bhij', q_c, rel * k_c[..., None, :, :])
        b_last = b_c[..., -1:, :]
        o = jnp.einsum('bhid,bhde->bhie', q_c * jnp.exp(b_c), state) + scores @ v_c
        state = (state * jnp.exp(b_last)[..., 0, :, None]
                 + jnp.einsum('bhjd,bhje->bhde', k_c * jnp.exp(b_last - b_c), v_c))
        return state, o

    _, o = lax.scan(step, jnp.zeros((B, H, dF, dI), f32), (q, k, v, b))
    return jnp.moveaxis(o, 0, 2).reshape(B, H, S, dI)


def _flip(t):
    return jnp.flip(t, axis=2)


LANES = 128
GDN_BLOCK = 16
GDN_TILE = 256
GDN_GROUP = 4


def _gdn_kernel(*refs):
    C, G, H = GDN_BLOCK, GDN_GROUP, GDN_HEADS
    per_dir = 3 * G + 1
    in_refs, out_refs = refs[:2 * per_dir], refs[2 * per_dir:2 * per_dir + 2]
    (state_ref, u_ref, w_ref, qe_ref, kd_ref, a_ref, dec_ref,
     us_ref, ws_ref, gc_ref) = refs[2 * per_dir + 2:]
    T = in_refs[0].shape[1]
    nb = T // C
    bf16 = jnp.bfloat16
    scale = GDN_DK ** -0.5
    hg = pl.program_id(1)

    @pl.when(pl.program_id(2) == 0)
    def _():
        state_ref[...] = jnp.zeros_like(state_ref)

    slab = lambda r: pl.ds(r, nb, stride=C)
    ones = jnp.ones((LANES, LANES), bf16)
    lane = lax.broadcasted_iota(jnp.int32, (nb, LANES), 1)
    rowsum = lambda x: jnp.broadcast_to(jnp.sum(x, axis=-1, keepdims=True), (nb, LANES))

    def pick(x, idx):
        return rowsum(jnp.where(lane == idx, x, 0.0))

    orders = (list(range(C)), list(range(C - 1, -1, -1)))
    chains = [(d, g) for d in range(2) for g in range(G)]
    sc_refs = [in_refs[d * per_dir + 3 * G] for d in range(2)]
    qkv_refs = {(d, g): [in_refs[d * per_dir + j * G + g] for j in range(3)] for d, g in chains}

    for d, g in chains:
        run = None
        for r in orders[d]:
            g_r = pick(sc_refs[d][0, slab(r), :], d * H + hg * G + g)
            run = g_r if run is None else run + g_r
            gc_ref[d, g, r] = run
        dec_ref[d, g] = jnp.exp(run)

    for n in range(C):
        for d, g in chains:
            order = orders[d]
            r = order[n]
            q_ref, k_ref, v_ref = qkv_refs[d, g]
            gc, us, ws = gc_ref.at[d, g], us_ref.at[d, g], ws_ref.at[d, g]
            q_r = q_ref[0, slab(r), :] * scale
            k_r = k_ref[0, slab(r), :]
            beta_r = pick(sc_refs[d][0, slab(r), :], 2 * H + d * H + hg * G + g)
            gc_r = gc[r]
            eg_r = jnp.exp(gc_r)
            u_r = beta_r * v_ref[0, slab(r), :]
            w_r = beta_r * eg_r * k_r
            k_prev = [k_ref[0, slab(r2), :] for r2 in order[:n]]
            prods = [q_r * k_r] + [q_r * k_2 for k_2 in k_prev] + [k_r * k_2 for k_2 in k_prev]
            dots = jnp.dot(jnp.concatenate(prods, axis=0).astype(bf16), ones,
                           preferred_element_type=jnp.float32)
            a_row = jnp.where(lane == r, dots[:nb], 0.0)
            for m, r2 in enumerate(order[:n]):
                decay = jnp.exp(gc_r - gc[r2])
                l_rr = beta_r * decay * dots[(1 + n + m) * nb:(2 + n + m) * nb]
                u_r = u_r - l_rr * us[r2]
                w_r = w_r - l_rr * ws[r2]
                a_row = jnp.where(lane == r2, decay * dots[(1 + m) * nb:(2 + m) * nb], a_row)
            us[r] = u_r
            ws[r] = w_r
            u_ref[d, g, slab(r), :] = u_r
            w_ref[d, g, slab(r), :] = w_r
            a_ref[d, g, slab(r), :] = a_row
            qe_ref[d, g, slab(r), :] = q_r * eg_r
            kd_ref[d, g, slab(r), :] = k_r * jnp.exp(gc[order[-1]] - gc_r)

    for i in range(nb):
        for d in range(2):
            blk = i if d == 0 else nb - 1 - i
            rows = pl.ds(blk * C, C)
            for g in range(G):
                st = state_ref[d, g]
                wq = jnp.concatenate([w_ref[d, g, rows, :], qe_ref[d, g, rows, :]], axis=0).astype(bf16)
                r1 = jnp.dot(wq, st.astype(bf16), preferred_element_type=jnp.float32)
                v_new = (u_ref[d, g, rows, :] - r1[:C]).astype(bf16)
                a_blk = a_ref[d, g, rows, :][:, :C].astype(bf16)
                out_refs[d][0, rows, g * LANES:(g + 1) * LANES] = r1[C:] + jnp.dot(
                    a_blk, v_new, preferred_element_type=jnp.float32)
                upd = lax.dot_general(kd_ref[d, g, rows, :].astype(bf16), v_new, (((0,), (0,)), ((), ())),
                                      preferred_element_type=jnp.float32)
                state_ref[d, g] = st * dec_ref[d, g, pl.ds(blk, 1), :] + upd


def gdn_bidir(qkv, scal):
    B, S, _ = qkv.shape
    H, G = GDN_HEADS, GDN_GROUP
    T = min(GDN_TILE, S)
    nT, nG = S // T, H // G
    nb = T // GDN_BLOCK
    fwd = lambda off: (lambda b, hg, t: (b, t, off + hg * G))
    bwd = lambda off: (lambda b, hg, t: (b, nT - 1 - t, off + hg * G))
    blk = lambda im: pl.BlockSpec((1, T, LANES), im)
    in_specs, operands = [], []
    for mk in (fwd, bwd):
        for j in range(3):
            for g in range(G):
                in_specs.append(blk(mk(j * H + g)))
                operands.append(qkv)
        in_specs.append(pl.BlockSpec((1, T, LANES), (lambda b, hg, t: (b, t, 0)) if mk is fwd
                                     else (lambda b, hg, t: (b, nT - 1 - t, 0))))
        operands.append(scal)
    out_specs = [pl.BlockSpec((1, T, G * LANES), lambda b, hg, t: (b, t, hg)),
                 pl.BlockSpec((1, T, G * LANES), lambda b, hg, t: (b, nT - 1 - t, hg))]
    out_sd = jax.ShapeDtypeStruct((B, S, H * LANES), jnp.float32)
    tile = pltpu.VMEM((2, G, T, LANES), jnp.float32)
    slabs = pltpu.VMEM((2, G, GDN_BLOCK, nb, LANES), jnp.float32)
    return pl.pallas_call(
        _gdn_kernel,
        grid=(B, nG, nT),
        in_specs=in_specs,
        out_specs=out_specs,
        out_shape=[out_sd, out_sd],
        scratch_shapes=[pltpu.VMEM((2, G, LANES, LANES), jnp.float32), tile, tile, tile, tile, tile,
                        pltpu.VMEM((2, G, nb, LANES), jnp.float32), slabs, slabs, slabs],
        compiler_params=pltpu.CompilerParams(
            dimension_semantics=("parallel", "parallel", "arbitrary"),
            vmem_limit_bytes=32 * 1024 * 1024),
        name="gdn_bidir",
    )(*operands)


def gdn_mixer(h, w_in, conv_w, A_log, dt_bias, norm_g, w_out):
    B, S, _ = h.shape
    H, dk, dv = GDN_HEADS, GDN_DK, GDN_DV
    qkv, z, a, b = jnp.split(h @ w_in, [2 * H * dk + H * dv, 2 * H * dk + 2 * H * dv,
                                         2 * H * dk + 2 * H * dv + 2 * H], axis=-1)
    qkv = jax.nn.silu(centred_depthwise_conv(qkv, conv_w))
    q, k, v = jnp.split(qkv, [H * dk, 2 * H * dk], axis=-1)
    q = l2_normalize(q.reshape(B, S, H, dk)).reshape(B, S, H * dk)
    k = l2_normalize(k.reshape(B, S, H, dk)).reshape(B, S, H * dk)
    a = a.astype(jnp.float32).reshape(B, S, 2, H)
    b = b.astype(jnp.float32).reshape(B, S, 2, H)
    g = -jnp.exp(A_log.astype(jnp.float32)) * jax.nn.softplus(a + dt_bias.astype(jnp.float32))
    beta = jax.nn.sigmoid(b)
    scal = jnp.concatenate([g.reshape(B, S, 2 * H), beta.reshape(B, S, 2 * H),
                            jnp.zeros((B, S, LANES - 4 * H), jnp.float32)], axis=-1)
    o_fwd, o_bwd = gdn_bidir(jnp.concatenate([q, k, v], axis=-1), scal)
    o = (o_fwd + o_bwd).astype(h.dtype).reshape(B, S, H, dv)
    o = rms_norm(o, norm_g) * jax.nn.silu(z.reshape(B, S, H, dv))
    return o.reshape(B, S, H * dv) @ w_out


HGRN_BLOCK = 16
HGRN_TILE = 256
HGRN_GROUP = 4
LANES = 128


def _hgrn_kernel(*refs):
    C, G = HGRN_BLOCK, HGRN_GROUP
    in_refs, out_refs = refs[:6 * G], refs[6 * G:6 * G + 2]
    state_ref, qe_ref, ke_ref, od_ref, dec_ref, b_ref, kk_ref = refs[6 * G + 2:]
    T = in_refs[0].shape[1]
    nb = T // C
    bf16 = jnp.bfloat16
    chains = [[(in_refs[(3 * d + 0) * G + g], in_refs[(3 * d + 1) * G + g], in_refs[(3 * d + 2) * G + g])
               for g in range(G)] for d in range(2)]

    @pl.when(pl.program_id(2) == 0)
    def _():
        state_ref[...] = jnp.zeros_like(state_ref)

    slab = lambda r: pl.ds(r, nb, stride=C)
    ones = jnp.ones((LANES, LANES), bf16)
    for d in range(2):
        order = list(range(C)) if d == 0 else list(range(C - 1, -1, -1))
        for g in range(G):
            q_ref, v_ref, l_ref = chains[d][g]
            bs, kk = b_ref.at[d, g], kk_ref.at[d, g]
            run = None
            for r in order:
                lf = l_ref[0, slab(r), :]
                run = lf if run is None else run + lf
                bs[r] = run
                kk[r] = 1.0 - jnp.exp(lf)
            dec_ref[d, g] = jnp.exp(run)
    for r in range(C):
        for d in range(2):
            for g in range(G):
                q_ref, v_ref, l_ref = chains[d][g]
                bs, kk = b_ref.at[d, g], kk_ref.at[d, g]
                q_r = q_ref[0, slab(r), :]
                b_r = bs[r]
                btot = bs[C - 1 if d == 0 else 0]
                qe_ref[d, g, slab(r), :] = q_r * jnp.exp(b_r)
                ke_ref[d, g, slab(r), :] = kk[r] * jnp.exp(btot - b_r)
                partners = list(range(r + 1) if d == 0 else range(r, C))
                terms = [q_r * kk[r2] * jnp.exp(b_r - bs[r2]) for r2 in partners]
                scores = jnp.dot(jnp.concatenate(terms, axis=0).astype(bf16), ones,
                                 preferred_element_type=jnp.float32)
                acc = jnp.zeros_like(q_r)
                for m, r2 in enumerate(partners):
                    acc = acc + scores[m * nb:(m + 1) * nb] * v_ref[0, slab(r2), :]
                od_ref[d, g, slab(r), :] = acc

    def block_step(i, carry):
        for d in range(2):
            blk = i if d == 0 else nb - 1 - i
            rows = pl.ds(blk * C, C)
            for g in range(G):
                q_ref, v_ref, l_ref = chains[d][g]
                st = state_ref[d, g]
                qe = qe_ref[d, g, rows, :].astype(bf16)
                out_refs[d][0, rows, g * LANES:(g + 1) * LANES] = od_ref[d, g, rows, :] + lax.dot_general(
                    qe, st.astype(bf16), (((1,), (1,)), ((), ())), preferred_element_type=jnp.float32)
                upd = lax.dot_general(v_ref[0, rows, :].astype(bf16), ke_ref[d, g, rows, :].astype(bf16),
                                      (((0,), (0,)), ((), ())), preferred_element_type=jnp.float32)
                state_ref[d, g] = st * dec_ref[d, g, pl.ds(blk, 1), :] + upd
        return carry

    for i in range(nb):
        block_step(i, 0)


def hgrn2_bidir(proj, logf):
    B, S, _ = proj.shape
    H = HGRN_HEADS
    T = min(HGRN_TILE, S)
    G = HGRN_GROUP
    nT, nG = S // T, H // G
    nb = T // HGRN_BLOCK
    fwd = lambda off: (lambda b, hg, t: (b, t, off + hg * G))
    bwd = lambda off: (lambda b, hg, t: (b, nT - 1 - t, off + hg * G))
    blk = lambda im: pl.BlockSpec((1, T, LANES), im)
    in_specs, operands = [], []
    for mk, lf_off in ((fwd, 0), (bwd, H)):
        for arr, off in ((proj, 0), (proj, H), (logf, lf_off)):
            for g in range(G):
                in_specs.append(blk(mk(off + g)))
                operands.append(arr)
    out_specs = [pl.BlockSpec((1, T, G * LANES), lambda b, hg, t: (b, t, hg)),
                 pl.BlockSpec((1, T, G * LANES), lambda b, hg, t: (b, nT - 1 - t, hg))]
    out_sd = jax.ShapeDtypeStruct((B, S, H * LANES), jnp.float32)
    outs = pl.pallas_call(
        _hgrn_kernel,
        grid=(B, nG, nT),
        in_specs=in_specs,
        out_specs=out_specs,
        out_shape=[out_sd, out_sd],
        scratch_shapes=[pltpu.VMEM((2, G, LANES, LANES), jnp.float32),
                        pltpu.VMEM((2, G, T, LANES), jnp.float32), pltpu.VMEM((2, G, T, LANES), jnp.float32),
                        pltpu.VMEM((2, G, T, LANES), jnp.float32),
                        pltpu.VMEM((2, G, nb, LANES), jnp.float32),
                        pltpu.VMEM((2, G, HGRN_BLOCK, nb, LANES), jnp.float32),
                        pltpu.VMEM((2, G, HGRN_BLOCK, nb, LANES), jnp.float32)],
        compiler_params=pltpu.CompilerParams(
            dimension_semantics=("parallel", "parallel", "arbitrary"),
            vmem_limit_bytes=32 * 1024 * 1024),
        name="hgrn2_bidir",
    )(*operands)
    return outs


def hgrn2_mixer(h, layer, w_in, lb_table, norm_g, w_out):
    B, S, _ = h.shape
    H, dF, dI = HGRN_HEADS, HGRN_DF, HGRN_DI
    proj = h @ w_in
    gate = proj[..., HGRN_F + H * dI:HGRN_F + 2 * H * dI]
    f_logit = proj[..., HGRN_F + 2 * H * dI:]
    lb_w = jax.nn.softmax(lb_table.astype(jnp.float32), axis=1)
    lb = (jnp.cumsum(lb_w, axis=1) - lb_w[:, :1])[:, layer]
    f_logit = f_logit.astype(jnp.float32).reshape(B, S, 2, HGRN_F)
    logf = jnp.logaddexp(jnp.log(lb), jnp.log1p(-lb) + jax.nn.log_sigmoid(f_logit))
    o_fwd, o_bwd = hgrn2_bidir(proj, logf.reshape(B, S, 2 * HGRN_F))
    o = (o_fwd + o_bwd).astype(h.dtype).reshape(B, S, H, dI)
    o = rms_norm(o, norm_g) * jax.nn.silu(gate.reshape(B, S, H, dI))
    return o.reshape(B, S, H * dI) @ w_out


def shortconv_mixer(h, w_in, conv_w, w_out):
    gb, gc, u = jnp.split(h @ w_in, 3, axis=-1)
    return (gb * centred_depthwise_conv(gc * u, conv_w)) @ w_out


def rope(x, pos):
    half = x.shape[-1] // 2
    inv = ROPE_THETA ** (-jnp.arange(half, dtype=jnp.float32) / half)
    ang = pos.astype(jnp.float32)[:, :, None, None] * inv
    cos, sin = jnp.cos(ang), jnp.sin(ang)
    xf = x.astype(jnp.float32)
    x1, x2 = xf[..., :half], xf[..., half:]
    return jnp.concatenate([x1 * cos - x2 * sin, x1 * sin + x2 * cos], axis=-1).astype(x.dtype)


MLA_QK_PAD = 128
MLA_TQ = 512
MLA_TK = 1024


def _mla_attn_kernel(q_ref, k_ref, v_ref, o_ref, *, tk):
    tq = q_ref.shape[1]
    nk = k_ref.shape[1] // tk
    head_lanes = [slice(hh * MLA_QK_PAD, (hh + 1) * MLA_QK_PAD) for hh in range(2)]
    qs = [q_ref[0, :, lanes] for lanes in head_lanes]

    def body(j, carry):
        rows = pl.ds(pl.multiple_of(j * tk, tk), tk)
        v = v_ref[0, rows, :]
        new = []
        for hh in range(2):
            m, l, acc = carry[hh]
            k = k_ref[0, rows, head_lanes[hh]]
            s = lax.dot_general(qs[hh], k, (((1,), (1,)), ((), ())), preferred_element_type=jnp.float32)
            m_new = jnp.maximum(m, jnp.max(s, axis=-1, keepdims=True))
            p = jnp.exp(s - m_new)
            alpha = jnp.exp(m - m_new)
            l = alpha * l + jnp.sum(p, axis=-1, keepdims=True)
            acc = alpha * acc + jnp.dot(p.astype(jnp.bfloat16), v, preferred_element_type=jnp.float32)
            new.append((m_new, l, acc))
        return tuple(new)

    init = (jnp.full((tq, 1), -jnp.inf, jnp.float32), jnp.zeros((tq, 1), jnp.float32),
            jnp.zeros((tq, 2 * MLA_V), jnp.float32))
    (_, l0, acc0), (_, l1, acc1) = lax.fori_loop(0, nk, body, (init, init))
    lane = lax.broadcasted_iota(jnp.int32, (tq, 2 * MLA_V), 1)
    o_ref[0] = jnp.where(lane < MLA_V, acc0 / l0, acc1 / l1).astype(o_ref.dtype)


def mla_attention(q_nope, q_rope, k_nope, k_rope, v):
    B, S, H, _ = q_nope.shape
    scale = (MLA_NOPE + MLA_ROPE) ** -0.5
    pad = MLA_QK_PAD - MLA_NOPE - MLA_ROPE
    bf16 = jnp.bfloat16
    qf = jnp.concatenate([q_nope, q_rope, jnp.zeros((B, S, H, pad), q_nope.dtype)], axis=-1) * scale
    kf = jnp.concatenate([k_nope, jnp.broadcast_to(k_rope[:, :, None, :], (B, S, H, MLA_ROPE)),
                          jnp.zeros((B, S, H, pad), k_nope.dtype)], axis=-1)
    qf = qf.astype(bf16).reshape(B, S, H * MLA_QK_PAD)
    kf = kf.astype(bf16).reshape(B, S, H * MLA_QK_PAD)
    vf = v.astype(bf16).reshape(B, S, H * MLA_V)
    tq, tk = min(MLA_TQ, S), min(MLA_TK, S)
    out = pl.pallas_call(
        functools.partial(_mla_attn_kernel, tk=tk),
        grid=(B, H // 2, S // tq),
        in_specs=[pl.BlockSpec((1, tq, 2 * MLA_QK_PAD), lambda b, h, i: (b, i, h)),
                  pl.BlockSpec((1, S, 2 * MLA_QK_PAD), lambda b, h, i: (b, 0, h)),
                  pl.BlockSpec((1, S, 2 * MLA_V), lambda b, h, i: (b, 0, h))],
        out_specs=pl.BlockSpec((1, tq, 2 * MLA_V), lambda b, h, i: (b, i, h)),
        out_shape=jax.ShapeDtypeStruct((B, S, H * MLA_V), jnp.float32),
        compiler_params=pltpu.CompilerParams(
            dimension_semantics=("parallel", "parallel", "arbitrary"),
            vmem_limit_bytes=48 * 1024 * 1024),
        name="mla_attention",
    )(qf, kf, vf)
    return out.reshape(B, S, H, MLA_V)


def mla_mixer(h, pos, w_in, q_norm, w_uq, kv_norm, w_ukv, w_o):
    B, S, _ = h.shape
    H = MLA_HEADS
    cq, ckv, kr = jnp.split(h @ w_in, [MLA_Q_LORA, MLA_Q_LORA + MLA_KV_LORA], axis=-1)
    q = (rms_norm(cq, q_norm) @ w_uq).reshape(B, S, H, MLA_NOPE + MLA_ROPE)
    q_nope, q_rope = q[..., :MLA_NOPE], rope(q[..., MLA_NOPE:], pos)
    kv = (rms_norm(ckv, kv_norm) @ w_ukv).reshape(B, S, H, MLA_NOPE + MLA_V)
    k_nope, v = kv[..., :MLA_NOPE], kv[..., MLA_NOPE:]
    k_rope = rope(kr[:, :, None, :], pos)[:, :, 0, :]
    o = mla_attention(q_nope, q_rope, k_nope, k_rope, v)
    return o.reshape(B, S, H * MLA_V) @ w_o


def expert_choice_ffn(h, w_router, w_gate, w_up, w_down):
    B, S, _ = h.shape
    cap = CAPACITY_FACTOR * S // N_EXPERTS
    aff = jax.nn.softmax(jnp.einsum('bsd,de->bse', h, w_router).astype(jnp.float32), axis=-1)
    gate, idx = lax.top_k(jnp.swapaxes(aff, 1, 2), cap)
    bi = jnp.arange(B)[:, None, None]
    xs = h[bi, idx]
    ys = expert_ffn(xs, gate, w_gate, w_up, w_down)
    return jnp.zeros_like(h).at[bi, idx].add(ys)


FFN_BATCH_ROWS = 2
FFN_TF = 512


def _expert_ffn_kernel(x_ref, g_ref, wg_ref, wu_ref, wd_ref, o_ref):
    f = pl.program_id(2)
    bb, _, cap, d = x_ref.shape
    x = x_ref[...].reshape(bb * cap, d)
    bf16 = jnp.bfloat16
    a = jnp.dot(x, wg_ref[0].astype(bf16), preferred_element_type=jnp.float32)
    u = jnp.dot(x, wu_ref[0].astype(bf16), preferred_element_type=jnp.float32)
    hid = (a * jax.nn.sigmoid(a) * u).astype(bf16)
    y = jnp.dot(hid, wd_ref[0].astype(bf16), preferred_element_type=jnp.float32).reshape(bb, 1, cap, d)

    @pl.when(f == 0)
    def _():
        o_ref[...] = y

    @pl.when(f > 0)
    def _():
        o_ref[...] += y

    @pl.when(f == pl.num_programs(2) - 1)
    def _():
        o_ref[...] = o_ref[...] * g_ref[...]


def expert_ffn(xs, gate, w_gate, w_up, w_down):
    B, E, cap, D = xs.shape
    F = w_gate.shape[-1]
    bb = min(FFN_BATCH_ROWS, B)
    tf = min(FFN_TF, F)
    return pl.pallas_call(
        _expert_ffn_kernel,
        grid=(E, B // bb, F // tf),
        in_specs=[pl.BlockSpec((bb, 1, cap, D), lambda e, b, f: (b, e, 0, 0)),
                  pl.BlockSpec((bb, 1, cap, 1), lambda e, b, f: (b, e, 0, 0)),
                  pl.BlockSpec((1, D, tf), lambda e, b, f: (e, 0, f)),
                  pl.BlockSpec((1, D, tf), lambda e, b, f: (e, 0, f)),
                  pl.BlockSpec((1, tf, D), lambda e, b, f: (e, f, 0))],
        out_specs=pl.BlockSpec((bb, 1, cap, D), lambda e, b, f: (b, e, 0, 0)),
        out_shape=jax.ShapeDtypeStruct((B, E, cap, D), jnp.float32),
        compiler_params=pltpu.CompilerParams(
            dimension_semantics=("parallel", "parallel", "arbitrary"),
            vmem_limit_bytes=56 * 1024 * 1024),
        name="expert_ffn",
    )(xs.astype(jnp.bfloat16), gate[..., None].astype(jnp.float32), w_gate, w_up, w_down)


def _final_norm_kernel(x_ref, g_ref, o_ref):
    x = x_ref[...]
    y = x * lax.rsqrt(jnp.mean(x * x, axis=-1, keepdims=True) + EPS)
    o_ref[...] = y * g_ref[...]


def final_norm(x, g):
    B, S, D = x.shape
    x2 = x.reshape(B * S, D)
    tm = 1024
    out = pl.pallas_call(
        _final_norm_kernel,
        grid=(B * S // tm,),
        in_specs=[pl.BlockSpec((tm, D), lambda i: (i, 0)),
                  pl.BlockSpec((1, D), lambda i: (0, 0))],
        out_specs=pl.BlockSpec((tm, D), lambda i: (i, 0)),
        out_shape=jax.ShapeDtypeStruct((B * S, D), x.dtype),
        name="final_norm",
    )(x2, g.reshape(1, D))
    return out.reshape(B, S, D)


def kernel(x, positions, norm_mix, norm_ffn, norm_final,
           gdn_w_in, gdn_conv, gdn_A_log, gdn_dt_bias, gdn_norm, gdn_w_out,
           hgrn_w_in, hgrn_lb, hgrn_norm, hgrn_w_out,
           sc_w_in, sc_conv, sc_w_out,
           mla_w_in, mla_q_norm, mla_w_uq, mla_kv_norm, mla_w_ukv, mla_w_o,
           moe_router, moe_w_gate, moe_w_up, moe_w_down):
    for i in range(DEPTH):
        m, j = i % N_MIXERS, i // N_MIXERS
        hn = rms_norm(x, norm_mix[i])
        if m == 0:
            y = gdn_mixer(hn, gdn_w_in[j], gdn_conv[j], gdn_A_log[j], gdn_dt_bias[j], gdn_norm[j], gdn_w_out[j])
        elif m == 1:
            y = hgrn2_mixer(hn, i, hgrn_w_in[j], hgrn_lb, hgrn_norm[j], hgrn_w_out[j])
        elif m == 2:
            y = shortconv_mixer(hn, sc_w_in[j], sc_conv[j], sc_w_out[j])
        else:
            y = mla_mixer(hn, positions, mla_w_in[j], mla_q_norm[j], mla_w_uq[j],
                          mla_kv_norm[j], mla_w_ukv[j], mla_w_o[j])
        x = x + y
        x = x + expert_choice_ffn(rms_norm(x, norm_ffn[i]), moe_router[i], moe_w_gate[i],
                                  moe_w_up[i], moe_w_down[i])
    return final_norm(x, norm_final)
```

```python
import functools
import math

import jax
import jax.numpy as jnp
from jax import lax
from jax.experimental import pallas as pl
from jax.experimental.pallas import tpu as pltpu

D_MODEL = 1024
BATCH = 4
SEQ = 8192
DEPTH = 4
N_MIXERS = 4
EPS = 1e-6
GDN_HEADS = 8
GDN_DK = 128
GDN_DV = 128
GDN_CONV = 5
HGRN_EXPAND = 128
HGRN_HEADS = D_MODEL // HGRN_EXPAND
HGRN_DF = HGRN_EXPAND
HGRN_DI = D_MODEL // HGRN_HEADS
HGRN_F = HGRN_HEADS * HGRN_DF
SC_WIDTH = 3
MLA_HEADS = 16
MLA_NOPE = 64
MLA_ROPE = 32
MLA_V = 64
MLA_Q_LORA = 384
MLA_KV_LORA = 256
ROPE_THETA = 10000.0
N_EXPERTS = 16
D_EXPERT = 2048
CAPACITY_FACTOR = 2


def rms_norm(x, g):
    xf = x.astype(jnp.float32)
    y = xf * lax.rsqrt(jnp.mean(xf * xf, axis=-1, keepdims=True) + EPS)
    return (y * g.astype(jnp.float32)).astype(x.dtype)


def l2_normalize(x):
    xf = x.astype(jnp.float32)
    return xf * lax.rsqrt(jnp.sum(xf * xf, axis=-1, keepdims=True) + EPS)


def centred_depthwise_conv(x, w):
    K, C = w.shape
    return lax.conv_general_dilated(
        x, w[:, None, :], window_strides=(1,), padding=[(K // 2, K // 2)],
        dimension_numbers=('NWC', 'WIO', 'NWC'), feature_group_count=C)


LANES = 128
GDN_BLOCK = 16
GDN_TILE = 256
GDN_GROUP = 4


def _gdn_kernel(*refs):
    C, G, H = GDN_BLOCK, GDN_GROUP, GDN_HEADS
    per_dir = 3 * G + 1
    in_refs, out_refs = refs[:2 * per_dir], refs[2 * per_dir:2 * per_dir + 2]
    (state_ref, u_ref, w_ref, qe_ref, kd_ref, a_ref, dec_ref,
     us_ref, ws_ref, gc_ref) = refs[2 * per_dir + 2:]
    T = in_refs[0].shape[1]
    nb = T // C
    bf16 = jnp.bfloat16
    scale = GDN_DK ** -0.5
    hg = pl.program_id(1)

    @pl.when(pl.program_id(2) == 0)
    def _():
        state_ref[...] = jnp.zeros_like(state_ref)

    slab = lambda r: pl.ds(r, nb, stride=C)
    ones = jnp.ones((LANES, LANES), bf16)
    lane = lax.broadcasted_iota(jnp.int32, (nb, LANES), 1)
    rowsum = lambda x: jnp.broadcast_to(jnp.sum(x, axis=-1, keepdims=True), (nb, LANES))

    def pick(x, idx):
        return rowsum(jnp.where(lane == idx, x, 0.0))

    orders = (list(range(C)), list(range(C - 1, -1, -1)))
    chains = [(d, g) for d in range(2) for g in range(G)]
    sc_refs = [in_refs[d * per_dir + 3 * G] for d in range(2)]
    qkv_refs = {(d, g): [in_refs[d * per_dir + j * G + g] for j in range(3)] for d, g in chains}

    for d, g in chains:
        run = None
        for r in orders[d]:
            g_r = pick(sc_refs[d][0, slab(r), :], d * H + hg * G + g)
            run = g_r if run is None else run + g_r
            gc_ref[d, g, r] = run
        dec_ref[d, g] = jnp.exp(run)

    for n in range(C):
        for d, g in chains:
            order = orders[d]
            r = order[n]
            q_ref, k_ref, v_ref = qkv_refs[d, g]
            gc, us, ws = gc_ref.at[d, g], us_ref.at[d, g], ws_ref.at[d, g]
            q_r = q_ref[0, slab(r), :] * scale
            k_r = k_ref[0, slab(r), :]
            beta_r = pick(sc_refs[d][0, slab(r), :], 2 * H + d * H + hg * G + g)
            gc_r = gc[r]
            eg_r = jnp.exp(gc_r)
            u_r = beta_r * v_ref[0, slab(r), :]
            w_r = beta_r * eg_r * k_r
            k_prev = [k_ref[0, slab(r2), :] for r2 in order[:n]]
            prods = [q_r * k_r] + [q_r * k_2 for k_2 in k_prev] + [k_r * k_2 for k_2 in k_prev]
            dots = jnp.dot(jnp.concatenate(prods, axis=0).astype(bf16), ones,
                           preferred_element_type=jnp.float32)
            a_row = jnp.where(lane == r, dots[:nb], 0.0)
            for m, r2 in enumerate(order[:n]):
                decay = jnp.exp(gc_r - gc[r2])
                l_rr = beta_r * decay * dots[(1 + n + m) * nb:(2 + n + m) * nb]
                u_r = u_r - l_rr * us[r2]
                w_r = w_r - l_rr * ws[r2]
                a_row = jnp.where(lane == r2, decay * dots[(1 + m) * nb:(2 + m) * nb], a_row)
            us[r] = u_r
            ws[r] = w_r
            u_ref[d, g, slab(r), :] = u_r
            w_ref[d, g, slab(r), :] = w_r
            a_ref[d, g, slab(r), :] = a_row
            qe_ref[d, g, slab(r), :] = q_r * eg_r
            kd_ref[d, g, slab(r), :] = k_r * jnp.exp(gc[order[-1]] - gc_r)

    for i in range(nb):
        for d in range(2):
            blk = i if d == 0 else nb - 1 - i
            rows = pl.ds(blk * C, C)
            for g in range(G):
                st = state_ref[d, g]
                wq = jnp.concatenate([w_ref[d, g, rows, :], qe_ref[d, g, rows, :]], axis=0).astype(bf16)
                r1 = jnp.dot(wq, st.astype(bf16), preferred_element_type=jnp.float32)
                v_new = (u_ref[d, g, rows, :] - r1[:C]).astype(bf16)
                a_blk = a_ref[d, g, rows, :][:, :C].astype(bf16)
                out_refs[d][0, rows, g * LANES:(g + 1) * LANES] = r1[C:] + jnp.dot(
                    a_blk, v_new, preferred_element_type=jnp.float32)
                upd = lax.dot_general(kd_ref[d, g, rows, :].astype(bf16), v_new, (((0,), (0,)), ((), ())),
                                      preferred_element_type=jnp.float32)
                state_ref[d, g] = st * dec_ref[d, g, pl.ds(blk, 1), :] + upd


def gdn_bidir(qkv, scal):
    B, S, _ = qkv.shape
    H, G = GDN_HEADS, GDN_GROUP
    T = min(GDN_TILE, S)
    nT, nG = S // T, H // G
    nb = T // GDN_BLOCK
    fwd = lambda off: (lambda b, hg, t: (b, t, off + hg * G))
    bwd = lambda off: (lambda b, hg, t: (b, nT - 1 - t, off + hg * G))
    blk = lambda im: pl.BlockSpec((1, T, LANES), im)
    in_specs, operands = [], []
    for mk in (fwd, bwd):
        for j in range(3):
            for g in range(G):
                in_specs.append(blk(mk(j * H + g)))
                operands.append(qkv)
        in_specs.append(pl.BlockSpec((1, T, LANES), (lambda b, hg, t: (b, t, 0)) if mk is fwd
                                     else (lambda b, hg, t: (b, nT - 1 - t, 0))))
        operands.append(scal)
    out_specs = [pl.BlockSpec((1, T, G * LANES), lambda b, hg, t: (b, t, hg)),
                 pl.BlockSpec((1, T, G * LANES), lambda b, hg, t: (b, nT - 1 - t, hg))]
    out_sd = jax.ShapeDtypeStruct((B, S, H * LANES), jnp.float32)
    tile = pltpu.VMEM((2, G, T, LANES), jnp.float32)
    slabs = pltpu.VMEM((2, G, GDN_BLOCK, nb, LANES), jnp.float32)
    return pl.pallas_call(
        _gdn_kernel,
        grid=(B, nG, nT),
        in_specs=in_specs,
        out_specs=out_specs,
        out_shape=[out_sd, out_sd],
        scratch_shapes=[pltpu.VMEM((2, G, LANES, LANES), jnp.float32), tile, tile, tile, tile, tile,
                        pltpu.VMEM((2, G, nb, LANES), jnp.float32), slabs, slabs, slabs],
        compiler_params=pltpu.CompilerParams(
            dimension_semantics=("parallel", "parallel", "arbitrary"),
            vmem_limit_bytes=32 * 1024 * 1024),
        name="gdn_bidir",
    )(*operands)


def gdn_mixer(x, gain, w_in, conv_w, A_log, dt_bias, norm_g, w_out):
    B, S, _ = x.shape
    H, dk, dv = GDN_HEADS, GDN_DK, GDN_DV
    n_main = 2 * H * dk + 2 * H * dv
    main = fused_matmul(x, w_in[:, :n_main], gain=gain)
    ab = fused_matmul(x, w_in[:, n_main:], gain=gain)
    qkv, z = main[..., :2 * H * dk + H * dv], main[..., 2 * H * dk + H * dv:]
    a, b = ab[..., :2 * H], ab[..., 2 * H:]
    qkv = jax.nn.silu(centred_depthwise_conv(qkv, conv_w))
    q, k, v = jnp.split(qkv, [H * dk, 2 * H * dk], axis=-1)
    q = l2_normalize(q.reshape(B, S, H, dk)).reshape(B, S, H * dk)
    k = l2_normalize(k.reshape(B, S, H, dk)).reshape(B, S, H * dk)
    a = a.astype(jnp.float32).reshape(B, S, 2, H)
    b = b.astype(jnp.float32).reshape(B, S, 2, H)
    g = -jnp.exp(A_log.astype(jnp.float32)) * jax.nn.softplus(a + dt_bias.astype(jnp.float32))
    beta = jax.nn.sigmoid(b)
    scal = jnp.concatenate([g.reshape(B, S, 2 * H), beta.reshape(B, S, 2 * H),
                            jnp.zeros((B, S, LANES - 4 * H), jnp.float32)], axis=-1)
    o_fwd, o_bwd = gdn_bidir(jnp.concatenate([q, k, v], axis=-1), scal)
    o = (o_fwd + o_bwd).reshape(B, S, H, dv)
    o = rms_norm(o, norm_g) * jax.nn.silu(z.reshape(B, S, H, dv))
    return fused_matmul(o.reshape(B, S, H * dv), w_out, residual=x)


HGRN_BLOCK = 16
HGRN_TILE = 256
HGRN_GROUP = 4


def _hgrn_kernel(*refs):
    C, G = HGRN_BLOCK, HGRN_GROUP
    in_refs, out_refs = refs[:6 * G], refs[6 * G:6 * G + 2]
    state_ref, qe_ref, ke_ref, od_ref, dec_ref, b_ref, kk_ref = refs[6 * G + 2:]
    T = in_refs[0].shape[1]
    nb = T // C
    bf16 = jnp.bfloat16
    chains = [[(in_refs[(3 * d + 0) * G + g], in_refs[(3 * d + 1) * G + g], in_refs[(3 * d + 2) * G + g])
               for g in range(G)] for d in range(2)]

    @pl.when(pl.program_id(2) == 0)
    def _():
        state_ref[...] = jnp.zeros_like(state_ref)

    slab = lambda r: pl.ds(r, nb, stride=C)
    ones = jnp.ones((LANES, LANES), bf16)
    for d in range(2):
        order = list(range(C)) if d == 0 else list(range(C - 1, -1, -1))
        for g in range(G):
            q_ref, v_ref, l_ref = chains[d][g]
            bs, kk = b_ref.at[d, g], kk_ref.at[d, g]
            run = None
            for r in order:
                lf = l_ref[0, slab(r), :]
                run = lf if run is None else run + lf
                bs[r] = run
                kk[r] = 1.0 - jnp.exp(lf)
            dec_ref[d, g] = jnp.exp(run)
    for r in range(C):
        for d in range(2):
            for g in range(G):
                q_ref, v_ref, l_ref = chains[d][g]
                bs, kk = b_ref.at[d, g], kk_ref.at[d, g]
                q_r = q_ref[0, slab(r), :]
                b_r = bs[r]
                btot = bs[C - 1 if d == 0 else 0]
                qe_ref[d, g, slab(r), :] = q_r * jnp.exp(b_r)
                ke_ref[d, g, slab(r), :] = kk[r] * jnp.exp(btot - b_r)
                partners = list(range(r + 1) if d == 0 else range(r, C))
                terms = [q_r * kk[r2] * jnp.exp(b_r - bs[r2]) for r2 in partners]
                scores = jnp.dot(jnp.concatenate(terms, axis=0).astype(bf16), ones,
                                 preferred_element_type=jnp.float32)
                acc = jnp.zeros_like(q_r)
                for m, r2 in enumerate(partners):
                    acc = acc + scores[m * nb:(m + 1) * nb] * v_ref[0, slab(r2), :]
                od_ref[d, g, slab(r), :] = acc

    def block_step(i, carry):
        for d in range(2):
            blk = i if d == 0 else nb - 1 - i
            rows = pl.ds(blk * C, C)
            for g in range(G):
                v_ref = chains[d][g][1]
                st = state_ref[d, g]
                qe = qe_ref[d, g, rows, :].astype(bf16)
                out_refs[d][0, rows, g * LANES:(g + 1) * LANES] = od_ref[d, g, rows, :] + lax.dot_general(
                    qe, st.astype(bf16), (((1,), (1,)), ((), ())), preferred_element_type=jnp.float32)
                upd = lax.dot_general(v_ref[0, rows, :].astype(bf16), ke_ref[d, g, rows, :].astype(bf16),
                                      (((0,), (0,)), ((), ())), preferred_element_type=jnp.float32)
                state_ref[d, g] = st * dec_ref[d, g, pl.ds(blk, 1), :] + upd
        return carry

    for i in range(nb):
        block_step(i, 0)


def hgrn2_bidir(proj, logf):
    B, S, _ = proj.shape
    H = HGRN_HEADS
    T = min(HGRN_TILE, S)
    G = HGRN_GROUP
    nT, nG = S // T, H // G
    nb = T // HGRN_BLOCK
    fwd = lambda off: (lambda b, hg, t: (b, t, off + hg * G))
    bwd = lambda off: (lambda b, hg, t: (b, nT - 1 - t, off + hg * G))
    blk = lambda im: pl.BlockSpec((1, T, LANES), im)
    in_specs, operands = [], []
    for mk, lf_off in ((fwd, 0), (bwd, H)):
        for arr, off in ((proj, 0), (proj, H), (logf, lf_off)):
            for g in range(G):
                in_specs.append(blk(mk(off + g)))
                operands.append(arr)
    out_specs = [pl.BlockSpec((1, T, G * LANES), lambda b, hg, t: (b, t, hg)),
                 pl.BlockSpec((1, T, G * LANES), lambda b, hg, t: (b, nT - 1 - t, hg))]
    out_sd = jax.ShapeDtypeStruct((B, S, H * LANES), jnp.float32)
    outs = pl.pallas_call(
        _hgrn_kernel,
        grid=(B, nG, nT),
        in_specs=in_specs,
        out_specs=out_specs,
        out_shape=[out_sd, out_sd],
        scratch_shapes=[pltpu.VMEM((2, G, LANES, LANES), jnp.float32),
                        pltpu.VMEM((2, G, T, LANES), jnp.float32), pltpu.VMEM((2, G, T, LANES), jnp.float32),
                        pltpu.VMEM((2, G, T, LANES), jnp.float32),
                        pltpu.VMEM((2, G, nb, LANES), jnp.float32),
                        pltpu.VMEM((2, G, HGRN_BLOCK, nb, LANES), jnp.float32),
                        pltpu.VMEM((2, G, HGRN_BLOCK, nb, LANES), jnp.float32)],
        compiler_params=pltpu.CompilerParams(
            dimension_semantics=("parallel", "parallel", "arbitrary"),
            vmem_limit_bytes=32 * 1024 * 1024),
        name="hgrn2_bidir",
    )(*operands)
    return outs


def hgrn2_mixer(x, gain, layer, w_in, lb_table, norm_g, w_out):
    B, S, _ = x.shape
    H, dF, dI = HGRN_HEADS, HGRN_DF, HGRN_DI
    proj = fused_matmul(x, w_in, gain=gain)
    gate = proj[..., HGRN_F + H * dI:HGRN_F + 2 * H * dI]
    f_logit = proj[..., HGRN_F + 2 * H * dI:]
    lb_w = jax.nn.softmax(lb_table.astype(jnp.float32), axis=1)
    lb = (jnp.cumsum(lb_w, axis=1) - lb_w[:, :1])[:, layer]
    f_logit = f_logit.astype(jnp.float32).reshape(B, S, 2, HGRN_F)
    logf = jnp.logaddexp(jnp.log(lb), jnp.log1p(-lb) + jax.nn.log_sigmoid(f_logit))
    o_fwd, o_bwd = hgrn2_bidir(proj, logf.reshape(B, S, 2 * HGRN_F))
    o = (o_fwd + o_bwd).reshape(B, S, H, dI)
    o = rms_norm(o, norm_g) * jax.nn.silu(gate.reshape(B, S, H, dI))
    return fused_matmul(o.reshape(B, S, H * dI), w_out, residual=x)


def shortconv_mixer(x, gain, w_in, conv_w, w_out):
    gb, gc, u = jnp.split(fused_matmul(x, w_in, gain=gain), 3, axis=-1)
    return fused_matmul(gb * centred_depthwise_conv(gc * u, conv_w), w_out, residual=x)


def rope(x, pos):
    half = x.shape[-1] // 2
    inv = ROPE_THETA ** (-jnp.arange(half, dtype=jnp.float32) / half)
    ang = pos.astype(jnp.float32)[:, :, None, None] * inv
    cos, sin = jnp.cos(ang), jnp.sin(ang)
    xf = x.astype(jnp.float32)
    x1, x2 = xf[..., :half], xf[..., half:]
    return jnp.concatenate([x1 * cos - x2 * sin, x1 * sin + x2 * cos], axis=-1).astype(x.dtype)


MLA_QK_PAD = 128
MLA_TQ = 512
MLA_TK = 2048


def _mla_attn_kernel(q_ref, k_ref, v_ref, o_ref, *, tk):
    tq = q_ref.shape[1]
    nk = k_ref.shape[1] // tk
    head_lanes = [slice(hh * MLA_QK_PAD, (hh + 1) * MLA_QK_PAD) for hh in range(2)]
    qs = [q_ref[0, :, lanes] for lanes in head_lanes]

    def body(j, carry):
        rows = pl.ds(pl.multiple_of(j * tk, tk), tk)
        v = v_ref[0, rows, :]
        new = []
        for hh in range(2):
            m, l, acc = carry[hh]
            k = k_ref[0, rows, head_lanes[hh]]
            s = lax.dot_general(qs[hh], k, (((1,), (1,)), ((), ())), preferred_element_type=jnp.float32)
            m_new = jnp.maximum(m, jnp.max(s, axis=-1, keepdims=True))
            p = jnp.exp(s - m_new)
            alpha = jnp.exp(m - m_new)
            l = alpha * l + jnp.sum(p, axis=-1, keepdims=True)
            acc = alpha * acc + jnp.dot(p.astype(jnp.bfloat16), v, preferred_element_type=jnp.float32)
            new.append((m_new, l, acc))
        return tuple(new)

    init = (jnp.full((tq, 1), -jnp.inf, jnp.float32), jnp.zeros((tq, 1), jnp.float32),
            jnp.zeros((tq, 2 * MLA_V), jnp.float32))
    (_, l0, acc0), (_, l1, acc1) = lax.fori_loop(0, nk, body, (init, init))
    lane = lax.broadcasted_iota(jnp.int32, (tq, 2 * MLA_V), 1)
    o_ref[0] = jnp.where(lane < MLA_V, acc0 / l0, acc1 / l1).astype(o_ref.dtype)


def mla_attention(q_nope, q_rope, k_nope, k_rope, v):
    B, S, H, _ = q_nope.shape
    scale = (MLA_NOPE + MLA_ROPE) ** -0.5
    pad = MLA_QK_PAD - MLA_NOPE - MLA_ROPE
    bf16 = jnp.bfloat16
    qf = jnp.concatenate([q_nope, q_rope, jnp.zeros((B, S, H, pad), q_nope.dtype)], axis=-1) * scale
    kf = jnp.concatenate([k_nope, jnp.broadcast_to(k_rope[:, :, None, :], (B, S, H, MLA_ROPE)),
                          jnp.zeros((B, S, H, pad), k_nope.dtype)], axis=-1)
    qf = qf.astype(bf16).reshape(B, S, H * MLA_QK_PAD)
    kf = kf.astype(bf16).reshape(B, S, H * MLA_QK_PAD)
    vf = v.astype(bf16).reshape(B, S, H * MLA_V)
    tq, tk = min(MLA_TQ, S), min(MLA_TK, S)
    out = pl.pallas_call(
        functools.partial(_mla_attn_kernel, tk=tk),
        grid=(B, H // 2, S // tq),
        in_specs=[pl.BlockSpec((1, tq, 2 * MLA_QK_PAD), lambda b, h, i: (b, i, h)),
                  pl.BlockSpec((1, S, 2 * MLA_QK_PAD), lambda b, h, i: (b, 0, h)),
                  pl.BlockSpec((1, S, 2 * MLA_V), lambda b, h, i: (b, 0, h))],
        out_specs=pl.BlockSpec((1, tq, 2 * MLA_V), lambda b, h, i: (b, i, h)),
        out_shape=jax.ShapeDtypeStruct((B, S, H * MLA_V), jnp.float32),
        compiler_params=pltpu.CompilerParams(
            dimension_semantics=("parallel", "parallel", "arbitrary"),
            vmem_limit_bytes=48 * 1024 * 1024),
        name="mla_attention",
    )(qf, kf, vf)
    return out.reshape(B, S, H, MLA_V)


def mla_mixer(x, gain, pos, w_in, q_norm, w_uq, kv_norm, w_ukv, w_o):
    B, S, _ = x.shape
    H = MLA_HEADS
    cq, ckv, kr = jnp.split(fused_matmul(x, w_in, gain=gain), [MLA_Q_LORA, MLA_Q_LORA + MLA_KV_LORA], axis=-1)
    q = fused_matmul(cq, w_uq, gain=q_norm).reshape(B, S, H, MLA_NOPE + MLA_ROPE)
    q_nope, q_rope = q[..., :MLA_NOPE], rope(q[..., MLA_NOPE:], pos)
    kv = fused_matmul(ckv, w_ukv, gain=kv_norm).reshape(B, S, H, MLA_NOPE + MLA_V)
    k_nope, v = kv[..., :MLA_NOPE], kv[..., MLA_NOPE:]
    k_rope = rope(kr[:, :, None, :], pos)[:, :, 0, :]
    o = mla_attention(q_nope, q_rope, k_nope, k_rope, v)
    return fused_matmul(o.reshape(B, S, H * MLA_V), w_o, residual=x)


ROUTE_TM = 1024


def _route_kernel(x_ref, g_ref, wr_ref, h_ref, aff_ref):
    x = x_ref[...]
    h = (x * lax.rsqrt(jnp.mean(x * x, axis=-1, keepdims=True) + EPS) * g_ref[...]).astype(jnp.bfloat16)
    h_ref[...] = h
    logits = jnp.dot(h, wr_ref[...].astype(jnp.bfloat16), preferred_element_type=jnp.float32)
    e = jnp.exp(logits - jnp.max(logits, axis=-1, keepdims=True))
    aff_ref[...] = e / jnp.sum(e, axis=-1, keepdims=True)


def route(x, gain, w_router):
    B, S, D = x.shape
    E = w_router.shape[-1]
    M = B * S
    tm = min(ROUTE_TM, M)
    h, aff = pl.pallas_call(
        _route_kernel,
        grid=(M // tm,),
        in_specs=[pl.BlockSpec((tm, D), lambda i: (i, 0)),
                  pl.BlockSpec((1, D), lambda i: (0, 0)),
                  pl.BlockSpec((D, E), lambda i: (0, 0))],
        out_specs=[pl.BlockSpec((tm, D), lambda i: (i, 0)),
                   pl.BlockSpec((tm, E), lambda i: (i, 0))],
        out_shape=[jax.ShapeDtypeStruct((M, D), jnp.bfloat16), jax.ShapeDtypeStruct((M, E), jnp.float32)],
        compiler_params=pltpu.CompilerParams(dimension_semantics=("parallel",)),
        name="route",
    )(x.reshape(M, D), gain.reshape(1, D).astype(jnp.float32), w_router)
    return h.reshape(B, S, D), aff.reshape(B, S, E)


def expert_choice_ffn(x, gain, w_router, w_gate, w_up, w_down):
    B, S, _ = x.shape
    cap = CAPACITY_FACTOR * S // N_EXPERTS
    h, aff = route(x, gain, w_router)
    gate, idx = lax.top_k(jnp.swapaxes(aff, 1, 2), cap)
    bi = jnp.arange(B)[:, None, None]
    ys = expert_ffn(h[bi, idx], gate, w_gate, w_up, w_down)
    return x.at[bi, idx].add(ys)


FFN_BATCH_ROWS = 2
FFN_TF = 512


def _expert_ffn_kernel(x_ref, g_ref, wg_ref, wu_ref, wd_ref, o_ref):
    f = pl.program_id(2)
    bb, _, cap, d = x_ref.shape
    x = x_ref[...].reshape(bb * cap, d)
    bf16 = jnp.bfloat16
    a = jnp.dot(x, wg_ref[0].astype(bf16), preferred_element_type=jnp.float32)
    u = jnp.dot(x, wu_ref[0].astype(bf16), preferred_element_type=jnp.float32)
    hid = (a * jax.nn.sigmoid(a) * u).astype(bf16)
    y = jnp.dot(hid, wd_ref[0].astype(bf16), preferred_element_type=jnp.float32).reshape(bb, 1, cap, d)

    @pl.when(f == 0)
    def _():
        o_ref[...] = y

    @pl.when(f > 0)
    def _():
        o_ref[...] += y

    @pl.when(f == pl.num_programs(2) - 1)
    def _():
        o_ref[...] = o_ref[...] * g_ref[...]


def expert_ffn(xs, gate, w_gate, w_up, w_down):
    B, E, cap, D = xs.shape
    F = w_gate.shape[-1]
    bb = min(FFN_BATCH_ROWS, B)
    tf = min(FFN_TF, F)
    return pl.pallas_call(
        _expert_ffn_kernel,
        grid=(E, B // bb, F // tf),
        in_specs=[pl.BlockSpec((bb, 1, cap, D), lambda e, b, f: (b, e, 0, 0)),
                  pl.BlockSpec((bb, 1, cap, 1), lambda e, b, f: (b, e, 0, 0)),
                  pl.BlockSpec((1, D, tf), lambda e, b, f: (e, 0, f)),
                  pl.BlockSpec((1, D, tf), lambda e, b, f: (e, 0, f)),
                  pl.BlockSpec((1, tf, D), lambda e, b, f: (e, f, 0))],
        out_specs=pl.BlockSpec((bb, 1, cap, D), lambda e, b, f: (b, e, 0, 0)),
        out_shape=jax.ShapeDtypeStruct((B, E, cap, D), jnp.float32),
        compiler_params=pltpu.CompilerParams(
            dimension_semantics=("parallel", "parallel", "arbitrary"),
            vmem_limit_bytes=56 * 1024 * 1024),
        name="expert_ffn",
    )(xs.astype(jnp.bfloat16), gate[..., None].astype(jnp.float32), w_gate, w_up, w_down)


PROJ_TM = 1024
PROJ_TN_CHOICES = (1024, 768, 512, 384, 256, 128)


def _proj_kernel(*refs, normed, with_residual):
    refs = list(refs)
    x = refs.pop(0)[...]
    if normed:
        g = refs.pop(0)[...]
        x = x * lax.rsqrt(jnp.mean(x * x, axis=-1, keepdims=True) + EPS) * g
    w = refs.pop(0)[...]
    y = jnp.dot(x.astype(jnp.bfloat16), w.astype(jnp.bfloat16), preferred_element_type=jnp.float32)
    if with_residual:
        y = y + refs.pop(0)[...]
    refs.pop(0)[...] = y


def fused_matmul(x, w, gain=None, residual=None):
    lead, K = x.shape[:-1], x.shape[-1]
    N = w.shape[-1]
    M = math.prod(lead)
    tm = min(PROJ_TM, M)
    tn = next((t for t in PROJ_TN_CHOICES if N % t == 0), N)
    operands = [x.reshape(M, K)]
    in_specs = [pl.BlockSpec((tm, K), lambda n, m: (m, 0))]
    if gain is not None:
        operands.append(gain.reshape(1, K).astype(jnp.float32))
        in_specs.append(pl.BlockSpec((1, K), lambda n, m: (0, 0)))
    operands.append(w)
    in_specs.append(pl.BlockSpec((K, tn), lambda n, m: (0, n)))
    if residual is not None:
        operands.append(residual.reshape(M, N))
        in_specs.append(pl.BlockSpec((tm, tn), lambda n, m: (m, n)))
    out = pl.pallas_call(
        functools.partial(_proj_kernel, normed=gain is not None, with_residual=residual is not None),
        grid=(N // tn, M // tm),
        in_specs=in_specs,
        out_specs=pl.BlockSpec((tm, tn), lambda n, m: (m, n)),
        out_shape=jax.ShapeDtypeStruct((M, N), jnp.float32),
        compiler_params=pltpu.CompilerParams(
            dimension_semantics=("parallel", "parallel"),
            vmem_limit_bytes=48 * 1024 * 1024),
        name="fused_matmul",
    )(*operands)
    return out.reshape(*lead, N)


def _final_norm_kernel(x_ref, g_ref, o_ref):
    x = x_ref[...]
    y = x * lax.rsqrt(jnp.mean(x * x, axis=-1, keepdims=True) + EPS)
    o_ref[...] = y * g_ref[...]


def final_norm(x, g):
    B, S, D = x.shape
    x2 = x.reshape(B * S, D)
    tm = 1024
    out = pl.pallas_call(
        _final_norm_kernel,
        grid=(B * S // tm,),
        in_specs=[pl.BlockSpec((tm, D), lambda i: (i, 0)),
                  pl.BlockSpec((1, D), lambda i: (0, 0))],
        out_specs=pl.BlockSpec((tm, D), lambda i: (i, 0)),
        out_shape=jax.ShapeDtypeStruct((B * S, D), x.dtype),
        name="final_norm",
    )(x2, g.reshape(1, D))
    return out.reshape(B, S, D)


def kernel(x, positions, norm_mix, norm_ffn, norm_final,
           gdn_w_in, gdn_conv, gdn_A_log, gdn_dt_bias, gdn_norm, gdn_w_out,
           hgrn_w_in, hgrn_lb, hgrn_norm, hgrn_w_out,
           sc_w_in, sc_conv, sc_w_out,
           mla_w_in, mla_q_norm, mla_w_uq, mla_kv_norm, mla_w_ukv, mla_w_o,
           moe_router, moe_w_gate, moe_w_up, moe_w_down):
    for i in range(DEPTH):
        m, j = i % N_MIXERS, i // N_MIXERS
        g = norm_mix[i]
        if m == 0:
            x = gdn_mixer(x, g, gdn_w_in[j], gdn_conv[j], gdn_A_log[j], gdn_dt_bias[j], gdn_norm[j], gdn_w_out[j])
        elif m == 1:
            x = hgrn2_mixer(x, g, i, hgrn_w_in[j], hgrn_lb, hgrn_norm[j], hgrn_w_out[j])
        elif m == 2:
            x = shortconv_mixer(x, g, sc_w_in[j], sc_conv[j], sc_w_out[j])
        else:
            x = mla_mixer(x, g, positions, mla_w_in[j], mla_q_norm[j], mla_w_uq[j],
                          mla_kv_norm[j], mla_w_ukv[j], mla_w_o[j])
        x = expert_choice_ffn(x, norm_ffn[i], moe_router[i], moe_w_gate[i], moe_w_up[i], moe_w_down[i])
    return final_norm(x, norm_final)
```

```python
import functools
import math

import jax
import jax.numpy as jnp
from jax import lax
from jax.experimental import pallas as pl
from jax.experimental.pallas import tpu as pltpu

D_MODEL = 1024
BATCH = 4
SEQ = 8192
DEPTH = 4
N_MIXERS = 4
EPS = 1e-6
GDN_HEADS = 8
GDN_DK = 128
GDN_DV = 128
GDN_CONV = 5
HGRN_EXPAND = 128
HGRN_HEADS = D_MODEL // HGRN_EXPAND
HGRN_DF = HGRN_EXPAND
HGRN_DI = D_MODEL // HGRN_HEADS
HGRN_F = HGRN_HEADS * HGRN_DF
SC_WIDTH = 3
MLA_HEADS = 16
MLA_NOPE = 64
MLA_ROPE = 32
MLA_V = 64
MLA_Q_LORA = 384
MLA_KV_LORA = 256
ROPE_THETA = 10000.0
N_EXPERTS = 16
D_EXPERT = 2048
CAPACITY_FACTOR = 2


def rms_norm(x, g):
    xf = x.astype(jnp.float32)
    y = xf * lax.rsqrt(jnp.mean(xf * xf, axis=-1, keepdims=True) + EPS)
    return (y * g.astype(jnp.float32)).astype(x.dtype)


def l2_normalize(x):
    xf = x.astype(jnp.float32)
    return xf * lax.rsqrt(jnp.sum(xf * xf, axis=-1, keepdims=True) + EPS)


def centred_depthwise_conv(x, w):
    K, C = w.shape
    return lax.conv_general_dilated(
        x, w[:, None, :], window_strides=(1,), padding=[(K // 2, K // 2)],
        dimension_numbers=('NWC', 'WIO', 'NWC'), feature_group_count=C)


LANES = 128
GDN_BLOCK = 16
GDN_TILE = 256
GDN_GROUP = 4


def _gdn_kernel(*refs):
    C, G, H = GDN_BLOCK, GDN_GROUP, GDN_HEADS
    per_dir = 3 * G + 1
    in_refs, out_refs = refs[:2 * per_dir], refs[2 * per_dir:2 * per_dir + 2]
    scratch = refs[2 * per_dir + 2:]
    state_ref, us_ref, ws_ref, gc_ref = scratch[0], scratch[13], scratch[14], scratch[15]
    u_ref, w_ref, qe_ref, kd_ref, a_ref, dec_ref = scratch[1:7]
    u_rd, w_rd, qe_rd, kd_rd, a_rd, dec_rd = scratch[7:13]
    handoff = tuple(zip(scratch[1:7], scratch[7:13]))
    T = in_refs[0].shape[1]
    nb = T // C
    bf16 = jnp.bfloat16
    scale = GDN_DK ** -0.5
    hg = pl.program_id(1)

    @pl.when(pl.program_id(2) == 0)
    def _():
        state_ref[...] = jnp.zeros_like(state_ref)
        for _, dst in handoff:
            dst[...] = jnp.zeros_like(dst)

    slab = lambda r: pl.ds(r, nb, stride=C)
    ones = jnp.ones((LANES, LANES), bf16)
    lane = lax.broadcasted_iota(jnp.int32, (nb, LANES), 1)
    rowsum = lambda x: jnp.broadcast_to(jnp.sum(x, axis=-1, keepdims=True), (nb, LANES))

    def pick(x, idx):
        return rowsum(jnp.where(lane == idx, x, 0.0))

    orders = (list(range(C)), list(range(C - 1, -1, -1)))
    chains = [(d, g) for d in range(2) for g in range(G)]
    sc_refs = [in_refs[d * per_dir + 3 * G] for d in range(2)]
    qkv_refs = {(d, g): [in_refs[d * per_dir + j * G + g] for j in range(3)] for d, g in chains}

    for d, g in chains:
        run = None
        for r in orders[d]:
            g_r = pick(sc_refs[d][0, slab(r), :], d * H + hg * G + g)
            run = g_r if run is None else run + g_r
            gc_ref[d, g, r] = run
        dec_ref[d, g] = jnp.exp(run)

    def block_phase(n):
        dots = {}
        for d, g in chains:
            order = orders[d]
            q_ref, k_ref, v_ref = qkv_refs[d, g]
            q_r = q_ref[0, slab(order[n]), :] * scale
            k_r = k_ref[0, slab(order[n]), :]
            k_prev = [k_ref[0, slab(r2), :] for r2 in order[:n]]
            prods = [q_r * k_r] + [q_r * k_2 for k_2 in k_prev] + [k_r * k_2 for k_2 in k_prev]
            dots[d, g] = jnp.dot(jnp.concatenate(prods, axis=0).astype(bf16), ones,
                                 preferred_element_type=jnp.float32)
        for d, g in chains:
            order = orders[d]
            r = order[n]
            q_ref, k_ref, v_ref = qkv_refs[d, g]
            gc, us, ws = gc_ref.at[d, g], us_ref.at[d, g], ws_ref.at[d, g]
            dt = dots[d, g]
            q_r = q_ref[0, slab(r), :] * scale
            k_r = k_ref[0, slab(r), :]
            beta_r = pick(sc_refs[d][0, slab(r), :], 2 * H + d * H + hg * G + g)
            gc_r = gc[r]
            eg_r = jnp.exp(gc_r)
            u_r = beta_r * v_ref[0, slab(r), :]
            w_r = beta_r * eg_r * k_r
            a_row = jnp.where(lane == r, dt[:nb], 0.0)
            for m, r2 in enumerate(order[:n]):
                decay = jnp.exp(gc_r - gc[r2])
                l_rr = beta_r * decay * dt[(1 + n + m) * nb:(2 + n + m) * nb]
                u_r = u_r - l_rr * us[r2]
                w_r = w_r - l_rr * ws[r2]
                a_row = jnp.where(lane == r2, decay * dt[(1 + m) * nb:(2 + m) * nb], a_row)
            us[r] = u_r
            ws[r] = w_r
            u_ref[d, g, slab(r), :] = u_r
            w_ref[d, g, slab(r), :] = w_r
            a_ref[d, g, slab(r), :] = a_row
            qe_ref[d, g, slab(r), :] = q_r * eg_r
            kd_ref[d, g, slab(r), :] = k_r * jnp.exp(gc[order[-1]] - gc_r)

    def state_phase(i):
        blks = [i, nb - 1 - i]
        rows = [pl.ds(blk * C, C) for blk in blks]
        r1s = {}
        for d, g in chains:
            wq = jnp.concatenate([w_rd[d, g, rows[d], :], qe_rd[d, g, rows[d], :]], axis=0).astype(bf16)
            r1s[d, g] = jnp.dot(wq, state_ref[d, g].astype(bf16), preferred_element_type=jnp.float32)
        v_news = {c: (u_rd[c[0], c[1], rows[c[0]], :] - r1s[c][:C]).astype(bf16) for c in chains}
        intra, upd = {}, {}
        for d, g in chains:
            a_blk = a_rd[d, g, rows[d], :][:, :C].astype(bf16)
            intra[d, g] = jnp.dot(a_blk, v_news[d, g], preferred_element_type=jnp.float32)
            upd[d, g] = lax.dot_general(kd_rd[d, g, rows[d], :].astype(bf16), v_news[d, g],
                                        (((0,), (0,)), ((), ())), preferred_element_type=jnp.float32)
        for d, g in chains:
            out_refs[d][0, rows[d], g * LANES:(g + 1) * LANES] = r1s[d, g][C:] + intra[d, g]
            state_ref[d, g] = state_ref[d, g] * dec_rd[d, g, pl.ds(blks[d], 1), :] + upd[d, g]

    for n in range(max(C, nb)):
        if n < nb:
            state_phase(n)
        if n < C:
            block_phase(n)
    for src, dst in handoff:
        dst[...] = src[...]


def gdn_bidir(qkv, scal):
    B, S, _ = qkv.shape
    H, G = GDN_HEADS, GDN_GROUP
    T = min(GDN_TILE, S)
    nT, nG = S // T, H // G
    nb = T // GDN_BLOCK
    fwd_in = lambda t: jnp.minimum(t, nT - 1)
    bwd_in = lambda t: jnp.maximum(nT - 1 - t, 0)
    fwd_out = lambda t: jnp.maximum(t - 1, 0)
    bwd_out = lambda t: jnp.minimum(nT - t, nT - 1)
    blk = lambda tile, off: pl.BlockSpec((1, T, LANES), lambda b, hg, t: (b, tile(t), off + hg * G))
    in_specs, operands = [], []
    for tile in (fwd_in, bwd_in):
        for j in range(3):
            for g in range(G):
                in_specs.append(blk(tile, j * H + g))
                operands.append(qkv)
        in_specs.append(pl.BlockSpec((1, T, LANES), lambda b, hg, t, tile=tile: (b, tile(t), 0)))
        operands.append(scal)
    out_specs = [pl.BlockSpec((1, T, G * LANES), lambda b, hg, t: (b, fwd_out(t), hg)),
                 pl.BlockSpec((1, T, G * LANES), lambda b, hg, t: (b, bwd_out(t), hg))]
    out_sd = jax.ShapeDtypeStruct((B, S, H * LANES), jnp.float32)
    tile_buf = pltpu.VMEM((2, G, T, LANES), jnp.float32)
    dec_buf = pltpu.VMEM((2, G, nb, LANES), jnp.float32)
    slabs = pltpu.VMEM((2, G, GDN_BLOCK, nb, LANES), jnp.float32)
    return pl.pallas_call(
        _gdn_kernel,
        grid=(B, nG, nT + 1),
        in_specs=in_specs,
        out_specs=out_specs,
        out_shape=[out_sd, out_sd],
        scratch_shapes=[pltpu.VMEM((2, G, LANES, LANES), jnp.float32),
                        tile_buf, tile_buf, tile_buf, tile_buf, tile_buf, dec_buf,
                        tile_buf, tile_buf, tile_buf, tile_buf, tile_buf, dec_buf,
                        slabs, slabs, slabs],
        compiler_params=pltpu.CompilerParams(
            dimension_semantics=("parallel", "parallel", "arbitrary"),
            vmem_limit_bytes=48 * 1024 * 1024),
        name="gdn_bidir",
    )(*operands)


def gdn_mixer(x, gain, w_in, conv_w, A_log, dt_bias, norm_g, w_out):
    B, S, _ = x.shape
    H, dk, dv = GDN_HEADS, GDN_DK, GDN_DV
    n_main = 2 * H * dk + 2 * H * dv
    main = fused_matmul(x, w_in[:, :n_main], gain=gain)
    ab = fused_matmul(x, w_in[:, n_main:], gain=gain)
    qkv, z = main[..., :2 * H * dk + H * dv], main[..., 2 * H * dk + H * dv:]
    a, b = ab[..., :2 * H], ab[..., 2 * H:]
    qkv = jax.nn.silu(centred_depthwise_conv(qkv, conv_w))
    q, k, v = jnp.split(qkv, [H * dk, 2 * H * dk], axis=-1)
    q = l2_normalize(q.reshape(B, S, H, dk)).reshape(B, S, H * dk)
    k = l2_normalize(k.reshape(B, S, H, dk)).reshape(B, S, H * dk)
    a = a.astype(jnp.float32).reshape(B, S, 2, H)
    b = b.astype(jnp.float32).reshape(B, S, 2, H)
    g = -jnp.exp(A_log.astype(jnp.float32)) * jax.nn.softplus(a + dt_bias.astype(jnp.float32))
    beta = jax.nn.sigmoid(b)
    scal = jnp.concatenate([g.reshape(B, S, 2 * H), beta.reshape(B, S, 2 * H),
                            jnp.zeros((B, S, LANES - 4 * H), jnp.float32)], axis=-1)
    o_fwd, o_bwd = gdn_bidir(jnp.concatenate([q, k, v], axis=-1), scal)
    o = (o_fwd + o_bwd).reshape(B, S, H, dv)
    o = rms_norm(o, norm_g) * jax.nn.silu(z.reshape(B, S, H, dv))
    return fused_matmul(o.reshape(B, S, H * dv), w_out, residual=x)


HGRN_BLOCK = 16
HGRN_TILE = 256
HGRN_GROUP = 4


def _hgrn_kernel(*refs):
    C, G = HGRN_BLOCK, HGRN_GROUP
    per_dir = 3 * G + 1
    in_refs, out_refs = refs[:2 * per_dir], refs[2 * per_dir:2 * per_dir + 2]
    scratch = refs[2 * per_dir + 2:]
    state_ref, b_ref, kk_ref = scratch[0], scratch[11], scratch[12]
    qe_ref, ke_ref, od_ref, vv_ref, dec_ref = scratch[1:6]
    qe_rd, ke_rd, od_rd, vv_rd, dec_rd = scratch[6:11]
    handoff = tuple(zip(scratch[1:6], scratch[6:11]))
    T = in_refs[0].shape[1]
    nb = T // C
    bf16 = jnp.bfloat16
    chains = [(d, g) for d in range(2) for g in range(G)]
    qvl_refs = {(d, g): [in_refs[d * per_dir + j * G + g] for j in range(3)] for d, g in chains}
    lb_refs = [in_refs[d * per_dir + 3 * G] for d in range(2)]

    @pl.when(pl.program_id(2) == 0)
    def _():
        state_ref[...] = jnp.zeros_like(state_ref)
        for _, dst in handoff:
            dst[...] = jnp.zeros_like(dst)

    slab = lambda r: pl.ds(r, nb, stride=C)
    ones = jnp.ones((LANES, LANES), bf16)

    def log_f(x, log_lb, log_1m_lb):
        log_sig = jnp.minimum(x, 0.0) - jnp.log(1.0 + jnp.exp(-jnp.abs(x)))
        a, b = log_lb, log_1m_lb + log_sig
        return jnp.maximum(a, b) + jnp.log(1.0 + jnp.exp(-jnp.abs(a - b)))

    for d, g in chains:
        order = list(range(C)) if d == 0 else list(range(C - 1, -1, -1))
        l_ref = qvl_refs[d, g][2]
        log_lb = lb_refs[d][0:1, g * LANES:(g + 1) * LANES]
        log_1m_lb = lb_refs[d][1:2, g * LANES:(g + 1) * LANES]
        run = None
        for r in order:
            lf = log_f(l_ref[0, slab(r), :], log_lb, log_1m_lb)
            run = lf if run is None else run + lf
            b_ref[d, g, r] = run
            kk_ref[d, g, r] = 1.0 - jnp.exp(lf)
        dec_ref[d, g] = jnp.exp(run)

    def block_phase(r):
        scores, partners = {}, {}
        for d, g in chains:
            q_ref, v_ref, _ = qvl_refs[d, g]
            bs, kk = b_ref.at[d, g], kk_ref.at[d, g]
            q_r = q_ref[0, slab(r), :]
            b_r = bs[r]
            btot = bs[C - 1 if d == 0 else 0]
            qe_ref[d, g, slab(r), :] = q_r * jnp.exp(b_r)
            ke_ref[d, g, slab(r), :] = kk[r] * jnp.exp(btot - b_r)
            vv_ref[d, g, slab(r), :] = v_ref[0, slab(r), :]
            partners[d, g] = list(range(r + 1) if d == 0 else range(r, C))
            terms = [q_r * kk[r2] * jnp.exp(b_r - bs[r2]) for r2 in partners[d, g]]
            scores[d, g] = jnp.dot(jnp.concatenate(terms, axis=0).astype(bf16), ones,
                                   preferred_element_type=jnp.float32)
        for d, g in chains:
            v_ref = qvl_refs[d, g][1]
            acc = None
            for m, r2 in enumerate(partners[d, g]):
                term = scores[d, g][m * nb:(m + 1) * nb] * v_ref[0, slab(r2), :]
                acc = term if acc is None else acc + term
            od_ref[d, g, slab(r), :] = acc

    def state_phase(i):
        blks = [i, nb - 1 - i]
        rows = [pl.ds(blk * C, C) for blk in blks]
        inter, upd = {}, {}
        for d, g in chains:
            qe = qe_rd[d, g, rows[d], :].astype(bf16)
            inter[d, g] = lax.dot_general(qe, state_ref[d, g].astype(bf16), (((1,), (1,)), ((), ())),
                                          preferred_element_type=jnp.float32)
            upd[d, g] = lax.dot_general(vv_rd[d, g, rows[d], :].astype(bf16), ke_rd[d, g, rows[d], :].astype(bf16),
                                        (((0,), (0,)), ((), ())), preferred_element_type=jnp.float32)
        for d, g in chains:
            out_refs[d][0, rows[d], g * LANES:(g + 1) * LANES] = od_rd[d, g, rows[d], :] + inter[d, g]
            state_ref[d, g] = state_ref[d, g] * dec_rd[d, g, pl.ds(blks[d], 1), :] + upd[d, g]

    for n in range(max(C, nb)):
        if n < nb:
            state_phase(n)
        if n < C:
            block_phase(n)
    for src, dst in handoff:
        dst[...] = src[...]


def hgrn2_bidir(proj, log_lb):
    B, S, _ = proj.shape
    H = HGRN_HEADS
    T = min(HGRN_TILE, S)
    G = HGRN_GROUP
    nT, nG = S // T, H // G
    nb = T // HGRN_BLOCK
    fwd_in = lambda t: jnp.minimum(t, nT - 1)
    bwd_in = lambda t: jnp.maximum(nT - 1 - t, 0)
    fwd_out = lambda t: jnp.maximum(t - 1, 0)
    bwd_out = lambda t: jnp.minimum(nT - t, nT - 1)
    blk = lambda tile, off: pl.BlockSpec((1, T, LANES), lambda b, hg, t: (b, tile(t), off + hg * G))
    in_specs, operands = [], []
    for d, tile in enumerate((fwd_in, bwd_in)):
        for off in (0, H, (3 + d) * H):
            for g in range(G):
                in_specs.append(blk(tile, off + g))
                operands.append(proj)
        in_specs.append(pl.BlockSpec((None, 2, G * LANES), lambda b, hg, t, d=d: (d, 0, hg)))
        operands.append(log_lb)
    out_specs = [pl.BlockSpec((1, T, G * LANES), lambda b, hg, t: (b, fwd_out(t), hg)),
                 pl.BlockSpec((1, T, G * LANES), lambda b, hg, t: (b, bwd_out(t), hg))]
    out_sd = jax.ShapeDtypeStruct((B, S, H * LANES), jnp.float32)
    tile_buf = pltpu.VMEM((2, G, T, LANES), jnp.float32)
    dec_buf = pltpu.VMEM((2, G, nb, LANES), jnp.float32)
    slabs = pltpu.VMEM((2, G, HGRN_BLOCK, nb, LANES), jnp.float32)
    return pl.pallas_call(
        _hgrn_kernel,
        grid=(B, nG, nT + 1),
        in_specs=in_specs,
        out_specs=out_specs,
        out_shape=[out_sd, out_sd],
        scratch_shapes=[pltpu.VMEM((2, G, LANES, LANES), jnp.float32),
                        tile_buf, tile_buf, tile_buf, tile_buf, dec_buf,
                        tile_buf, tile_buf, tile_buf, tile_buf, dec_buf,
                        slabs, slabs],
        compiler_params=pltpu.CompilerParams(
            dimension_semantics=("parallel", "parallel", "arbitrary"),
            vmem_limit_bytes=48 * 1024 * 1024),
        name="hgrn2_bidir",
    )(*operands)


def hgrn2_mixer(x, gain, layer, w_in, lb_table, norm_g, w_out):
    B, S, _ = x.shape
    H, dF, dI = HGRN_HEADS, HGRN_DF, HGRN_DI
    proj = fused_matmul(x, w_in, gain=gain)
    gate = proj[..., HGRN_F + H * dI:HGRN_F + 2 * H * dI]
    lb_w = jax.nn.softmax(lb_table.astype(jnp.float32), axis=1)
    lb = (jnp.cumsum(lb_w, axis=1) - lb_w[:, :1])[:, layer]
    log_lb = jnp.stack([jnp.log(lb), jnp.log1p(-lb)], axis=1)
    o_fwd, o_bwd = hgrn2_bidir(proj, log_lb)
    o = (o_fwd + o_bwd).reshape(B, S, H, dI)
    o = rms_norm(o, norm_g) * jax.nn.silu(gate.reshape(B, S, H, dI))
    return fused_matmul(o.reshape(B, S, H * dI), w_out, residual=x)


def shortconv_mixer(x, gain, w_in, conv_w, w_out):
    gb, gc, u = jnp.split(fused_matmul(x, w_in, gain=gain), 3, axis=-1)
    return fused_matmul(gb * centred_depthwise_conv(gc * u, conv_w), w_out, residual=x)


def rope(x, pos):
    half = x.shape[-1] // 2
    inv = ROPE_THETA ** (-jnp.arange(half, dtype=jnp.float32) / half)
    ang = pos.astype(jnp.float32)[:, :, None, None] * inv
    cos, sin = jnp.cos(ang), jnp.sin(ang)
    xf = x.astype(jnp.float32)
    x1, x2 = xf[..., :half], xf[..., half:]
    return jnp.concatenate([x1 * cos - x2 * sin, x1 * sin + x2 * cos], axis=-1).astype(x.dtype)


MLA_QK_PAD = 128
MLA_TQ = 512
MLA_TK = 2048


def _mla_attn_kernel(q_ref, k_ref, v_ref, o_ref, *, tk):
    tq = q_ref.shape[1]
    nk = k_ref.shape[1] // tk
    head_lanes = [slice(hh * MLA_QK_PAD, (hh + 1) * MLA_QK_PAD) for hh in range(2)]
    qs = [q_ref[0, :, lanes] for lanes in head_lanes]

    def body(j, carry):
        rows = pl.ds(pl.multiple_of(j * tk, tk), tk)
        v = v_ref[0, rows, :]
        new = []
        for hh in range(2):
            m, l, acc = carry[hh]
            k = k_ref[0, rows, head_lanes[hh]]
            s = lax.dot_general(qs[hh], k, (((1,), (1,)), ((), ())), preferred_element_type=jnp.float32)
            m_new = jnp.maximum(m, jnp.max(s, axis=-1, keepdims=True))
            p = jnp.exp(s - m_new)
            alpha = jnp.exp(m - m_new)
            l = alpha * l + jnp.sum(p, axis=-1, keepdims=True)
            acc = alpha * acc + jnp.dot(p.astype(jnp.bfloat16), v, preferred_element_type=jnp.float32)
            new.append((m_new, l, acc))
        return tuple(new)

    init = (jnp.full((tq, 1), -jnp.inf, jnp.float32), jnp.zeros((tq, 1), jnp.float32),
            jnp.zeros((tq, 2 * MLA_V), jnp.float32))
    (_, l0, acc0), (_, l1, acc1) = lax.fori_loop(0, nk, body, (init, init))
    lane = lax.broadcasted_iota(jnp.int32, (tq, 2 * MLA_V), 1)
    o_ref[0] = jnp.where(lane < MLA_V, acc0 / l0, acc1 / l1).astype(o_ref.dtype)


def mla_attention(q_nope, q_rope, k_nope, k_rope, v):
    B, S, H, _ = q_nope.shape
    scale = (MLA_NOPE + MLA_ROPE) ** -0.5
    pad = MLA_QK_PAD - MLA_NOPE - MLA_ROPE
    bf16 = jnp.bfloat16
    qf = jnp.concatenate([q_nope, q_rope, jnp.zeros((B, S, H, pad), q_nope.dtype)], axis=-1) * scale
    kf = jnp.concatenate([k_nope, jnp.broadcast_to(k_rope[:, :, None, :], (B, S, H, MLA_ROPE)),
                          jnp.zeros((B, S, H, pad), k_nope.dtype)], axis=-1)
    qf = qf.astype(bf16).reshape(B, S, H * MLA_QK_PAD)
    kf = kf.astype(bf16).reshape(B, S, H * MLA_QK_PAD)
    vf = v.astype(bf16).reshape(B, S, H * MLA_V)
    tq, tk = min(MLA_TQ, S), min(MLA_TK, S)
    out = pl.pallas_call(
        functools.partial(_mla_attn_kernel, tk=tk),
        grid=(B, H // 2, S // tq),
        in_specs=[pl.BlockSpec((1, tq, 2 * MLA_QK_PAD), lambda b, h, i: (b, i, h)),
                  pl.BlockSpec((1, S, 2 * MLA_QK_PAD), lambda b, h, i: (b, 0, h)),
                  pl.BlockSpec((1, S, 2 * MLA_V), lambda b, h, i: (b, 0, h))],
        out_specs=pl.BlockSpec((1, tq, 2 * MLA_V), lambda b, h, i: (b, i, h)),
        out_shape=jax.ShapeDtypeStruct((B, S, H * MLA_V), jnp.float32),
        compiler_params=pltpu.CompilerParams(
            dimension_semantics=("parallel", "parallel", "arbitrary"),
            vmem_limit_bytes=48 * 1024 * 1024),
        name="mla_attention",
    )(qf, kf, vf)
    return out.reshape(B, S, H, MLA_V)


def mla_mixer(x, gain, pos, w_in, q_norm, w_uq, kv_norm, w_ukv, w_o):
    B, S, _ = x.shape
    H = MLA_HEADS
    cq, ckv, kr = jnp.split(fused_matmul(x, w_in, gain=gain), [MLA_Q_LORA, MLA_Q_LORA + MLA_KV_LORA], axis=-1)
    q = fused_matmul(cq, w_uq, gain=q_norm).reshape(B, S, H, MLA_NOPE + MLA_ROPE)
    q_nope, q_rope = q[..., :MLA_NOPE], rope(q[..., MLA_NOPE:], pos)
    kv = fused_matmul(ckv, w_ukv, gain=kv_norm).reshape(B, S, H, MLA_NOPE + MLA_V)
    k_nope, v = kv[..., :MLA_NOPE], kv[..., MLA_NOPE:]
    k_rope = rope(kr[:, :, None, :], pos)[:, :, 0, :]
    o = mla_attention(q_nope, q_rope, k_nope, k_rope, v)
    return fused_matmul(o.reshape(B, S, H * MLA_V), w_o, residual=x)


ROUTE_TM = 1024


def _route_kernel(x_ref, g_ref, wr_ref, h_ref, aff_ref):
    x = x_ref[...]
    h = (x * lax.rsqrt(jnp.mean(x * x, axis=-1, keepdims=True) + EPS) * g_ref[...]).astype(jnp.bfloat16)
    h_ref[...] = h
    logits = jnp.dot(h, wr_ref[...].astype(jnp.bfloat16), preferred_element_type=jnp.float32)
    e = jnp.exp(logits - jnp.max(logits, axis=-1, keepdims=True))
    aff_ref[...] = e / jnp.sum(e, axis=-1, keepdims=True)


def route(x, gain, w_router):
    B, S, D = x.shape
    E = w_router.shape[-1]
    M = B * S
    tm = min(ROUTE_TM, M)
    h, aff = pl.pallas_call(
        _route_kernel,
        grid=(M // tm,),
        in_specs=[pl.BlockSpec((tm, D), lambda i: (i, 0)),
                  pl.BlockSpec((1, D), lambda i: (0, 0)),
                  pl.BlockSpec((D, E), lambda i: (0, 0))],
        out_specs=[pl.BlockSpec((tm, D), lambda i: (i, 0)),
                   pl.BlockSpec((tm, E), lambda i: (i, 0))],
        out_shape=[jax.ShapeDtypeStruct((M, D), jnp.bfloat16), jax.ShapeDtypeStruct((M, E), jnp.float32)],
        compiler_params=pltpu.CompilerParams(dimension_semantics=("parallel",)),
        name="route",
    )(x.reshape(M, D), gain.reshape(1, D).astype(jnp.float32), w_router)
    return h.reshape(B, S, D), aff.reshape(B, S, E)


def expert_choice_ffn(x, gain, w_router, w_gate, w_up, w_down):
    B, S, _ = x.shape
    cap = CAPACITY_FACTOR * S // N_EXPERTS
    h, aff = route(x, gain, w_router)
    gate, idx = lax.top_k(jnp.swapaxes(aff, 1, 2), cap)
    bi = jnp.arange(B)[:, None, None]
    ys = expert_ffn(h[bi, idx], gate, w_gate, w_up, w_down)
    return x.at[bi, idx].add(ys)


FFN_BATCH_ROWS = 2
FFN_TF = 512


def _expert_ffn_kernel(x_ref, g_ref, wg_ref, wu_ref, wd_ref, o_ref):
    f = pl.program_id(2)
    bb, _, cap, d = x_ref.shape
    x = x_ref[...].reshape(bb * cap, d)
    bf16 = jnp.bfloat16
    a = jnp.dot(x, wg_ref[0].astype(bf16), preferred_element_type=jnp.float32)
    u = jnp.dot(x, wu_ref[0].astype(bf16), preferred_element_type=jnp.float32)
    hid = (a * jax.nn.sigmoid(a) * u).astype(bf16)
    y = jnp.dot(hid, wd_ref[0].astype(bf16), preferred_element_type=jnp.float32).reshape(bb, 1, cap, d)

    @pl.when(f == 0)
    def _():
        o_ref[...] = y

    @pl.when(f > 0)
    def _():
        o_ref[...] += y

    @pl.when(f == pl.num_programs(2) - 1)
    def _():
        o_ref[...] = o_ref[...] * g_ref[...]


def expert_ffn(xs, gate, w_gate, w_up, w_down):
    B, E, cap, D = xs.shape
    F = w_gate.shape[-1]
    bb = min(FFN_BATCH_ROWS, B)
    tf = min(FFN_TF, F)
    return pl.pallas_call(
        _expert_ffn_kernel,
        grid=(E, B // bb, F // tf),
        in_specs=[pl.BlockSpec((bb, 1, cap, D), lambda e, b, f: (b, e, 0, 0)),
                  pl.BlockSpec((bb, 1, cap, 1), lambda e, b, f: (b, e, 0, 0)),
                  pl.BlockSpec((1, D, tf), lambda e, b, f: (e, 0, f)),
                  pl.BlockSpec((1, D, tf), lambda e, b, f: (e, 0, f)),
                  pl.BlockSpec((1, tf, D), lambda e, b, f: (e, f, 0))],
        out_specs=pl.BlockSpec((bb, 1, cap, D), lambda e, b, f: (b, e, 0, 0)),
        out_shape=jax.ShapeDtypeStruct((B, E, cap, D), jnp.float32),
        compiler_params=pltpu.CompilerParams(
            dimension_semantics=("parallel", "parallel", "arbitrary"),
            vmem_limit_bytes=56 * 1024 * 1024),
        name="expert_ffn",
    )(xs.astype(jnp.bfloat16), gate[..., None].astype(jnp.float32), w_gate, w_up, w_down)


PROJ_TM = 1024
PROJ_TN_CHOICES = (1024, 768, 512, 384, 256, 128)


def _proj_kernel(*refs, normed, with_residual):
    refs = list(refs)
    x = refs.pop(0)[...]
    if normed:
        g = refs.pop(0)[...]
        x = x * lax.rsqrt(jnp.mean(x * x, axis=-1, keepdims=True) + EPS) * g
    w = refs.pop(0)[...]
    y = jnp.dot(x.astype(jnp.bfloat16), w.astype(jnp.bfloat16), preferred_element_type=jnp.float32)
    if with_residual:
        y = y + refs.pop(0)[...]
    refs.pop(0)[...] = y


def fused_matmul(x, w, gain=None, residual=None):
    lead, K = x.shape[:-1], x.shape[-1]
    N = w.shape[-1]
    M = math.prod(lead)
    tm = min(PROJ_TM, M)
    tn = next((t for t in PROJ_TN_CHOICES if N % t == 0), N)
    operands = [x.reshape(M, K)]
    in_specs = [pl.BlockSpec((tm, K), lambda n, m: (m, 0))]
    if gain is not None:
        operands.append(gain.reshape(1, K).astype(jnp.float32))
        in_specs.append(pl.BlockSpec((1, K), lambda n, m: (0, 0)))
    operands.append(w)
    in_specs.append(pl.BlockSpec((K, tn), lambda n, m: (0, n)))
    if residual is not None:
        operands.append(residual.reshape(M, N))
        in_specs.append(pl.BlockSpec((tm, tn), lambda n, m: (m, n)))
    out = pl.pallas_call(
        functools.partial(_proj_kernel, normed=gain is not None, with_residual=residual is not None),
        grid=(N // tn, M // tm),
        in_specs=in_specs,
        out_specs=pl.BlockSpec((tm, tn), lambda n, m: (m, n)),
        out_shape=jax.ShapeDtypeStruct((M, N), jnp.float32),
        compiler_params=pltpu.CompilerParams(
            dimension_semantics=("parallel", "parallel"),
            vmem_limit_bytes=48 * 1024 * 1024),
        name="fused_matmul",
    )(*operands)
    return out.reshape(*lead, N)


def _final_norm_kernel(x_ref, g_ref, o_ref):
    x = x_ref[...]
    y = x * lax.rsqrt(jnp.mean(x * x, axis=-1, keepdims=True) + EPS)
    o_ref[...] = y * g_ref[...]


def final_norm(x, g):
    B, S, D = x.shape
    x2 = x.reshape(B * S, D)
    tm = 1024
    out = pl.pallas_call(
        _final_norm_kernel,
        grid=(B * S // tm,),
        in_specs=[pl.BlockSpec((tm, D), lambda i: (i, 0)),
                  pl.BlockSpec((1, D), lambda i: (0, 0))],
        out_specs=pl.BlockSpec((tm, D), lambda i: (i, 0)),
        out_shape=jax.ShapeDtypeStruct((B * S, D), x.dtype),
        name="final_norm",
    )(x2, g.reshape(1, D))
    return out.reshape(B, S, D)


def kernel(x, positions, norm_mix, norm_ffn, norm_final,
           gdn_w_in, gdn_conv, gdn_A_log, gdn_dt_bias, gdn_norm, gdn_w_out,
           hgrn_w_in, hgrn_lb, hgrn_norm, hgrn_w_out,
           sc_w_in, sc_conv, sc_w_out,
           mla_w_in, mla_q_norm, mla_w_uq, mla_kv_norm, mla_w_ukv, mla_w_o,
           moe_router, moe_w_gate, moe_w_up, moe_w_down):
    for i in range(DEPTH):
        m, j = i % N_MIXERS, i // N_MIXERS
        g = norm_mix[i]
        if m == 0:
            x = gdn_mixer(x, g, gdn_w_in[j], gdn_conv[j], gdn_A_log[j], gdn_dt_bias[j], gdn_norm[j], gdn_w_out[j])
        elif m == 1:
            x = hgrn2_mixer(x, g, i, hgrn_w_in[j], hgrn_lb, hgrn_norm[j], hgrn_w_out[j])
        elif m == 2:
            x = shortconv_mixer(x, g, sc_w_in[j], sc_conv[j], sc_w_out[j])
        else:
            x = mla_mixer(x, g, positions, mla_w_in[j], mla_q_norm[j], mla_w_uq[j],
                          mla_kv_norm[j], mla_w_ukv[j], mla_w_o[j])
        x = expert_choice_ffn(x, norm_ffn[i], moe_router[i], moe_w_gate[i], moe_w_up[i], moe_w_down[i])
    return final_norm(x, norm_final)
```

```python
import functools
import math

import jax
import jax.numpy as jnp
from jax import lax
from jax.experimental import pallas as pl
from jax.experimental.pallas import tpu as pltpu

D_MODEL = 1024
BATCH = 4
SEQ = 8192
DEPTH = 4
N_MIXERS = 4
EPS = 1e-6
GDN_HEADS = 8
GDN_DK = 128
GDN_DV = 128
GDN_CONV = 5
HGRN_EXPAND = 128
HGRN_HEADS = D_MODEL // HGRN_EXPAND
HGRN_DF = HGRN_EXPAND
HGRN_DI = D_MODEL // HGRN_HEADS
HGRN_F = HGRN_HEADS * HGRN_DF
SC_WIDTH = 3
MLA_HEADS = 16
MLA_NOPE = 64
MLA_ROPE = 32
MLA_V = 64
MLA_Q_LORA = 384
MLA_KV_LORA = 256
ROPE_THETA = 10000.0
N_EXPERTS = 16
D_EXPERT = 2048
CAPACITY_FACTOR = 2


def rms_norm(x, g):
    xf = x.astype(jnp.float32)
    y = xf * lax.rsqrt(jnp.mean(xf * xf, axis=-1, keepdims=True) + EPS)
    return (y * g.astype(jnp.float32)).astype(x.dtype)


def l2_normalize(x):
    xf = x.astype(jnp.float32)
    return xf * lax.rsqrt(jnp.sum(xf * xf, axis=-1, keepdims=True) + EPS)


def centred_depthwise_conv(x, w):
    K, C = w.shape
    return lax.conv_general_dilated(
        x, w[:, None, :], window_strides=(1,), padding=[(K // 2, K // 2)],
        dimension_numbers=('NWC', 'WIO', 'NWC'), feature_group_count=C)


LANES = 128
SUBLANES = 8
CONV_TILE = 256
CONV_CHANNELS = 1024


def _row_conv(prev, x, nxt, w):
    T, K = x.shape[0], w.shape[0]
    ext = jnp.concatenate([prev, x, nxt], axis=0)
    acc = None
    for j in range(K):
        start = SUBLANES - K // 2 + j
        term = ext[start:start + T] * w[j:j + 1, :]
        acc = term if acc is None else acc + term
    return acc


def _halo_specs(S, T, ct):
    r = T // SUBLANES
    return [pl.BlockSpec((1, SUBLANES, ct), lambda b, t, c: (b, jnp.maximum(t * r - 1, 0), c)),
            pl.BlockSpec((1, T, ct), lambda b, t, c: (b, t, c)),
            pl.BlockSpec((1, SUBLANES, ct), lambda b, t, c: (b, jnp.minimum((t + 1) * r, S // SUBLANES - 1), c))]


def _gdn_prep_kernel(xp_ref, x_ref, xn_ref, w_ref, o_ref, *, n_norm_chunks):
    t, c = pl.program_id(1), pl.program_id(2)
    prev = jnp.where(t == 0, 0.0, xp_ref[0])
    nxt = jnp.where(t == pl.num_programs(1) - 1, 0.0, xn_ref[0])
    y = _row_conv(prev, x_ref[0], nxt, w_ref[...])
    y = y * jax.nn.sigmoid(y)
    heads = []
    for h in range(y.shape[-1] // LANES):
        yh = y[:, h * LANES:(h + 1) * LANES]
        heads.append(yh * lax.rsqrt(jnp.sum(yh * yh, axis=-1, keepdims=True) + EPS))
    o_ref[0] = jnp.where(c < n_norm_chunks, jnp.concatenate(heads, axis=-1), y)


def gdn_prep(qkv_raw, conv_w, n_norm):
    B, S, C = qkv_raw.shape
    T, ct = min(CONV_TILE, S), min(CONV_CHANNELS, C)
    return pl.pallas_call(
        functools.partial(_gdn_prep_kernel, n_norm_chunks=n_norm // ct),
        grid=(B, S // T, C // ct),
        in_specs=_halo_specs(S, T, ct) + [pl.BlockSpec((conv_w.shape[0], ct), lambda b, t, c: (0, c))],
        out_specs=pl.BlockSpec((1, T, ct), lambda b, t, c: (b, t, c)),
        out_shape=jax.ShapeDtypeStruct((B, S, C), jnp.float32),
        compiler_params=pltpu.CompilerParams(dimension_semantics=("parallel", "parallel", "parallel")),
        name="gdn_prep",
    )(qkv_raw, qkv_raw, qkv_raw, conv_w)


def _shortconv_gate_kernel(gb_ref, cp_ref, c_ref, cn_ref, up_ref, u_ref, un_ref, w_ref, o_ref):
    t = pl.program_id(1)
    first, last = t == 0, t == pl.num_programs(1) - 1
    prev = jnp.where(first, 0.0, cp_ref[0] * up_ref[0])
    nxt = jnp.where(last, 0.0, cn_ref[0] * un_ref[0])
    o_ref[0] = gb_ref[0] * _row_conv(prev, c_ref[0] * u_ref[0], nxt, w_ref[...])


def shortconv_gate(gb, gc, u, conv_w):
    B, S, C = gb.shape
    T, ct = min(CONV_TILE, S), min(CONV_CHANNELS, C)
    halo = _halo_specs(S, T, ct)
    return pl.pallas_call(
        _shortconv_gate_kernel,
        grid=(B, S // T, C // ct),
        in_specs=[halo[1]] + halo + halo + [pl.BlockSpec((conv_w.shape[0], ct), lambda b, t, c: (0, c))],
        out_specs=pl.BlockSpec((1, T, ct), lambda b, t, c: (b, t, c)),
        out_shape=jax.ShapeDtypeStruct((B, S, C), jnp.float32),
        compiler_params=pltpu.CompilerParams(dimension_semantics=("parallel", "parallel", "parallel")),
        name="shortconv_gate",
    )(gb, gc, gc, gc, u, u, u, conv_w)


GDN_BLOCK = 16
GDN_TILE = 256
GDN_GROUP = 4


def _gdn_kernel(*refs):
    C, G, H = GDN_BLOCK, GDN_GROUP, GDN_HEADS
    per_dir = 3 * G + 1
    in_refs, out_refs = refs[:2 * per_dir], refs[2 * per_dir:2 * per_dir + 2]
    scratch = refs[2 * per_dir + 2:]
    state_ref, us_ref, ws_ref, gc_ref = scratch[0], scratch[13], scratch[14], scratch[15]
    u_ref, w_ref, qe_ref, kd_ref, a_ref, dec_ref = scratch[1:7]
    u_rd, w_rd, qe_rd, kd_rd, a_rd, dec_rd = scratch[7:13]
    handoff = tuple(zip(scratch[1:7], scratch[7:13]))
    T = in_refs[0].shape[1]
    nb = T // C
    bf16 = jnp.bfloat16
    scale = GDN_DK ** -0.5
    hg = pl.program_id(1)

    @pl.when(pl.program_id(2) == 0)
    def _():
        state_ref[...] = jnp.zeros_like(state_ref)
        for _, dst in handoff:
            dst[...] = jnp.zeros_like(dst)

    slab = lambda r: pl.ds(r, nb, stride=C)
    ones = jnp.ones((LANES, LANES), bf16)
    lane = lax.broadcasted_iota(jnp.int32, (nb, LANES), 1)
    rowsum = lambda x: jnp.broadcast_to(jnp.sum(x, axis=-1, keepdims=True), (nb, LANES))

    def pick(x, idx):
        return rowsum(jnp.where(lane == idx, x, 0.0))

    orders = (list(range(C)), list(range(C - 1, -1, -1)))
    chains = [(d, g) for d in range(2) for g in range(G)]
    sc_refs = [in_refs[d * per_dir + 3 * G] for d in range(2)]
    qkv_refs = {(d, g): [in_refs[d * per_dir + j * G + g] for j in range(3)] for d, g in chains}

    for d, g in chains:
        run = None
        for r in orders[d]:
            g_r = pick(sc_refs[d][0, slab(r), :], d * H + hg * G + g)
            run = g_r if run is None else run + g_r
            gc_ref[d, g, r] = run
        dec_ref[d, g] = jnp.exp(run)

    def block_phase(n):
        dots = {}
        for d, g in chains:
            order = orders[d]
            q_ref, k_ref, v_ref = qkv_refs[d, g]
            q_r = q_ref[0, slab(order[n]), :] * scale
            k_r = k_ref[0, slab(order[n]), :]
            k_prev = [k_ref[0, slab(r2), :] for r2 in order[:n]]
            prods = [q_r * k_r] + [q_r * k_2 for k_2 in k_prev] + [k_r * k_2 for k_2 in k_prev]
            dots[d, g] = jnp.dot(jnp.concatenate(prods, axis=0).astype(bf16), ones,
                                 preferred_element_type=jnp.float32)
        for d, g in chains:
            order = orders[d]
            r = order[n]
            q_ref, k_ref, v_ref = qkv_refs[d, g]
            gc, us, ws = gc_ref.at[d, g], us_ref.at[d, g], ws_ref.at[d, g]
            dt = dots[d, g]
            q_r = q_ref[0, slab(r), :] * scale
            k_r = k_ref[0, slab(r), :]
            beta_r = pick(sc_refs[d][0, slab(r), :], 2 * H + d * H + hg * G + g)
            gc_r = gc[r]
            eg_r = jnp.exp(gc_r)
            u_r = beta_r * v_ref[0, slab(r), :]
            w_r = beta_r * eg_r * k_r
            a_row = jnp.where(lane == r, dt[:nb], 0.0)
            for m, r2 in enumerate(order[:n]):
                decay = jnp.exp(gc_r - gc[r2])
                l_rr = beta_r * decay * dt[(1 + n + m) * nb:(2 + n + m) * nb]
                u_r = u_r - l_rr * us[r2]
                w_r = w_r - l_rr * ws[r2]
                a_row = jnp.where(lane == r2, decay * dt[(1 + m) * nb:(2 + m) * nb], a_row)
            us[r] = u_r
            ws[r] = w_r
            u_ref[d, g, slab(r), :] = u_r
            w_ref[d, g, slab(r), :] = w_r
            a_ref[d, g, slab(r), :] = a_row
            qe_ref[d, g, slab(r), :] = q_r * eg_r
            kd_ref[d, g, slab(r), :] = k_r * jnp.exp(gc[order[-1]] - gc_r)

    def state_phase(i):
        blks = [i, nb - 1 - i]
        rows = [pl.ds(blk * C, C) for blk in blks]
        r1s = {}
        for d, g in chains:
            wq = jnp.concatenate([w_rd[d, g, rows[d], :], qe_rd[d, g, rows[d], :]], axis=0).astype(bf16)
            r1s[d, g] = jnp.dot(wq, state_ref[d, g].astype(bf16), preferred_element_type=jnp.float32)
        v_news = {c: (u_rd[c[0], c[1], rows[c[0]], :] - r1s[c][:C]).astype(bf16) for c in chains}
        intra, upd = {}, {}
        for d, g in chains:
            a_blk = a_rd[d, g, rows[d], :][:, :C].astype(bf16)
            intra[d, g] = jnp.dot(a_blk, v_news[d, g], preferred_element_type=jnp.float32)
            upd[d, g] = lax.dot_general(kd_rd[d, g, rows[d], :].astype(bf16), v_news[d, g],
                                        (((0,), (0,)), ((), ())), preferred_element_type=jnp.float32)
        for d, g in chains:
            out_refs[d][0, rows[d], g * LANES:(g + 1) * LANES] = r1s[d, g][C:] + intra[d, g]
            state_ref[d, g] = state_ref[d, g] * dec_rd[d, g, pl.ds(blks[d], 1), :] + upd[d, g]

    for n in range(max(C, nb)):
        if n < nb:
            state_phase(n)
        if n < C:
            block_phase(n)
    for src, dst in handoff:
        dst[...] = src[...]


def gdn_bidir(qkv, scal):
    B, S, _ = qkv.shape
    H, G = GDN_HEADS, GDN_GROUP
    T = min(GDN_TILE, S)
    nT, nG = S // T, H // G
    nb = T // GDN_BLOCK
    fwd_in = lambda t: jnp.minimum(t, nT - 1)
    bwd_in = lambda t: jnp.maximum(nT - 1 - t, 0)
    fwd_out = lambda t: jnp.maximum(t - 1, 0)
    bwd_out = lambda t: jnp.minimum(nT - t, nT - 1)
    blk = lambda tile, off: pl.BlockSpec((1, T, LANES), lambda b, hg, t: (b, tile(t), off + hg * G))
    in_specs, operands = [], []
    for tile in (fwd_in, bwd_in):
        for j in range(3):
            for g in range(G):
                in_specs.append(blk(tile, j * H + g))
                operands.append(qkv)
        in_specs.append(pl.BlockSpec((1, T, LANES), lambda b, hg, t, tile=tile: (b, tile(t), 0)))
        operands.append(scal)
    out_specs = [pl.BlockSpec((1, T, G * LANES), lambda b, hg, t: (b, fwd_out(t), hg)),
                 pl.BlockSpec((1, T, G * LANES), lambda b, hg, t: (b, bwd_out(t), hg))]
    out_sd = jax.ShapeDtypeStruct((B, S, H * LANES), jnp.float32)
    tile_buf = pltpu.VMEM((2, G, T, LANES), jnp.float32)
    dec_buf = pltpu.VMEM((2, G, nb, LANES), jnp.float32)
    slabs = pltpu.VMEM((2, G, GDN_BLOCK, nb, LANES), jnp.float32)
    return pl.pallas_call(
        _gdn_kernel,
        grid=(B, nG, nT + 1),
        in_specs=in_specs,
        out_specs=out_specs,
        out_shape=[out_sd, out_sd],
        scratch_shapes=[pltpu.VMEM((2, G, LANES, LANES), jnp.float32),
                        tile_buf, tile_buf, tile_buf, tile_buf, tile_buf, dec_buf,
                        tile_buf, tile_buf, tile_buf, tile_buf, tile_buf, dec_buf,
                        slabs, slabs, slabs],
        compiler_params=pltpu.CompilerParams(
            dimension_semantics=("parallel", "parallel", "arbitrary"),
            vmem_limit_bytes=48 * 1024 * 1024),
        name="gdn_bidir",
    )(*operands)


def gdn_mixer(x, gain, w_in, conv_w, A_log, dt_bias, norm_g, w_out):
    B, S, _ = x.shape
    H, dk, dv = GDN_HEADS, GDN_DK, GDN_DV
    n_qkv, n_z = 2 * H * dk + H * dv, H * dv
    qkv = fused_matmul(x, w_in[:, :n_qkv], gain=gain)
    z = fused_matmul(x, w_in[:, n_qkv:n_qkv + n_z], gain=gain)
    ab = fused_matmul(x, w_in[:, n_qkv + n_z:], gain=gain)
    a, b = ab[..., :2 * H], ab[..., 2 * H:]
    qkv = gdn_prep(qkv, conv_w, n_norm=2 * H * dk)
    a = a.astype(jnp.float32).reshape(B, S, 2, H)
    b = b.astype(jnp.float32).reshape(B, S, 2, H)
    g = -jnp.exp(A_log.astype(jnp.float32)) * jax.nn.softplus(a + dt_bias.astype(jnp.float32))
    beta = jax.nn.sigmoid(b)
    scal = jnp.concatenate([g.reshape(B, S, 2 * H), beta.reshape(B, S, 2 * H),
                            jnp.zeros((B, S, LANES - 4 * H), jnp.float32)], axis=-1)
    o_fwd, o_bwd = gdn_bidir(qkv, scal)
    return gated_out_proj(o_fwd, o_bwd, z, norm_g, w_out, x)


HGRN_BLOCK = 16
HGRN_TILE = 256
HGRN_GROUP = 4


def _hgrn_kernel(*refs):
    C, G = HGRN_BLOCK, HGRN_GROUP
    per_dir = 3 * G + 1
    in_refs, out_refs = refs[:2 * per_dir], refs[2 * per_dir:2 * per_dir + 2]
    scratch = refs[2 * per_dir + 2:]
    state_ref, b_ref, kk_ref = scratch[0], scratch[11], scratch[12]
    qe_ref, ke_ref, od_ref, vv_ref, dec_ref = scratch[1:6]
    qe_rd, ke_rd, od_rd, vv_rd, dec_rd = scratch[6:11]
    handoff = tuple(zip(scratch[1:6], scratch[6:11]))
    T = in_refs[0].shape[1]
    nb = T // C
    bf16 = jnp.bfloat16
    chains = [(d, g) for d in range(2) for g in range(G)]
    qvl_refs = {(d, g): [in_refs[d * per_dir + j * G + g] for j in range(3)] for d, g in chains}
    lb_refs = [in_refs[d * per_dir + 3 * G] for d in range(2)]

    @pl.when(pl.program_id(2) == 0)
    def _():
        state_ref[...] = jnp.zeros_like(state_ref)
        for _, dst in handoff:
            dst[...] = jnp.zeros_like(dst)

    slab = lambda r: pl.ds(r, nb, stride=C)
    ones = jnp.ones((LANES, LANES), bf16)

    def log_f(x, log_lb, log_1m_lb):
        log_sig = jnp.minimum(x, 0.0) - jnp.log(1.0 + jnp.exp(-jnp.abs(x)))
        a, b = log_lb, log_1m_lb + log_sig
        return jnp.maximum(a, b) + jnp.log(1.0 + jnp.exp(-jnp.abs(a - b)))

    for d, g in chains:
        order = list(range(C)) if d == 0 else list(range(C - 1, -1, -1))
        l_ref = qvl_refs[d, g][2]
        log_lb = lb_refs[d][0:1, g * LANES:(g + 1) * LANES]
        log_1m_lb = lb_refs[d][1:2, g * LANES:(g + 1) * LANES]
        run = None
        for r in order:
            lf = log_f(l_ref[0, slab(r), :], log_lb, log_1m_lb)
            run = lf if run is None else run + lf
            b_ref[d, g, r] = run
            kk_ref[d, g, r] = 1.0 - jnp.exp(lf)
        dec_ref[d, g] = jnp.exp(run)

    def block_phase(r):
        scores, partners = {}, {}
        for d, g in chains:
            q_ref, v_ref, _ = qvl_refs[d, g]
            bs, kk = b_ref.at[d, g], kk_ref.at[d, g]
            q_r = q_ref[0, slab(r), :]
            b_r = bs[r]
            btot = bs[C - 1 if d == 0 else 0]
            qe_ref[d, g, slab(r), :] = q_r * jnp.exp(b_r)
            ke_ref[d, g, slab(r), :] = kk[r] * jnp.exp(btot - b_r)
            vv_ref[d, g, slab(r), :] = v_ref[0, slab(r), :]
            partners[d, g] = list(range(r + 1) if d == 0 else range(r, C))
            terms = [q_r * kk[r2] * jnp.exp(b_r - bs[r2]) for r2 in partners[d, g]]
            scores[d, g] = jnp.dot(jnp.concatenate(terms, axis=0).astype(bf16), ones,
                                   preferred_element_type=jnp.float32)
        for d, g in chains:
            v_ref = qvl_refs[d, g][1]
            acc = None
            for m, r2 in enumerate(partners[d, g]):
                term = scores[d, g][m * nb:(m + 1) * nb] * v_ref[0, slab(r2), :]
                acc = term if acc is None else acc + term
            od_ref[d, g, slab(r), :] = acc

    def state_phase(i):
        blks = [i, nb - 1 - i]
        rows = [pl.ds(blk * C, C) for blk in blks]
        inter, upd = {}, {}
        for d, g in chains:
            qe = qe_rd[d, g, rows[d], :].astype(bf16)
            inter[d, g] = lax.dot_general(qe, state_ref[d, g].astype(bf16), (((1,), (1,)), ((), ())),
                                          preferred_element_type=jnp.float32)
            upd[d, g] = lax.dot_general(vv_rd[d, g, rows[d], :].astype(bf16), ke_rd[d, g, rows[d], :].astype(bf16),
                                        (((0,), (0,)), ((), ())), preferred_element_type=jnp.float32)
        for d, g in chains:
            out_refs[d][0, rows[d], g * LANES:(g + 1) * LANES] = od_rd[d, g, rows[d], :] + inter[d, g]
            state_ref[d, g] = state_ref[d, g] * dec_rd[d, g, pl.ds(blks[d], 1), :] + upd[d, g]

    for n in range(max(C, nb)):
        if n < nb:
            state_phase(n)
        if n < C:
            block_phase(n)
    for src, dst in handoff:
        dst[...] = src[...]


def hgrn2_bidir(proj, log_lb):
    B, S, _ = proj.shape
    H = HGRN_HEADS
    T = min(HGRN_TILE, S)
    G = HGRN_GROUP
    nT, nG = S // T, H // G
    nb = T // HGRN_BLOCK
    fwd_in = lambda t: jnp.minimum(t, nT - 1)
    bwd_in = lambda t: jnp.maximum(nT - 1 - t, 0)
    fwd_out = lambda t: jnp.maximum(t - 1, 0)
    bwd_out = lambda t: jnp.minimum(nT - t, nT - 1)
    blk = lambda tile, off: pl.BlockSpec((1, T, LANES), lambda b, hg, t: (b, tile(t), off + hg * G))
    in_specs, operands = [], []
    for d, tile in enumerate((fwd_in, bwd_in)):
        for off in (0, H, (2 + d) * H):
            for g in range(G):
                in_specs.append(blk(tile, off + g))
                operands.append(proj)
        in_specs.append(pl.BlockSpec((None, 2, G * LANES), lambda b, hg, t, d=d: (d, 0, hg)))
        operands.append(log_lb)
    out_specs = [pl.BlockSpec((1, T, G * LANES), lambda b, hg, t: (b, fwd_out(t), hg)),
                 pl.BlockSpec((1, T, G * LANES), lambda b, hg, t: (b, bwd_out(t), hg))]
    out_sd = jax.ShapeDtypeStruct((B, S, H * LANES), jnp.float32)
    tile_buf = pltpu.VMEM((2, G, T, LANES), jnp.float32)
    dec_buf = pltpu.VMEM((2, G, nb, LANES), jnp.float32)
    slabs = pltpu.VMEM((2, G, HGRN_BLOCK, nb, LANES), jnp.float32)
    return pl.pallas_call(
        _hgrn_kernel,
        grid=(B, nG, nT + 1),
        in_specs=in_specs,
        out_specs=out_specs,
        out_shape=[out_sd, out_sd],
        scratch_shapes=[pltpu.VMEM((2, G, LANES, LANES), jnp.float32),
                        tile_buf, tile_buf, tile_buf, tile_buf, dec_buf,
                        tile_buf, tile_buf, tile_buf, tile_buf, dec_buf,
                        slabs, slabs],
        compiler_params=pltpu.CompilerParams(
            dimension_semantics=("parallel", "parallel", "arbitrary"),
            vmem_limit_bytes=48 * 1024 * 1024),
        name="hgrn2_bidir",
    )(*operands)


def hgrn2_mixer(x, gain, layer, w_in, lb_table, norm_g, w_out):
    B, S, _ = x.shape
    H, dF, dI = HGRN_HEADS, HGRN_DF, HGRN_DI
    n_qv = HGRN_F + H * dI
    proj = fused_matmul(x, jnp.concatenate([w_in[:, :n_qv], w_in[:, n_qv + H * dI:]], axis=1), gain=gain)
    gate = fused_matmul(x, w_in[:, n_qv:n_qv + H * dI], gain=gain)
    lb_w = jax.nn.softmax(lb_table.astype(jnp.float32), axis=1)
    lb = (jnp.cumsum(lb_w, axis=1) - lb_w[:, :1])[:, layer]
    log_lb = jnp.stack([jnp.log(lb), jnp.log1p(-lb)], axis=1)
    o_fwd, o_bwd = hgrn2_bidir(proj, log_lb)
    return gated_out_proj(o_fwd, o_bwd, gate, norm_g, w_out, x)


def shortconv_mixer(x, gain, w_in, conv_w, w_out):
    D = x.shape[-1]
    gb, gc, u = (fused_matmul(x, w_in[:, j * D:(j + 1) * D], gain=gain) for j in range(3))
    return fused_matmul(shortconv_gate(gb, gc, u, conv_w), w_out, residual=x)


def rope(x, pos):
    half = x.shape[-1] // 2
    inv = ROPE_THETA ** (-jnp.arange(half, dtype=jnp.float32) / half)
    ang = pos.astype(jnp.float32)[:, :, None, None] * inv
    cos, sin = jnp.cos(ang), jnp.sin(ang)
    xf = x.astype(jnp.float32)
    x1, x2 = xf[..., :half], xf[..., half:]
    return jnp.concatenate([x1 * cos - x2 * sin, x1 * sin + x2 * cos], axis=-1).astype(x.dtype)


MLA_QK_PAD = 128
MLA_TQ = 512
MLA_TK = 2048


def _mla_attn_kernel(q_ref, k_ref, v_ref, o_ref, *, tk):
    tq = q_ref.shape[1]
    nk = k_ref.shape[1] // tk
    head_lanes = [slice(hh * MLA_QK_PAD, (hh + 1) * MLA_QK_PAD) for hh in range(2)]
    qs = [q_ref[0, :, lanes] for lanes in head_lanes]

    def body(j, carry):
        rows = pl.ds(pl.multiple_of(j * tk, tk), tk)
        v = v_ref[0, rows, :]
        new = []
        for hh in range(2):
            m, l, acc = carry[hh]
            k = k_ref[0, rows, head_lanes[hh]]
            s = lax.dot_general(qs[hh], k, (((1,), (1,)), ((), ())), preferred_element_type=jnp.float32)
            m_new = jnp.maximum(m, jnp.max(s, axis=-1, keepdims=True))
            p = jnp.exp(s - m_new)
            alpha = jnp.exp(m - m_new)
            l = alpha * l + jnp.sum(p, axis=-1, keepdims=True)
            acc = alpha * acc + jnp.dot(p.astype(jnp.bfloat16), v, preferred_element_type=jnp.float32)
            new.append((m_new, l, acc))
        return tuple(new)

    init = (jnp.full((tq, 1), -jnp.inf, jnp.float32), jnp.zeros((tq, 1), jnp.float32),
            jnp.zeros((tq, 2 * MLA_V), jnp.float32))
    (_, l0, acc0), (_, l1, acc1) = lax.fori_loop(0, nk, body, (init, init))
    lane = lax.broadcasted_iota(jnp.int32, (tq, 2 * MLA_V), 1)
    o_ref[0] = jnp.where(lane < MLA_V, acc0 / l0, acc1 / l1).astype(o_ref.dtype)


def mla_attention(q_nope, q_rope, k_nope, k_rope, v):
    B, S, H, _ = q_nope.shape
    scale = (MLA_NOPE + MLA_ROPE) ** -0.5
    pad = MLA_QK_PAD - MLA_NOPE - MLA_ROPE
    bf16 = jnp.bfloat16
    qf = jnp.concatenate([q_nope, q_rope, jnp.zeros((B, S, H, pad), q_nope.dtype)], axis=-1) * scale
    kf = jnp.concatenate([k_nope, jnp.broadcast_to(k_rope[:, :, None, :], (B, S, H, MLA_ROPE)),
                          jnp.zeros((B, S, H, pad), k_nope.dtype)], axis=-1)
    qf = qf.astype(bf16).reshape(B, S, H * MLA_QK_PAD)
    kf = kf.astype(bf16).reshape(B, S, H * MLA_QK_PAD)
    vf = v.astype(bf16).reshape(B, S, H * MLA_V)
    tq, tk = min(MLA_TQ, S), min(MLA_TK, S)
    out = pl.pallas_call(
        functools.partial(_mla_attn_kernel, tk=tk),
        grid=(B, H // 2, S // tq),
        in_specs=[pl.BlockSpec((1, tq, 2 * MLA_QK_PAD), lambda b, h, i: (b, i, h)),
                  pl.BlockSpec((1, S, 2 * MLA_QK_PAD), lambda b, h, i: (b, 0, h)),
                  pl.BlockSpec((1, S, 2 * MLA_V), lambda b, h, i: (b, 0, h))],
        out_specs=pl.BlockSpec((1, tq, 2 * MLA_V), lambda b, h, i: (b, i, h)),
        out_shape=jax.ShapeDtypeStruct((B, S, H * MLA_V), jnp.float32),
        compiler_params=pltpu.CompilerParams(
            dimension_semantics=("parallel", "parallel", "arbitrary"),
            vmem_limit_bytes=48 * 1024 * 1024),
        name="mla_attention",
    )(qf, kf, vf)
    return out.reshape(B, S, H, MLA_V)


def mla_mixer(x, gain, pos, w_in, q_norm, w_uq, kv_norm, w_ukv, w_o):
    B, S, _ = x.shape
    H = MLA_HEADS
    n_q, n_kv = MLA_Q_LORA, MLA_KV_LORA
    cq = fused_matmul(x, w_in[:, :n_q], gain=gain)
    ckv = fused_matmul(x, w_in[:, n_q:n_q + n_kv], gain=gain)
    kr = fused_matmul(x, w_in[:, n_q + n_kv:], gain=gain)
    q = fused_matmul(cq, w_uq, gain=q_norm).reshape(B, S, H, MLA_NOPE + MLA_ROPE)
    q_nope, q_rope = q[..., :MLA_NOPE], rope(q[..., MLA_NOPE:], pos)
    kv = fused_matmul(ckv, w_ukv, gain=kv_norm).reshape(B, S, H, MLA_NOPE + MLA_V)
    k_nope, v = kv[..., :MLA_NOPE], kv[..., MLA_NOPE:]
    k_rope = rope(kr[:, :, None, :], pos)[:, :, 0, :]
    o = mla_attention(q_nope, q_rope, k_nope, k_rope, v)
    return fused_matmul(o.reshape(B, S, H * MLA_V), w_o, residual=x)


ROUTE_TM = 1024


def _route_kernel(x_ref, g_ref, wr_ref, h_ref, aff_ref):
    x = x_ref[...]
    h = (x * lax.rsqrt(jnp.mean(x * x, axis=-1, keepdims=True) + EPS) * g_ref[...]).astype(jnp.bfloat16)
    h_ref[...] = h
    logits = jnp.dot(h, wr_ref[...].astype(jnp.bfloat16), preferred_element_type=jnp.float32)
    e = jnp.exp(logits - jnp.max(logits, axis=-1, keepdims=True))
    aff_ref[...] = e / jnp.sum(e, axis=-1, keepdims=True)


def route(x, gain, w_router):
    B, S, D = x.shape
    E = w_router.shape[-1]
    M = B * S
    tm = min(ROUTE_TM, M)
    h, aff = pl.pallas_call(
        _route_kernel,
        grid=(M // tm,),
        in_specs=[pl.BlockSpec((tm, D), lambda i: (i, 0)),
                  pl.BlockSpec((1, D), lambda i: (0, 0)),
                  pl.BlockSpec((D, E), lambda i: (0, 0))],
        out_specs=[pl.BlockSpec((tm, D), lambda i: (i, 0)),
                   pl.BlockSpec((tm, E), lambda i: (i, 0))],
        out_shape=[jax.ShapeDtypeStruct((M, D), jnp.bfloat16), jax.ShapeDtypeStruct((M, E), jnp.float32)],
        compiler_params=pltpu.CompilerParams(dimension_semantics=("parallel",)),
        name="route",
    )(x.reshape(M, D), gain.reshape(1, D).astype(jnp.float32), w_router)
    return h.reshape(B, S, D), aff.reshape(B, S, E)


def expert_choice_ffn(x, gain, w_router, w_gate, w_up, w_down):
    B, S, _ = x.shape
    cap = CAPACITY_FACTOR * S // N_EXPERTS
    h, aff = route(x, gain, w_router)
    gate, idx = lax.top_k(jnp.swapaxes(aff, 1, 2), cap)
    bi = jnp.arange(B)[:, None, None]
    ys = expert_ffn(h[bi, idx], gate, w_gate, w_up, w_down)
    return x.at[bi, idx].add(ys)


FFN_BATCH_ROWS = 2
FFN_TF = 512


def _expert_ffn_kernel(x_ref, g_ref, wg_ref, wu_ref, wd_ref, o_ref):
    f = pl.program_id(2)
    bb, _, cap, d = x_ref.shape
    x = x_ref[...].reshape(bb * cap, d)
    bf16 = jnp.bfloat16
    a = jnp.dot(x, wg_ref[0].astype(bf16), preferred_element_type=jnp.float32)
    u = jnp.dot(x, wu_ref[0].astype(bf16), preferred_element_type=jnp.float32)
    hid = (a * jax.nn.sigmoid(a) * u).astype(bf16)
    y = jnp.dot(hid, wd_ref[0].astype(bf16), preferred_element_type=jnp.float32).reshape(bb, 1, cap, d)

    @pl.when(f == 0)
    def _():
        o_ref[...] = y

    @pl.when(f > 0)
    def _():
        o_ref[...] += y

    @pl.when(f == pl.num_programs(2) - 1)
    def _():
        o_ref[...] = o_ref[...] * g_ref[...]


def expert_ffn(xs, gate, w_gate, w_up, w_down):
    B, E, cap, D = xs.shape
    F = w_gate.shape[-1]
    bb = min(FFN_BATCH_ROWS, B)
    tf = min(FFN_TF, F)
    return pl.pallas_call(
        _expert_ffn_kernel,
        grid=(E, B // bb, F // tf),
        in_specs=[pl.BlockSpec((bb, 1, cap, D), lambda e, b, f: (b, e, 0, 0)),
                  pl.BlockSpec((bb, 1, cap, 1), lambda e, b, f: (b, e, 0, 0)),
                  pl.BlockSpec((1, D, tf), lambda e, b, f: (e, 0, f)),
                  pl.BlockSpec((1, D, tf), lambda e, b, f: (e, 0, f)),
                  pl.BlockSpec((1, tf, D), lambda e, b, f: (e, f, 0))],
        out_specs=pl.BlockSpec((bb, 1, cap, D), lambda e, b, f: (b, e, 0, 0)),
        out_shape=jax.ShapeDtypeStruct((B, E, cap, D), jnp.float32),
        compiler_params=pltpu.CompilerParams(
            dimension_semantics=("parallel", "parallel", "arbitrary"),
            vmem_limit_bytes=56 * 1024 * 1024),
        name="expert_ffn",
    )(xs.astype(jnp.bfloat16), gate[..., None].astype(jnp.float32), w_gate, w_up, w_down)


PROJ_TM = 1024
PROJ_TN_CHOICES = (1024, 768, 512, 384, 256, 128)


def _proj_kernel(*refs, normed, with_residual):
    refs = list(refs)
    x = refs.pop(0)[...]
    if normed:
        g = refs.pop(0)[...]
        x = x * lax.rsqrt(jnp.mean(x * x, axis=-1, keepdims=True) + EPS) * g
    w = refs.pop(0)[...]
    y = jnp.dot(x.astype(jnp.bfloat16), w.astype(jnp.bfloat16), preferred_element_type=jnp.float32)
    if with_residual:
        y = y + refs.pop(0)[...]
    refs.pop(0)[...] = y


def fused_matmul(x, w, gain=None, residual=None):
    lead, K = x.shape[:-1], x.shape[-1]
    N = w.shape[-1]
    M = math.prod(lead)
    tm = min(PROJ_TM, M)
    tn = next((t for t in PROJ_TN_CHOICES if N % t == 0), N)
    operands = [x.reshape(M, K)]
    in_specs = [pl.BlockSpec((tm, K), lambda n, m: (m, 0))]
    if gain is not None:
        operands.append(gain.reshape(1, K).astype(jnp.float32))
        in_specs.append(pl.BlockSpec((1, K), lambda n, m: (0, 0)))
    operands.append(w)
    in_specs.append(pl.BlockSpec((K, tn), lambda n, m: (0, n)))
    if residual is not None:
        operands.append(residual.reshape(M, N))
        in_specs.append(pl.BlockSpec((tm, tn), lambda n, m: (m, n)))
    out = pl.pallas_call(
        functools.partial(_proj_kernel, normed=gain is not None, with_residual=residual is not None),
        grid=(N // tn, M // tm),
        in_specs=in_specs,
        out_specs=pl.BlockSpec((tm, tn), lambda n, m: (m, n)),
        out_shape=jax.ShapeDtypeStruct((M, N), jnp.float32),
        compiler_params=pltpu.CompilerParams(
            dimension_semantics=("parallel", "parallel"),
            vmem_limit_bytes=48 * 1024 * 1024),
        name="fused_matmul",
    )(*operands)
    return out.reshape(*lead, N)


OUT_TM = 512


def _gated_out_kernel(of_ref, ob_ref, z_ref, g_ref, w_ref, res_ref, o_ref):
    o = of_ref[...] + ob_ref[...]
    z = z_ref[...]
    g = g_ref[...]
    heads = []
    for h in range(o.shape[-1] // LANES):
        oh = o[:, h * LANES:(h + 1) * LANES]
        heads.append(oh * lax.rsqrt(jnp.mean(oh * oh, axis=-1, keepdims=True) + EPS) * g)
    y = jnp.concatenate(heads, axis=-1) * (z * jax.nn.sigmoid(z))
    o_ref[...] = res_ref[...] + jnp.dot(y.astype(jnp.bfloat16), w_ref[...].astype(jnp.bfloat16),
                                        preferred_element_type=jnp.float32)


def gated_out_proj(o_fwd, o_bwd, z, norm_g, w_out, residual):
    lead, K = o_fwd.shape[:-1], o_fwd.shape[-1]
    N = w_out.shape[-1]
    M = math.prod(lead)
    tm = min(OUT_TM, M)
    row = lambda width: pl.BlockSpec((tm, width), lambda m: (m, 0))
    out = pl.pallas_call(
        _gated_out_kernel,
        grid=(M // tm,),
        in_specs=[row(K), row(K), row(K),
                  pl.BlockSpec((1, LANES), lambda m: (0, 0)),
                  pl.BlockSpec((K, N), lambda m: (0, 0)),
                  row(N)],
        out_specs=row(N),
        out_shape=jax.ShapeDtypeStruct((M, N), jnp.float32),
        compiler_params=pltpu.CompilerParams(
            dimension_semantics=("parallel",),
            vmem_limit_bytes=48 * 1024 * 1024),
        name="gated_out_proj",
    )(o_fwd.reshape(M, K), o_bwd.reshape(M, K), z.reshape(M, K),
      norm_g.reshape(1, LANES).astype(jnp.float32), w_out, residual.reshape(M, N))
    return out.reshape(*lead, N)


def _final_norm_kernel(x_ref, g_ref, o_ref):
    x = x_ref[...]
    y = x * lax.rsqrt(jnp.mean(x * x, axis=-1, keepdims=True) + EPS)
    o_ref[...] = y * g_ref[...]


def final_norm(x, g):
    B, S, D = x.shape
    x2 = x.reshape(B * S, D)
    tm = 1024
    out = pl.pallas_call(
        _final_norm_kernel,
        grid=(B * S // tm,),
        in_specs=[pl.BlockSpec((tm, D), lambda i: (i, 0)),
                  pl.BlockSpec((1, D), lambda i: (0, 0))],
        out_specs=pl.BlockSpec((tm, D), lambda i: (i, 0)),
        out_shape=jax.ShapeDtypeStruct((B * S, D), x.dtype),
        name="final_norm",
    )(x2, g.reshape(1, D))
    return out.reshape(B, S, D)


def kernel(x, positions, norm_mix, norm_ffn, norm_final,
           gdn_w_in, gdn_conv, gdn_A_log, gdn_dt_bias, gdn_norm, gdn_w_out,
           hgrn_w_in, hgrn_lb, hgrn_norm, hgrn_w_out,
           sc_w_in, sc_conv, sc_w_out,
           mla_w_in, mla_q_norm, mla_w_uq, mla_kv_norm, mla_w_ukv, mla_w_o,
           moe_router, moe_w_gate, moe_w_up, moe_w_down):
    for i in range(DEPTH):
        m, j = i % N_MIXERS, i // N_MIXERS
        g = norm_mix[i]
        if m == 0:
            x = gdn_mixer(x, g, gdn_w_in[j], gdn_conv[j], gdn_A_log[j], gdn_dt_bias[j], gdn_norm[j], gdn_w_out[j])
        elif m == 1:
            x = hgrn2_mixer(x, g, i, hgrn_w_in[j], hgrn_lb, hgrn_norm[j], hgrn_w_out[j])
        elif m == 2:
            x = shortconv_mixer(x, g, sc_w_in[j], sc_conv[j], sc_w_out[j])
        else:
            x = mla_mixer(x, g, positions, mla_w_in[j], mla_q_norm[j], mla_w_uq[j],
                          mla_kv_norm[j], mla_w_ukv[j], mla_w_o[j])
        x = expert_choice_ffn(x, norm_ffn[i], moe_router[i], moe_w_gate[i], moe_w_up[i], moe_w_down[i])
    return final_norm(x, norm_final)
```

```python
import functools
import math

import jax
import jax.numpy as jnp
from jax import lax
from jax.experimental import pallas as pl
from jax.experimental.pallas import tpu as pltpu

D_MODEL = 1024
BATCH = 4
SEQ = 8192
DEPTH = 4
N_MIXERS = 4
EPS = 1e-6
GDN_HEADS = 8
GDN_DK = 128
GDN_DV = 128
GDN_CONV = 5
HGRN_EXPAND = 128
HGRN_HEADS = D_MODEL // HGRN_EXPAND
HGRN_DF = HGRN_EXPAND
HGRN_DI = D_MODEL // HGRN_HEADS
HGRN_F = HGRN_HEADS * HGRN_DF
SC_WIDTH = 3
MLA_HEADS = 16
MLA_NOPE = 64
MLA_ROPE = 32
MLA_V = 64
MLA_Q_LORA = 384
MLA_KV_LORA = 256
ROPE_THETA = 10000.0
N_EXPERTS = 16
D_EXPERT = 2048
CAPACITY_FACTOR = 2


def rms_norm(x, g):
    xf = x.astype(jnp.float32)
    y = xf * lax.rsqrt(jnp.mean(xf * xf, axis=-1, keepdims=True) + EPS)
    return (y * g.astype(jnp.float32)).astype(x.dtype)


def l2_normalize(x):
    xf = x.astype(jnp.float32)
    return xf * lax.rsqrt(jnp.sum(xf * xf, axis=-1, keepdims=True) + EPS)


def centred_depthwise_conv(x, w):
    K, C = w.shape
    return lax.conv_general_dilated(
        x, w[:, None, :], window_strides=(1,), padding=[(K // 2, K // 2)],
        dimension_numbers=('NWC', 'WIO', 'NWC'), feature_group_count=C)


LANES = 128
SUBLANES = 8
CONV_TILE = 256
CONV_CHANNELS = 1024


def _row_conv(prev, x, nxt, w):
    T, K = x.shape[0], w.shape[0]
    ext = jnp.concatenate([prev, x, nxt], axis=0)
    acc = None
    for j in range(K):
        start = SUBLANES - K // 2 + j
        term = ext[start:start + T] * w[j:j + 1, :]
        acc = term if acc is None else acc + term
    return acc


def _halo_specs(S, T, ct):
    r = T // SUBLANES
    return [pl.BlockSpec((1, SUBLANES, ct), lambda b, t, c: (b, jnp.maximum(t * r - 1, 0), c)),
            pl.BlockSpec((1, T, ct), lambda b, t, c: (b, t, c)),
            pl.BlockSpec((1, SUBLANES, ct), lambda b, t, c: (b, jnp.minimum((t + 1) * r, S // SUBLANES - 1), c))]


def _gdn_prep_kernel(xp_ref, x_ref, xn_ref, w_ref, o_ref, *, n_norm_chunks):
    t, c = pl.program_id(1), pl.program_id(2)
    prev = jnp.where(t == 0, 0.0, xp_ref[0])
    nxt = jnp.where(t == pl.num_programs(1) - 1, 0.0, xn_ref[0])
    y = _row_conv(prev, x_ref[0], nxt, w_ref[...])
    y = y * jax.nn.sigmoid(y)
    heads = []
    for h in range(y.shape[-1] // LANES):
        yh = y[:, h * LANES:(h + 1) * LANES]
        heads.append(yh * lax.rsqrt(jnp.sum(yh * yh, axis=-1, keepdims=True) + EPS))
    o_ref[0] = jnp.where(c < n_norm_chunks, jnp.concatenate(heads, axis=-1), y)


def gdn_prep(qkv_raw, conv_w, n_norm):
    B, S, C = qkv_raw.shape
    T, ct = min(CONV_TILE, S), min(CONV_CHANNELS, C)
    return pl.pallas_call(
        functools.partial(_gdn_prep_kernel, n_norm_chunks=n_norm // ct),
        grid=(B, S // T, C // ct),
        in_specs=_halo_specs(S, T, ct) + [pl.BlockSpec((conv_w.shape[0], ct), lambda b, t, c: (0, c))],
        out_specs=pl.BlockSpec((1, T, ct), lambda b, t, c: (b, t, c)),
        out_shape=jax.ShapeDtypeStruct((B, S, C), jnp.float32),
        compiler_params=pltpu.CompilerParams(dimension_semantics=("parallel", "parallel", "parallel")),
        name="gdn_prep",
    )(qkv_raw, qkv_raw, qkv_raw, conv_w)


def _shortconv_gate_kernel(gb_ref, cp_ref, c_ref, cn_ref, up_ref, u_ref, un_ref, w_ref, o_ref):
    t = pl.program_id(1)
    first, last = t == 0, t == pl.num_programs(1) - 1
    prev = jnp.where(first, 0.0, cp_ref[0] * up_ref[0])
    nxt = jnp.where(last, 0.0, cn_ref[0] * un_ref[0])
    o_ref[0] = gb_ref[0] * _row_conv(prev, c_ref[0] * u_ref[0], nxt, w_ref[...])


def shortconv_gate(gb, gc, u, conv_w):
    B, S, C = gb.shape
    T, ct = min(CONV_TILE, S), min(CONV_CHANNELS, C)
    halo = _halo_specs(S, T, ct)
    return pl.pallas_call(
        _shortconv_gate_kernel,
        grid=(B, S // T, C // ct),
        in_specs=[halo[1]] + halo + halo + [pl.BlockSpec((conv_w.shape[0], ct), lambda b, t, c: (0, c))],
        out_specs=pl.BlockSpec((1, T, ct), lambda b, t, c: (b, t, c)),
        out_shape=jax.ShapeDtypeStruct((B, S, C), jnp.float32),
        compiler_params=pltpu.CompilerParams(dimension_semantics=("parallel", "parallel", "parallel")),
        name="shortconv_gate",
    )(gb, gc, gc, gc, u, u, u, conv_w)


GDN_BLOCK = 16
GDN_TILE = 256
GDN_GROUP = 4


def _gdn_kernel(*refs):
    C, G, H = GDN_BLOCK, GDN_GROUP, GDN_HEADS
    per_dir = 3 * G + 1
    in_refs, out_refs = refs[:2 * per_dir], refs[2 * per_dir:2 * per_dir + 2]
    scratch = refs[2 * per_dir + 2:]
    state_ref, us_ref, ws_ref, gc_ref = scratch[0], scratch[13], scratch[14], scratch[15]
    u_ref, w_ref, qe_ref, kd_ref, a_ref, dec_ref = scratch[1:7]
    u_rd, w_rd, qe_rd, kd_rd, a_rd, dec_rd = scratch[7:13]
    handoff = tuple(zip(scratch[1:7], scratch[7:13]))
    T = in_refs[0].shape[1]
    nb = T // C
    bf16 = jnp.bfloat16
    scale = GDN_DK ** -0.5
    hg = pl.program_id(1)

    @pl.when(pl.program_id(2) == 0)
    def _():
        state_ref[...] = jnp.zeros_like(state_ref)
        for _, dst in handoff:
            dst[...] = jnp.zeros_like(dst)

    slab = lambda r: pl.ds(r, nb, stride=C)
    ones = jnp.ones((LANES, LANES), bf16)
    lane = lax.broadcasted_iota(jnp.int32, (nb, LANES), 1)
    rowsum = lambda x: jnp.broadcast_to(jnp.sum(x, axis=-1, keepdims=True), (nb, LANES))

    def pick(x, idx):
        return rowsum(jnp.where(lane == idx, x, 0.0))

    orders = (list(range(C)), list(range(C - 1, -1, -1)))
    chains = [(d, g) for d in range(2) for g in range(G)]
    sc_refs = [in_refs[d * per_dir + 3 * G] for d in range(2)]
    qkv_refs = {(d, g): [in_refs[d * per_dir + j * G + g] for j in range(3)] for d, g in chains}

    for d, g in chains:
        run = None
        for r in orders[d]:
            g_r = pick(sc_refs[d][0, slab(r), :], d * H + hg * G + g)
            run = g_r if run is None else run + g_r
            gc_ref[d, g, r] = run
        dec_ref[d, g] = jnp.exp(run)

    def block_phase(n):
        dots = {}
        for d, g in chains:
            order = orders[d]
            q_ref, k_ref, v_ref = qkv_refs[d, g]
            q_r = q_ref[0, slab(order[n]), :] * scale
            k_r = k_ref[0, slab(order[n]), :]
            k_prev = [k_ref[0, slab(r2), :] for r2 in order[:n]]
            prods = [q_r * k_r] + [q_r * k_2 for k_2 in k_prev] + [k_r * k_2 for k_2 in k_prev]
            dots[d, g] = jnp.dot(jnp.concatenate(prods, axis=0).astype(bf16), ones,
                                 preferred_element_type=jnp.float32)
        for d, g in chains:
            order = orders[d]
            r = order[n]
            q_ref, k_ref, v_ref = qkv_refs[d, g]
            gc, us, ws = gc_ref.at[d, g], us_ref.at[d, g], ws_ref.at[d, g]
            dt = dots[d, g]
            q_r = q_ref[0, slab(r), :] * scale
            k_r = k_ref[0, slab(r), :]
            beta_r = pick(sc_refs[d][0, slab(r), :], 2 * H + d * H + hg * G + g)
            gc_r = gc[r]
            eg_r = jnp.exp(gc_r)
            u_r = beta_r * v_ref[0, slab(r), :]
            w_r = beta_r * eg_r * k_r
            a_row = jnp.where(lane == r, dt[:nb], 0.0)
            for m, r2 in enumerate(order[:n]):
                decay = jnp.exp(gc_r - gc[r2])
                l_rr = beta_r * decay * dt[(1 + n + m) * nb:(2 + n + m) * nb]
                u_r = u_r - l_rr * us[r2]
                w_r = w_r - l_rr * ws[r2]
                a_row = jnp.where(lane == r2, decay * dt[(1 + m) * nb:(2 + m) * nb], a_row)
            us[r] = u_r
            ws[r] = w_r
            u_ref[d, g, slab(r), :] = u_r
            w_ref[d, g, slab(r), :] = w_r
            a_ref[d, g, slab(r), :] = a_row
            qe_ref[d, g, slab(r), :] = q_r * eg_r
            kd_ref[d, g, slab(r), :] = k_r * jnp.exp(gc[order[-1]] - gc_r)

    def state_phase(i):
        blks = [i, nb - 1 - i]
        rows = [pl.ds(blk * C, C) for blk in blks]
        r1s = {}
        for d, g in chains:
            wq = jnp.concatenate([w_rd[d, g, rows[d], :], qe_rd[d, g, rows[d], :]], axis=0).astype(bf16)
            r1s[d, g] = jnp.dot(wq, state_ref[d, g].astype(bf16), preferred_element_type=jnp.float32)
        v_news = {c: (u_rd[c[0], c[1], rows[c[0]], :] - r1s[c][:C]).astype(bf16) for c in chains}
        intra, upd = {}, {}
        for d, g in chains:
            a_blk = a_rd[d, g, rows[d], :][:, :C].astype(bf16)
            intra[d, g] = jnp.dot(a_blk, v_news[d, g], preferred_element_type=jnp.float32)
            upd[d, g] = lax.dot_general(kd_rd[d, g, rows[d], :].astype(bf16), v_news[d, g],
                                        (((0,), (0,)), ((), ())), preferred_element_type=jnp.float32)
        for d, g in chains:
            out_refs[d][0, rows[d], g * LANES:(g + 1) * LANES] = r1s[d, g][C:] + intra[d, g]
            state_ref[d, g] = state_ref[d, g] * dec_rd[d, g, pl.ds(blks[d], 1), :] + upd[d, g]

    for n in range(max(C, nb)):
        if n < nb:
            state_phase(n)
        if n < C:
            block_phase(n)
    for src, dst in handoff:
        dst[...] = src[...]


def gdn_bidir(qkv, scal):
    B, S, _ = qkv.shape
    H, G = GDN_HEADS, GDN_GROUP
    T = min(GDN_TILE, S)
    nT, nG = S // T, H // G
    nb = T // GDN_BLOCK
    fwd_in = lambda t: jnp.minimum(t, nT - 1)
    bwd_in = lambda t: jnp.maximum(nT - 1 - t, 0)
    fwd_out = lambda t: jnp.maximum(t - 1, 0)
    bwd_out = lambda t: jnp.minimum(nT - t, nT - 1)
    blk = lambda tile, off: pl.BlockSpec((1, T, LANES), lambda b, hg, t: (b, tile(t), off + hg * G))
    in_specs, operands = [], []
    for tile in (fwd_in, bwd_in):
        for j in range(3):
            for g in range(G):
                in_specs.append(blk(tile, j * H + g))
                operands.append(qkv)
        in_specs.append(pl.BlockSpec((1, T, LANES), lambda b, hg, t, tile=tile: (b, tile(t), 0)))
        operands.append(scal)
    out_specs = [pl.BlockSpec((1, T, G * LANES), lambda b, hg, t: (b, fwd_out(t), hg)),
                 pl.BlockSpec((1, T, G * LANES), lambda b, hg, t: (b, bwd_out(t), hg))]
    out_sd = jax.ShapeDtypeStruct((B, S, H * LANES), jnp.float32)
    tile_buf = pltpu.VMEM((2, G, T, LANES), jnp.float32)
    dec_buf = pltpu.VMEM((2, G, nb, LANES), jnp.float32)
    slabs = pltpu.VMEM((2, G, GDN_BLOCK, nb, LANES), jnp.float32)
    return pl.pallas_call(
        _gdn_kernel,
        grid=(B, nG, nT + 1),
        in_specs=in_specs,
        out_specs=out_specs,
        out_shape=[out_sd, out_sd],
        scratch_shapes=[pltpu.VMEM((2, G, LANES, LANES), jnp.float32),
                        tile_buf, tile_buf, tile_buf, tile_buf, tile_buf, dec_buf,
                        tile_buf, tile_buf, tile_buf, tile_buf, tile_buf, dec_buf,
                        slabs, slabs, slabs],
        compiler_params=pltpu.CompilerParams(
            dimension_semantics=("parallel", "parallel", "arbitrary"),
            vmem_limit_bytes=48 * 1024 * 1024),
        name="gdn_bidir",
    )(*operands)


def gdn_mixer(x, gain, w_in, conv_w, A_log, dt_bias, norm_g, w_out):
    B, S, _ = x.shape
    H, dk, dv = GDN_HEADS, GDN_DK, GDN_DV
    n_qkv, n_z = 2 * H * dk + H * dv, H * dv
    qkv = fused_matmul(x, w_in[:, :n_qkv], gain=gain)
    z = fused_matmul(x, w_in[:, n_qkv:n_qkv + n_z], gain=gain)
    ab = fused_matmul(x, w_in[:, n_qkv + n_z:], gain=gain)
    a, b = ab[..., :2 * H], ab[..., 2 * H:]
    qkv = gdn_prep(qkv, conv_w, n_norm=2 * H * dk)
    a = a.astype(jnp.float32).reshape(B, S, 2, H)
    b = b.astype(jnp.float32).reshape(B, S, 2, H)
    g = -jnp.exp(A_log.astype(jnp.float32)) * jax.nn.softplus(a + dt_bias.astype(jnp.float32))
    beta = jax.nn.sigmoid(b)
    scal = jnp.concatenate([g.reshape(B, S, 2 * H), beta.reshape(B, S, 2 * H),
                            jnp.zeros((B, S, LANES - 4 * H), jnp.float32)], axis=-1)
    o_fwd, o_bwd = gdn_bidir(qkv, scal)
    return gated_out_proj(o_fwd, o_bwd, z, norm_g, w_out, x)


HGRN_BLOCK = 16
HGRN_TILE = 256
HGRN_GROUP = 4


def _hgrn_kernel(*refs):
    C, G = HGRN_BLOCK, HGRN_GROUP
    per_dir = 3 * G + 1
    in_refs, out_refs = refs[:2 * per_dir], refs[2 * per_dir:2 * per_dir + 2]
    scratch = refs[2 * per_dir + 2:]
    state_ref, b_ref, kk_ref = scratch[0], scratch[11], scratch[12]
    qe_ref, ke_ref, od_ref, vv_ref, dec_ref = scratch[1:6]
    qe_rd, ke_rd, od_rd, vv_rd, dec_rd = scratch[6:11]
    handoff = tuple(zip(scratch[1:6], scratch[6:11]))
    T = in_refs[0].shape[1]
    nb = T // C
    bf16 = jnp.bfloat16
    chains = [(d, g) for d in range(2) for g in range(G)]
    qvl_refs = {(d, g): [in_refs[d * per_dir + j * G + g] for j in range(3)] for d, g in chains}
    lb_refs = [in_refs[d * per_dir + 3 * G] for d in range(2)]

    @pl.when(pl.program_id(2) == 0)
    def _():
        state_ref[...] = jnp.zeros_like(state_ref)
        for _, dst in handoff:
            dst[...] = jnp.zeros_like(dst)

    slab = lambda r: pl.ds(r, nb, stride=C)
    ones = jnp.ones((LANES, LANES), bf16)

    def log_f(x, log_lb, log_1m_lb):
        log_sig = jnp.minimum(x, 0.0) - jnp.log(1.0 + jnp.exp(-jnp.abs(x)))
        a, b = log_lb, log_1m_lb + log_sig
        return jnp.maximum(a, b) + jnp.log(1.0 + jnp.exp(-jnp.abs(a - b)))

    for d, g in chains:
        order = list(range(C)) if d == 0 else list(range(C - 1, -1, -1))
        l_ref = qvl_refs[d, g][2]
        log_lb = lb_refs[d][0:1, g * LANES:(g + 1) * LANES]
        log_1m_lb = lb_refs[d][1:2, g * LANES:(g + 1) * LANES]
        run = None
        for r in order:
            lf = log_f(l_ref[0, slab(r), :], log_lb, log_1m_lb)
            run = lf if run is None else run + lf
            b_ref[d, g, r] = run
            kk_ref[d, g, r] = 1.0 - jnp.exp(lf)
        dec_ref[d, g] = jnp.exp(run)

    def block_phase(r):
        scores, partners = {}, {}
        for d, g in chains:
            q_ref, v_ref, _ = qvl_refs[d, g]
            bs, kk = b_ref.at[d, g], kk_ref.at[d, g]
            q_r = q_ref[0, slab(r), :]
            b_r = bs[r]
            btot = bs[C - 1 if d == 0 else 0]
            qe_ref[d, g, slab(r), :] = q_r * jnp.exp(b_r)
            ke_ref[d, g, slab(r), :] = kk[r] * jnp.exp(btot - b_r)
            vv_ref[d, g, slab(r), :] = v_ref[0, slab(r), :]
            partners[d, g] = list(range(r + 1) if d == 0 else range(r, C))
            terms = [q_r * kk[r2] * jnp.exp(b_r - bs[r2]) for r2 in partners[d, g]]
            scores[d, g] = jnp.dot(jnp.concatenate(terms, axis=0).astype(bf16), ones,
                                   preferred_element_type=jnp.float32)
        for d, g in chains:
            v_ref = qvl_refs[d, g][1]
            acc = None
            for m, r2 in enumerate(partners[d, g]):
                term = scores[d, g][m * nb:(m + 1) * nb] * v_ref[0, slab(r2), :]
                acc = term if acc is None else acc + term
            od_ref[d, g, slab(r), :] = acc

    def state_phase(i):
        blks = [i, nb - 1 - i]
        rows = [pl.ds(blk * C, C) for blk in blks]
        inter, upd = {}, {}
        for d, g in chains:
            qe = qe_rd[d, g, rows[d], :].astype(bf16)
            inter[d, g] = lax.dot_general(qe, state_ref[d, g].astype(bf16), (((1,), (1,)), ((), ())),
                                          preferred_element_type=jnp.float32)
            upd[d, g] = lax.dot_general(vv_rd[d, g, rows[d], :].astype(bf16), ke_rd[d, g, rows[d], :].astype(bf16),
                                        (((0,), (0,)), ((), ())), preferred_element_type=jnp.float32)
        for d, g in chains:
            out_refs[d][0, rows[d], g * LANES:(g + 1) * LANES] = od_rd[d, g, rows[d], :] + inter[d, g]
            state_ref[d, g] = state_ref[d, g] * dec_rd[d, g, pl.ds(blks[d], 1), :] + upd[d, g]

    for n in range(max(C, nb)):
        if n < nb:
            state_phase(n)
        if n < C:
            block_phase(n)
    for src, dst in handoff:
        dst[...] = src[...]


def hgrn2_bidir(proj, log_lb):
    B, S, _ = proj.shape
    H = HGRN_HEADS
    T = min(HGRN_TILE, S)
    G = HGRN_GROUP
    nT, nG = S // T, H // G
    nb = T // HGRN_BLOCK
    fwd_in = lambda t: jnp.minimum(t, nT - 1)
    bwd_in = lambda t: jnp.maximum(nT - 1 - t, 0)
    fwd_out = lambda t: jnp.maximum(t - 1, 0)
    bwd_out = lambda t: jnp.minimum(nT - t, nT - 1)
    blk = lambda tile, off: pl.BlockSpec((1, T, LANES), lambda b, hg, t: (b, tile(t), off + hg * G))
    in_specs, operands = [], []
    for d, tile in enumerate((fwd_in, bwd_in)):
        for off in (0, H, (2 + d) * H):
            for g in range(G):
                in_specs.append(blk(tile, off + g))
                operands.append(proj)
        in_specs.append(pl.BlockSpec((None, 2, G * LANES), lambda b, hg, t, d=d: (d, 0, hg)))
        operands.append(log_lb)
    out_specs = [pl.BlockSpec((1, T, G * LANES), lambda b, hg, t: (b, fwd_out(t), hg)),
                 pl.BlockSpec((1, T, G * LANES), lambda b, hg, t: (b, bwd_out(t), hg))]
    out_sd = jax.ShapeDtypeStruct((B, S, H * LANES), jnp.float32)
    tile_buf = pltpu.VMEM((2, G, T, LANES), jnp.float32)
    dec_buf = pltpu.VMEM((2, G, nb, LANES), jnp.float32)
    slabs = pltpu.VMEM((2, G, HGRN_BLOCK, nb, LANES), jnp.float32)
    return pl.pallas_call(
        _hgrn_kernel,
        grid=(B, nG, nT + 1),
        in_specs=in_specs,
        out_specs=out_specs,
        out_shape=[out_sd, out_sd],
        scratch_shapes=[pltpu.VMEM((2, G, LANES, LANES), jnp.float32),
                        tile_buf, tile_buf, tile_buf, tile_buf, dec_buf,
                        tile_buf, tile_buf, tile_buf, tile_buf, dec_buf,
                        slabs, slabs],
        compiler_params=pltpu.CompilerParams(
            dimension_semantics=("parallel", "parallel", "arbitrary"),
            vmem_limit_bytes=48 * 1024 * 1024),
        name="hgrn2_bidir",
    )(*operands)


def hgrn2_mixer(x, gain, layer, w_in, lb_table, norm_g, w_out):
    B, S, _ = x.shape
    H, dF, dI = HGRN_HEADS, HGRN_DF, HGRN_DI
    n_qv = HGRN_F + H * dI
    proj = fused_matmul(x, jnp.concatenate([w_in[:, :n_qv], w_in[:, n_qv + H * dI:]], axis=1), gain=gain)
    gate = fused_matmul(x, w_in[:, n_qv:n_qv + H * dI], gain=gain)
    lb_w = jax.nn.softmax(lb_table.astype(jnp.float32), axis=1)
    lb = (jnp.cumsum(lb_w, axis=1) - lb_w[:, :1])[:, layer]
    log_lb = jnp.stack([jnp.log(lb), jnp.log1p(-lb)], axis=1)
    o_fwd, o_bwd = hgrn2_bidir(proj, log_lb)
    return gated_out_proj(o_fwd, o_bwd, gate, norm_g, w_out, x)


def shortconv_mixer(x, gain, w_in, conv_w, w_out):
    D = x.shape[-1]
    gb, gc, u = (fused_matmul(x, w_in[:, j * D:(j + 1) * D], gain=gain) for j in range(3))
    return fused_matmul(shortconv_gate(gb, gc, u, conv_w), w_out, residual=x)


def rope(x, pos):
    half = x.shape[-1] // 2
    inv = ROPE_THETA ** (-jnp.arange(half, dtype=jnp.float32) / half)
    ang = pos.astype(jnp.float32)[:, :, None, None] * inv
    cos, sin = jnp.cos(ang), jnp.sin(ang)
    xf = x.astype(jnp.float32)
    x1, x2 = xf[..., :half], xf[..., half:]
    return jnp.concatenate([x1 * cos - x2 * sin, x1 * sin + x2 * cos], axis=-1).astype(x.dtype)


MLA_QK_PAD = 128
MLA_TQ = 512
MLA_TK = 2048


def _mla_attn_kernel(q_ref, k_ref, v_ref, o_ref, *, tk):
    tq = q_ref.shape[1]
    nk = k_ref.shape[1] // tk
    head_lanes = [slice(hh * MLA_QK_PAD, (hh + 1) * MLA_QK_PAD) for hh in range(2)]
    qs = [q_ref[0, :, lanes] for lanes in head_lanes]

    def body(j, carry):
        rows = pl.ds(pl.multiple_of(j * tk, tk), tk)
        v = v_ref[0, rows, :]
        new = []
        for hh in range(2):
            m, l, acc = carry[hh]
            k = k_ref[0, rows, head_lanes[hh]]
            s = lax.dot_general(qs[hh], k, (((1,), (1,)), ((), ())), preferred_element_type=jnp.float32)
            m_new = jnp.maximum(m, jnp.max(s, axis=-1, keepdims=True))
            p = jnp.exp(s - m_new)
            alpha = jnp.exp(m - m_new)
            l = alpha * l + jnp.sum(p, axis=-1, keepdims=True)
            acc = alpha * acc + jnp.dot(p.astype(jnp.bfloat16), v, preferred_element_type=jnp.float32)
            new.append((m_new, l, acc))
        return tuple(new)

    init = (jnp.full((tq, 1), -jnp.inf, jnp.float32), jnp.zeros((tq, 1), jnp.float32),
            jnp.zeros((tq, 2 * MLA_V), jnp.float32))
    (_, l0, acc0), (_, l1, acc1) = lax.fori_loop(0, nk, body, (init, init))
    lane = lax.broadcasted_iota(jnp.int32, (tq, 2 * MLA_V), 1)
    o_ref[0] = jnp.where(lane < MLA_V, acc0 / l0, acc1 / l1).astype(o_ref.dtype)


def mla_attention(qf, kf, vf):
    B, S, _ = qf.shape
    H = MLA_HEADS
    tq, tk = min(MLA_TQ, S), min(MLA_TK, S)
    return pl.pallas_call(
        functools.partial(_mla_attn_kernel, tk=tk),
        grid=(B, H // 2, S // tq),
        in_specs=[pl.BlockSpec((1, tq, 2 * MLA_QK_PAD), lambda b, h, i: (b, i, h)),
                  pl.BlockSpec((1, S, 2 * MLA_QK_PAD), lambda b, h, i: (b, 0, h)),
                  pl.BlockSpec((1, S, 2 * MLA_V), lambda b, h, i: (b, 0, h))],
        out_specs=pl.BlockSpec((1, tq, 2 * MLA_V), lambda b, h, i: (b, i, h)),
        out_shape=jax.ShapeDtypeStruct((B, S, H * MLA_V), jnp.float32),
        compiler_params=pltpu.CompilerParams(
            dimension_semantics=("parallel", "parallel", "arbitrary"),
            vmem_limit_bytes=48 * 1024 * 1024),
        name="mla_attention",
    )(qf, kf, vf)


PACK_TM = 512


def _pack_q_kernel(q_ref, c_ref, sp_ref, sm_ref, o_ref, *, scale, half):
    q = q_ref[...]
    n_heads = q.shape[-1] // LANES
    wide = lambda ref: jnp.concatenate([ref[...]] * n_heads, axis=-1)
    up = pltpu.roll(q, q.shape[-1] - half, axis=1)
    down = pltpu.roll(q, half, axis=1)
    o_ref[...] = ((q * wide(c_ref) + up * wide(sp_ref) + down * wide(sm_ref)) * scale).astype(o_ref.dtype)


def _pack_k_kernel(k_ref, kr_ref, o_ref):
    n_heads = k_ref.shape[-1] // LANES
    o_ref[...] = (k_ref[...] + jnp.concatenate([kr_ref[...]] * n_heads, axis=-1)).astype(o_ref.dtype)


def _pack_call(kernel_fn, wide, narrow, name):
    M, N = wide.shape
    tm = min(PACK_TM, M)
    return pl.pallas_call(
        kernel_fn,
        grid=(M // tm,),
        in_specs=[pl.BlockSpec((tm, N), lambda m: (m, 0))] + [pl.BlockSpec((tm, LANES), lambda m: (m, 0))] * len(narrow),
        out_specs=pl.BlockSpec((tm, N), lambda m: (m, 0)),
        out_shape=jax.ShapeDtypeStruct((M, N), jnp.bfloat16),
        compiler_params=pltpu.CompilerParams(dimension_semantics=("parallel",)),
        name=name,
    )(wide, *narrow)


def mla_mixer(x, gain, pos, w_in, q_norm, w_uq, kv_norm, w_ukv, w_o):
    B, S, _ = x.shape
    H, M = MLA_HEADS, B * S
    n_q, n_kv, half = MLA_Q_LORA, MLA_KV_LORA, MLA_ROPE // 2
    pad = MLA_QK_PAD - MLA_NOPE - MLA_ROPE
    f32 = jnp.float32
    cq = fused_matmul(x, w_in[:, :n_q], gain=gain)
    ckv = fused_matmul(x, w_in[:, n_q:n_q + n_kv], gain=gain)
    kr = fused_matmul(x, w_in[:, n_q + n_kv:], gain=gain)
    w_q = jnp.pad(w_uq.reshape(n_q, H, MLA_NOPE + MLA_ROPE), ((0, 0), (0, 0), (0, pad))).reshape(n_q, H * MLA_QK_PAD)
    w_kv = w_ukv.reshape(n_kv, H, MLA_NOPE + MLA_V)
    w_k = jnp.pad(w_kv[..., :MLA_NOPE], ((0, 0), (0, 0), (0, MLA_QK_PAD - MLA_NOPE))).reshape(n_kv, H * MLA_QK_PAD)
    w_v = w_kv[..., MLA_NOPE:].reshape(n_kv, H * MLA_V)
    q_pad = fused_matmul(cq, w_q, gain=q_norm).reshape(M, H * MLA_QK_PAD)
    k_pad = fused_matmul(ckv, w_k, gain=kv_norm).reshape(M, H * MLA_QK_PAD)
    vf = fused_matmul(ckv, w_v, gain=kv_norm, out_dtype=jnp.bfloat16)
    inv = ROPE_THETA ** (-jnp.arange(half, dtype=f32) / half)
    ang = pos.astype(f32).reshape(M, 1) * inv
    cos, sin = jnp.cos(ang), jnp.sin(ang)
    zeros = lambda n: jnp.zeros((M, n), f32)
    c_tab = jnp.concatenate([jnp.ones((M, MLA_NOPE), f32), cos, cos, zeros(pad)], axis=-1)
    sp_tab = jnp.concatenate([zeros(MLA_NOPE), -sin, zeros(half), zeros(pad)], axis=-1)
    sm_tab = jnp.concatenate([zeros(MLA_NOPE), zeros(half), sin, zeros(pad)], axis=-1)
    scale = (MLA_NOPE + MLA_ROPE) ** -0.5
    qf = _pack_call(functools.partial(_pack_q_kernel, scale=scale, half=half), q_pad, [c_tab, sp_tab, sm_tab], "pack_q")
    kr_tile = jnp.concatenate([zeros(MLA_NOPE), rope(kr[:, :, None, :], pos).reshape(M, MLA_ROPE), zeros(pad)], axis=-1)
    kf = _pack_call(_pack_k_kernel, k_pad, [kr_tile], "pack_k")
    o = mla_attention(qf.reshape(B, S, -1), kf.reshape(B, S, -1), vf)
    return fused_matmul(o, w_o, residual=x)


ROUTE_TM = 1024


def _route_kernel(x_ref, g_ref, wr_ref, h_ref, aff_ref):
    x = x_ref[...]
    h = (x * lax.rsqrt(jnp.mean(x * x, axis=-1, keepdims=True) + EPS) * g_ref[...]).astype(jnp.bfloat16)
    h_ref[...] = h
    logits = jnp.dot(h, wr_ref[...].astype(jnp.bfloat16), preferred_element_type=jnp.float32)
    e = jnp.exp(logits - jnp.max(logits, axis=-1, keepdims=True))
    aff_ref[...] = e / jnp.sum(e, axis=-1, keepdims=True)


def route(x, gain, w_router):
    B, S, D = x.shape
    E = w_router.shape[-1]
    M = B * S
    tm = min(ROUTE_TM, M)
    h, aff = pl.pallas_call(
        _route_kernel,
        grid=(M // tm,),
        in_specs=[pl.BlockSpec((tm, D), lambda i: (i, 0)),
                  pl.BlockSpec((1, D), lambda i: (0, 0)),
                  pl.BlockSpec((D, E), lambda i: (0, 0))],
        out_specs=[pl.BlockSpec((tm, D), lambda i: (i, 0)),
                   pl.BlockSpec((tm, E), lambda i: (i, 0))],
        out_shape=[jax.ShapeDtypeStruct((M, D), jnp.bfloat16), jax.ShapeDtypeStruct((M, E), jnp.float32)],
        compiler_params=pltpu.CompilerParams(dimension_semantics=("parallel",)),
        name="route",
    )(x.reshape(M, D), gain.reshape(1, D).astype(jnp.float32), w_router)
    return h.reshape(B, S, D), aff.reshape(B, S, E)


def expert_choice_ffn(x, gain, w_router, w_gate, w_up, w_down):
    B, S, _ = x.shape
    cap = CAPACITY_FACTOR * S // N_EXPERTS
    h, aff = route(x, gain, w_router)
    gate, idx = lax.top_k(jnp.swapaxes(aff, 1, 2), cap)
    bi = jnp.arange(B)[:, None, None]
    ys = expert_ffn(h[bi, idx], gate, w_gate, w_up, w_down)
    return x.at[bi, idx].add(ys)


FFN_BATCH_ROWS = 2
FFN_TF = 512


def _expert_ffn_kernel(x_ref, g_ref, wg_ref, wu_ref, wd_ref, o_ref):
    f = pl.program_id(2)
    bb, _, cap, d = x_ref.shape
    x = x_ref[...].reshape(bb * cap, d)
    bf16 = jnp.bfloat16
    a = jnp.dot(x, wg_ref[0].astype(bf16), preferred_element_type=jnp.float32)
    u = jnp.dot(x, wu_ref[0].astype(bf16), preferred_element_type=jnp.float32)
    hid = (a * jax.nn.sigmoid(a) * u).astype(bf16)
    y = jnp.dot(hid, wd_ref[0].astype(bf16), preferred_element_type=jnp.float32).reshape(bb, 1, cap, d)

    @pl.when(f == 0)
    def _():
        o_ref[...] = y

    @pl.when(f > 0)
    def _():
        o_ref[...] += y

    @pl.when(f == pl.num_programs(2) - 1)
    def _():
        o_ref[...] = o_ref[...] * g_ref[...]


def expert_ffn(xs, gate, w_gate, w_up, w_down):
    B, E, cap, D = xs.shape
    F = w_gate.shape[-1]
    bb = min(FFN_BATCH_ROWS, B)
    tf = min(FFN_TF, F)
    return pl.pallas_call(
        _expert_ffn_kernel,
        grid=(E, B // bb, F // tf),
        in_specs=[pl.BlockSpec((bb, 1, cap, D), lambda e, b, f: (b, e, 0, 0)),
                  pl.BlockSpec((bb, 1, cap, 1), lambda e, b, f: (b, e, 0, 0)),
                  pl.BlockSpec((1, D, tf), lambda e, b, f: (e, 0, f)),
                  pl.BlockSpec((1, D, tf), lambda e, b, f: (e, 0, f)),
                  pl.BlockSpec((1, tf, D), lambda e, b, f: (e, f, 0))],
        out_specs=pl.BlockSpec((bb, 1, cap, D), lambda e, b, f: (b, e, 0, 0)),
        out_shape=jax.ShapeDtypeStruct((B, E, cap, D), jnp.float32),
        compiler_params=pltpu.CompilerParams(
            dimension_semantics=("parallel", "parallel", "arbitrary"),
            vmem_limit_bytes=56 * 1024 * 1024),
        name="expert_ffn",
    )(xs.astype(jnp.bfloat16), gate[..., None].astype(jnp.float32), w_gate, w_up, w_down)


PROJ_TM = 1024
PROJ_TN_CHOICES = (1024, 768, 512, 384, 256, 128)


def _proj_kernel(*refs, normed, with_residual):
    refs = list(refs)
    x = refs.pop(0)[...]
    if normed:
        g = refs.pop(0)[...]
        x = x * lax.rsqrt(jnp.mean(x * x, axis=-1, keepdims=True) + EPS) * g
    w = refs.pop(0)[...]
    y = jnp.dot(x.astype(jnp.bfloat16), w.astype(jnp.bfloat16), preferred_element_type=jnp.float32)
    if with_residual:
        y = y + refs.pop(0)[...]
    o_ref = refs.pop(0)
    o_ref[...] = y.astype(o_ref.dtype)


def fused_matmul(x, w, gain=None, residual=None, out_dtype=jnp.float32):
    lead, K = x.shape[:-1], x.shape[-1]
    N = w.shape[-1]
    M = math.prod(lead)
    tm = min(PROJ_TM, M)
    tn = next((t for t in PROJ_TN_CHOICES if N % t == 0), N)
    operands = [x.reshape(M, K)]
    in_specs = [pl.BlockSpec((tm, K), lambda n, m: (m, 0))]
    if gain is not None:
        operands.append(gain.reshape(1, K).astype(jnp.float32))
        in_specs.append(pl.BlockSpec((1, K), lambda n, m: (0, 0)))
    operands.append(w)
    in_specs.append(pl.BlockSpec((K, tn), lambda n, m: (0, n)))
    if residual is not None:
        operands.append(residual.reshape(M, N))
        in_specs.append(pl.BlockSpec((tm, tn), lambda n, m: (m, n)))
    out = pl.pallas_call(
        functools.partial(_proj_kernel, normed=gain is not None, with_residual=residual is not None),
        grid=(N // tn, M // tm),
        in_specs=in_specs,
        out_specs=pl.BlockSpec((tm, tn), lambda n, m: (m, n)),
        out_shape=jax.ShapeDtypeStruct((M, N), out_dtype),
        compiler_params=pltpu.CompilerParams(
            dimension_semantics=("parallel", "parallel"),
            vmem_limit_bytes=48 * 1024 * 1024),
        name="fused_matmul",
    )(*operands)
    return out.reshape(*lead, N)


OUT_TM = 512


def _gated_out_kernel(of_ref, ob_ref, z_ref, g_ref, w_ref, res_ref, o_ref):
    o = of_ref[...] + ob_ref[...]
    z = z_ref[...]
    g = g_ref[...]
    heads = []
    for h in range(o.shape[-1] // LANES):
        oh = o[:, h * LANES:(h + 1) * LANES]
        heads.append(oh * lax.rsqrt(jnp.mean(oh * oh, axis=-1, keepdims=True) + EPS) * g)
    y = jnp.concatenate(heads, axis=-1) * (z * jax.nn.sigmoid(z))
    o_ref[...] = res_ref[...] + jnp.dot(y.astype(jnp.bfloat16), w_ref[...].astype(jnp.bfloat16),
                                        preferred_element_type=jnp.float32)


def gated_out_proj(o_fwd, o_bwd, z, norm_g, w_out, residual):
    lead, K = o_fwd.shape[:-1], o_fwd.shape[-1]
    N = w_out.shape[-1]
    M = math.prod(lead)
    tm = min(OUT_TM, M)
    row = lambda width: pl.BlockSpec((tm, width), lambda m: (m, 0))
    out = pl.pallas_call(
        _gated_out_kernel,
        grid=(M // tm,),
        in_specs=[row(K), row(K), row(K),
                  pl.BlockSpec((1, LANES), lambda m: (0, 0)),
                  pl.BlockSpec((K, N), lambda m: (0, 0)),
                  row(N)],
        out_specs=row(N),
        out_shape=jax.ShapeDtypeStruct((M, N), jnp.float32),
        compiler_params=pltpu.CompilerParams(
            dimension_semantics=("parallel",),
            vmem_limit_bytes=48 * 1024 * 1024),
        name="gated_out_proj",
    )(o_fwd.reshape(M, K), o_bwd.reshape(M, K), z.reshape(M, K),
      norm_g.reshape(1, LANES).astype(jnp.float32), w_out, residual.reshape(M, N))
    return out.reshape(*lead, N)


def _final_norm_kernel(x_ref, g_ref, o_ref):
    x = x_ref[...]
    y = x * lax.rsqrt(jnp.mean(x * x, axis=-1, keepdims=True) + EPS)
    o_ref[...] = y * g_ref[...]


def final_norm(x, g):
    B, S, D = x.shape
    x2 = x.reshape(B * S, D)
    tm = 1024
    out = pl.pallas_call(
        _final_norm_kernel,
        grid=(B * S // tm,),
        in_specs=[pl.BlockSpec((tm, D), lambda i: (i, 0)),
                  pl.BlockSpec((1, D), lambda i: (0, 0))],
        out_specs=pl.BlockSpec((tm, D), lambda i: (i, 0)),
        out_shape=jax.ShapeDtypeStruct((B * S, D), x.dtype),
        name="final_norm",
    )(x2, g.reshape(1, D))
    return out.reshape(B, S, D)


def kernel(x, positions, norm_mix, norm_ffn, norm_final,
           gdn_w_in, gdn_conv, gdn_A_log, gdn_dt_bias, gdn_norm, gdn_w_out,
           hgrn_w_in, hgrn_lb, hgrn_norm, hgrn_w_out,
           sc_w_in, sc_conv, sc_w_out,
           mla_w_in, mla_q_norm, mla_w_uq, mla_kv_norm, mla_w_ukv, mla_w_o,
           moe_router, moe_w_gate, moe_w_up, moe_w_down):
    for i in range(DEPTH):
        m, j = i % N_MIXERS, i // N_MIXERS
        g = norm_mix[i]
        if m == 0:
            x = gdn_mixer(x, g, gdn_w_in[j], gdn_conv[j], gdn_A_log[j], gdn_dt_bias[j], gdn_norm[j], gdn_w_out[j])
        elif m == 1:
            x = hgrn2_mixer(x, g, i, hgrn_w_in[j], hgrn_lb, hgrn_norm[j], hgrn_w_out[j])
        elif m == 2:
            x = shortconv_mixer(x, g, sc_w_in[j], sc_conv[j], sc_w_out[j])
        else:
            x = mla_mixer(x, g, positions, mla_w_in[j], mla_q_norm[j], mla_w_uq[j],
                          mla_kv_norm[j], mla_w_ukv[j], mla_w_o[j])
        x = expert_choice_ffn(x, norm_ffn[i], moe_router[i], moe_w_gate[i], moe_w_up[i], moe_w_down[i])
    return final_norm(x, norm_final)
```

```python
import functools
import math

import jax
import jax.numpy as jnp
from jax import lax
from jax.experimental import pallas as pl
from jax.experimental.pallas import tpu as pltpu

D_MODEL = 1024
BATCH = 4
SEQ = 8192
DEPTH = 4
N_MIXERS = 4
EPS = 1e-6
GDN_HEADS = 8
GDN_DK = 128
GDN_DV = 128
GDN_CONV = 5
HGRN_EXPAND = 128
HGRN_HEADS = D_MODEL // HGRN_EXPAND
HGRN_DF = HGRN_EXPAND
HGRN_DI = D_MODEL // HGRN_HEADS
HGRN_F = HGRN_HEADS * HGRN_DF
SC_WIDTH = 3
MLA_HEADS = 16
MLA_NOPE = 64
MLA_ROPE = 32
MLA_V = 64
MLA_Q_LORA = 384
MLA_KV_LORA = 256
ROPE_THETA = 10000.0
N_EXPERTS = 16
D_EXPERT = 2048
CAPACITY_FACTOR = 2


def rms_norm(x, g):
    xf = x.astype(jnp.float32)
    y = xf * lax.rsqrt(jnp.mean(xf * xf, axis=-1, keepdims=True) + EPS)
    return (y * g.astype(jnp.float32)).astype(x.dtype)


def l2_normalize(x):
    xf = x.astype(jnp.float32)
    return xf * lax.rsqrt(jnp.sum(xf * xf, axis=-1, keepdims=True) + EPS)


def centred_depthwise_conv(x, w):
    K, C = w.shape
    return lax.conv_general_dilated(
        x, w[:, None, :], window_strides=(1,), padding=[(K // 2, K // 2)],
        dimension_numbers=('NWC', 'WIO', 'NWC'), feature_group_count=C)


LANES = 128
SUBLANES = 8
CONV_TILE = 256
CONV_CHANNELS = 1024


def _row_conv(prev, x, nxt, w):
    T, K = x.shape[0], w.shape[0]
    ext = jnp.concatenate([prev, x, nxt], axis=0)
    acc = None
    for j in range(K):
        start = SUBLANES - K // 2 + j
        term = ext[start:start + T] * w[j:j + 1, :]
        acc = term if acc is None else acc + term
    return acc


def _halo_specs(S, T, ct):
    r = T // SUBLANES
    return [pl.BlockSpec((1, SUBLANES, ct), lambda b, t, c: (b, jnp.maximum(t * r - 1, 0), c)),
            pl.BlockSpec((1, T, ct), lambda b, t, c: (b, t, c)),
            pl.BlockSpec((1, SUBLANES, ct), lambda b, t, c: (b, jnp.minimum((t + 1) * r, S // SUBLANES - 1), c))]


def _gdn_prep_kernel(xp_ref, x_ref, xn_ref, w_ref, o_ref, *, n_norm_chunks):
    t, c = pl.program_id(1), pl.program_id(2)
    prev = jnp.where(t == 0, 0.0, xp_ref[0])
    nxt = jnp.where(t == pl.num_programs(1) - 1, 0.0, xn_ref[0])
    y = _row_conv(prev, x_ref[0], nxt, w_ref[...])
    y = y * jax.nn.sigmoid(y)
    heads = []
    for h in range(y.shape[-1] // LANES):
        yh = y[:, h * LANES:(h + 1) * LANES]
        heads.append(yh * lax.rsqrt(jnp.sum(yh * yh, axis=-1, keepdims=True) + EPS))
    o_ref[0] = jnp.where(c < n_norm_chunks, jnp.concatenate(heads, axis=-1), y)


def gdn_prep(qkv_raw, conv_w, n_norm):
    B, S, C = qkv_raw.shape
    T, ct = min(CONV_TILE, S), min(CONV_CHANNELS, C)
    return pl.pallas_call(
        functools.partial(_gdn_prep_kernel, n_norm_chunks=n_norm // ct),
        grid=(B, S // T, C // ct),
        in_specs=_halo_specs(S, T, ct) + [pl.BlockSpec((conv_w.shape[0], ct), lambda b, t, c: (0, c))],
        out_specs=pl.BlockSpec((1, T, ct), lambda b, t, c: (b, t, c)),
        out_shape=jax.ShapeDtypeStruct((B, S, C), jnp.float32),
        compiler_params=pltpu.CompilerParams(dimension_semantics=("parallel", "parallel", "parallel")),
        name="gdn_prep",
    )(qkv_raw, qkv_raw, qkv_raw, conv_w)


def _shortconv_gate_kernel(gb_ref, cp_ref, c_ref, cn_ref, up_ref, u_ref, un_ref, w_ref, o_ref):
    t = pl.program_id(1)
    first, last = t == 0, t == pl.num_programs(1) - 1
    prev = jnp.where(first, 0.0, cp_ref[0] * up_ref[0])
    nxt = jnp.where(last, 0.0, cn_ref[0] * un_ref[0])
    o_ref[0] = gb_ref[0] * _row_conv(prev, c_ref[0] * u_ref[0], nxt, w_ref[...])


def shortconv_gate(gb, gc, u, conv_w):
    B, S, C = gb.shape
    T, ct = min(CONV_TILE, S), min(CONV_CHANNELS, C)
    halo = _halo_specs(S, T, ct)
    return pl.pallas_call(
        _shortconv_gate_kernel,
        grid=(B, S // T, C // ct),
        in_specs=[halo[1]] + halo + halo + [pl.BlockSpec((conv_w.shape[0], ct), lambda b, t, c: (0, c))],
        out_specs=pl.BlockSpec((1, T, ct), lambda b, t, c: (b, t, c)),
        out_shape=jax.ShapeDtypeStruct((B, S, C), jnp.float32),
        compiler_params=pltpu.CompilerParams(dimension_semantics=("parallel", "parallel", "parallel")),
        name="shortconv_gate",
    )(gb, gc, gc, gc, u, u, u, conv_w)


GDN_BLOCK = 16
GDN_TILE = 256
GDN_GROUP = 4


def _gdn_kernel(*refs):
    C, G, H = GDN_BLOCK, GDN_GROUP, GDN_HEADS
    per_dir = 3 * G + 1
    in_refs, out_refs = refs[:2 * per_dir], refs[2 * per_dir:2 * per_dir + 2]
    scratch = refs[2 * per_dir + 2:]
    state_ref, us_ref, ws_ref, gc_ref, ks_ref = scratch[0], *scratch[13:17]
    u_ref, w_ref, qe_ref, kd_ref, a_ref, dec_ref = scratch[1:7]
    u_rd, w_rd, qe_rd, kd_rd, a_rd, dec_rd = scratch[7:13]
    handoff = tuple(zip(scratch[1:7], scratch[7:13]))
    T = in_refs[0].shape[1]
    nb = T // C
    bf16 = jnp.bfloat16
    scale = GDN_DK ** -0.5
    hg = pl.program_id(1)

    @pl.when(pl.program_id(2) == 0)
    def _():
        state_ref[...] = jnp.zeros_like(state_ref)
        for _, dst in handoff:
            dst[...] = jnp.zeros_like(dst)

    slab = lambda r: pl.ds(r, nb, stride=C)
    ones = jnp.ones((LANES, LANES), bf16)
    lane = lax.broadcasted_iota(jnp.int32, (nb, LANES), 1)
    rowsum = lambda x: jnp.broadcast_to(jnp.sum(x, axis=-1, keepdims=True), (nb, LANES))

    def pick(x, idx):
        return rowsum(jnp.where(lane == idx, x, 0.0))

    orders = (list(range(C)), list(range(C - 1, -1, -1)))
    chains = [(d, g) for d in range(2) for g in range(G)]
    sc_refs = [in_refs[d * per_dir + 3 * G] for d in range(2)]
    qkv_refs = {(d, g): [in_refs[d * per_dir + j * G + g] for j in range(3)] for d, g in chains}

    for d, g in chains:
        run = None
        for r in orders[d]:
            g_r = pick(sc_refs[d][0, slab(r), :], d * H + hg * G + g)
            run = g_r if run is None else run + g_r
            gc_ref[d, g, r] = run
            ks_ref[d, g, r] = qkv_refs[d, g][1][0, slab(r), :]
        dec_ref[d, g] = jnp.exp(run)

    def block_phase(n):
        dots = {}
        for d, g in chains:
            order = orders[d]
            q_ref, k_ref, v_ref = qkv_refs[d, g]
            q_r = q_ref[0, slab(order[n]), :] * scale
            k_r = ks_ref[d, g, order[n]]
            k_prev = [ks_ref[d, g, r2] for r2 in order[:n]]
            prods = [q_r * k_r] + [q_r * k_2 for k_2 in k_prev] + [k_r * k_2 for k_2 in k_prev]
            dots[d, g] = jnp.dot(jnp.concatenate(prods, axis=0).astype(bf16), ones,
                                 preferred_element_type=jnp.float32)
        for d, g in chains:
            order = orders[d]
            r = order[n]
            q_ref, k_ref, v_ref = qkv_refs[d, g]
            gc, us, ws = gc_ref.at[d, g], us_ref.at[d, g], ws_ref.at[d, g]
            dt = dots[d, g]
            q_r = q_ref[0, slab(r), :] * scale
            k_r = ks_ref[d, g, r]
            beta_r = pick(sc_refs[d][0, slab(r), :], 2 * H + d * H + hg * G + g)
            gc_r = gc[r]
            eg_r = jnp.exp(gc_r)
            u_r = beta_r * v_ref[0, slab(r), :]
            w_r = beta_r * eg_r * k_r
            a_row = jnp.where(lane == r, dt[:nb], 0.0)
            for m, r2 in enumerate(order[:n]):
                decay = jnp.exp(gc_r - gc[r2])
                l_rr = beta_r * decay * dt[(1 + n + m) * nb:(2 + n + m) * nb]
                u_r = u_r - l_rr * us[r2]
                w_r = w_r - l_rr * ws[r2]
                a_row = jnp.where(lane == r2, decay * dt[(1 + m) * nb:(2 + m) * nb], a_row)
            us[r] = u_r
            ws[r] = w_r
            u_ref[d, g, slab(r), :] = u_r
            w_ref[d, g, slab(r), :] = w_r
            a_ref[d, g, slab(r), :] = a_row
            qe_ref[d, g, slab(r), :] = q_r * eg_r
            kd_ref[d, g, slab(r), :] = k_r * jnp.exp(gc[order[-1]] - gc_r)

    def state_phase(i):
        blks = [i, nb - 1 - i]
        rows = [pl.ds(blk * C, C) for blk in blks]
        r1s = {}
        for d, g in chains:
            wq = jnp.concatenate([w_rd[d, g, rows[d], :], qe_rd[d, g, rows[d], :]], axis=0).astype(bf16)
            r1s[d, g] = jnp.dot(wq, state_ref[d, g].astype(bf16), preferred_element_type=jnp.float32)
        v_news = {c: (u_rd[c[0], c[1], rows[c[0]], :] - r1s[c][:C]).astype(bf16) for c in chains}
        intra, upd = {}, {}
        for d, g in chains:
            a_blk = a_rd[d, g, rows[d], :][:, :C].astype(bf16)
            intra[d, g] = jnp.dot(a_blk, v_news[d, g], preferred_element_type=jnp.float32)
            upd[d, g] = lax.dot_general(kd_rd[d, g, rows[d], :].astype(bf16), v_news[d, g],
                                        (((0,), (0,)), ((), ())), preferred_element_type=jnp.float32)
        for d, g in chains:
            out_refs[d][0, rows[d], g * LANES:(g + 1) * LANES] = r1s[d, g][C:] + intra[d, g]
            state_ref[d, g] = state_ref[d, g] * dec_rd[d, g, pl.ds(blks[d], 1), :] + upd[d, g]

    for n in range(max(C, nb)):
        if n < nb:
            state_phase(n)
        if n < C:
            block_phase(n)
    for src, dst in handoff:
        dst[...] = src[...]


def gdn_bidir(qkv, scal):
    B, S, _ = qkv.shape
    H, G = GDN_HEADS, GDN_GROUP
    T = min(GDN_TILE, S)
    nT, nG = S // T, H // G
    nb = T // GDN_BLOCK
    fwd_in = lambda t: jnp.minimum(t, nT - 1)
    bwd_in = lambda t: jnp.maximum(nT - 1 - t, 0)
    fwd_out = lambda t: jnp.maximum(t - 1, 0)
    bwd_out = lambda t: jnp.minimum(nT - t, nT - 1)
    blk = lambda tile, off: pl.BlockSpec((1, T, LANES), lambda b, hg, t: (b, tile(t), off + hg * G))
    in_specs, operands = [], []
    for tile in (fwd_in, bwd_in):
        for j in range(3):
            for g in range(G):
                in_specs.append(blk(tile, j * H + g))
                operands.append(qkv)
        in_specs.append(pl.BlockSpec((1, T, LANES), lambda b, hg, t, tile=tile: (b, tile(t), 0)))
        operands.append(scal)
    out_specs = [pl.BlockSpec((1, T, G * LANES), lambda b, hg, t: (b, fwd_out(t), hg)),
                 pl.BlockSpec((1, T, G * LANES), lambda b, hg, t: (b, bwd_out(t), hg))]
    out_sd = jax.ShapeDtypeStruct((B, S, H * LANES), jnp.float32)
    tile_buf = pltpu.VMEM((2, G, T, LANES), jnp.float32)
    dec_buf = pltpu.VMEM((2, G, nb, LANES), jnp.float32)
    slabs = pltpu.VMEM((2, G, GDN_BLOCK, nb, LANES), jnp.float32)
    return pl.pallas_call(
        _gdn_kernel,
        grid=(B, nG, nT + 1),
        in_specs=in_specs,
        out_specs=out_specs,
        out_shape=[out_sd, out_sd],
        scratch_shapes=[pltpu.VMEM((2, G, LANES, LANES), jnp.float32),
                        tile_buf, tile_buf, tile_buf, tile_buf, tile_buf, dec_buf,
                        tile_buf, tile_buf, tile_buf, tile_buf, tile_buf, dec_buf,
                        slabs, slabs, slabs, slabs],
        compiler_params=pltpu.CompilerParams(
            dimension_semantics=("parallel", "parallel", "arbitrary"),
            vmem_limit_bytes=48 * 1024 * 1024),
        name="gdn_bidir",
    )(*operands)


def gdn_mixer(x, gain, w_in, conv_w, A_log, dt_bias, norm_g, w_out):
    B, S, _ = x.shape
    H, dk, dv = GDN_HEADS, GDN_DK, GDN_DV
    n_qkv, n_z = 2 * H * dk + H * dv, H * dv
    qkv = fused_matmul(x, w_in[:, :n_qkv], gain=gain)
    z = fused_matmul(x, w_in[:, n_qkv:n_qkv + n_z], gain=gain)
    ab = fused_matmul(x, w_in[:, n_qkv + n_z:], gain=gain)
    a, b = ab[..., :2 * H], ab[..., 2 * H:]
    qkv = gdn_prep(qkv, conv_w, n_norm=2 * H * dk)
    a = a.astype(jnp.float32).reshape(B, S, 2, H)
    b = b.astype(jnp.float32).reshape(B, S, 2, H)
    g = -jnp.exp(A_log.astype(jnp.float32)) * jax.nn.softplus(a + dt_bias.astype(jnp.float32))
    beta = jax.nn.sigmoid(b)
    scal = jnp.concatenate([g.reshape(B, S, 2 * H), beta.reshape(B, S, 2 * H),
                            jnp.zeros((B, S, LANES - 4 * H), jnp.float32)], axis=-1)
    o_fwd, o_bwd = gdn_bidir(qkv, scal)
    return gated_out_proj(o_fwd, o_bwd, z, norm_g, w_out, x)


HGRN_BLOCK = 16
HGRN_TILE = 256
HGRN_GROUP = 4


def _hgrn_kernel(*refs):
    C, G = HGRN_BLOCK, HGRN_GROUP
    per_dir = 3 * G + 1
    in_refs, out_refs = refs[:2 * per_dir], refs[2 * per_dir:2 * per_dir + 2]
    scratch = refs[2 * per_dir + 2:]
    state_ref, b_ref, kk_ref, vs_ref, qx_ref, kx_ref = scratch[0], *scratch[11:16]
    qe_ref, ke_ref, od_ref, vv_ref, dec_ref = scratch[1:6]
    qe_rd, ke_rd, od_rd, vv_rd, dec_rd = scratch[6:11]
    handoff = tuple(zip(scratch[1:6], scratch[6:11]))
    T = in_refs[0].shape[1]
    nb = T // C
    bf16 = jnp.bfloat16
    chains = [(d, g) for d in range(2) for g in range(G)]
    qvl_refs = {(d, g): [in_refs[d * per_dir + j * G + g] for j in range(3)] for d, g in chains}
    lb_refs = [in_refs[d * per_dir + 3 * G] for d in range(2)]

    @pl.when(pl.program_id(2) == 0)
    def _():
        state_ref[...] = jnp.zeros_like(state_ref)
        for _, dst in handoff:
            dst[...] = jnp.zeros_like(dst)

    slab = lambda r: pl.ds(r, nb, stride=C)
    ones = jnp.ones((LANES, LANES), bf16)

    def log_f(x, log_lb, log_1m_lb):
        log_sig = jnp.minimum(x, 0.0) - jnp.log(1.0 + jnp.exp(-jnp.abs(x)))
        a, b = log_lb, log_1m_lb + log_sig
        return jnp.maximum(a, b) + jnp.log(1.0 + jnp.exp(-jnp.abs(a - b)))

    for d, g in chains:
        order = list(range(C)) if d == 0 else list(range(C - 1, -1, -1))
        v_ref, l_ref = qvl_refs[d, g][1:]
        log_lb = lb_refs[d][0:1, g * LANES:(g + 1) * LANES]
        log_1m_lb = lb_refs[d][1:2, g * LANES:(g + 1) * LANES]
        run = None
        for r in order:
            lf = log_f(l_ref[0, slab(r), :], log_lb, log_1m_lb)
            run = lf if run is None else run + lf
            b_ref[d, g, r] = run
            kk_ref[d, g, r] = 1.0 - jnp.exp(lf)
            vs_ref[d, g, r] = v_ref[0, slab(r), :]
        dec_ref[d, g] = jnp.exp(run)

    n_levels = C.bit_length() - 1
    for d, g in chains:
        q_ref = qvl_refs[d, g][0]
        bs, kk = b_ref.at[d, g], kk_ref.at[d, g]
        for r in range(C):
            q_r = q_ref[0, slab(r), :]
            for lv in range(n_levels):
                upper = (r >> lv) & 1 == 1
                mid = (r >> lv) << lv if upper else ((r >> lv) | 1) << lv
                b_mid = bs[mid - 1] if d == 0 else bs[mid]
                if upper == (d == 0):
                    qx_ref[d, g, lv, r] = q_r * jnp.exp(bs[r] - b_mid)
                else:
                    kx_ref[d, g, lv, r] = kk[r] * jnp.exp(b_mid - bs[r])

    def block_phase(r):
        scores, partners = {}, {}
        for d, g in chains:
            q_ref, v_ref, _ = qvl_refs[d, g]
            bs, kk = b_ref.at[d, g], kk_ref.at[d, g]
            q_r = q_ref[0, slab(r), :]
            b_r = bs[r]
            btot = bs[C - 1 if d == 0 else 0]
            qe_ref[d, g, slab(r), :] = q_r * jnp.exp(b_r)
            ke_ref[d, g, slab(r), :] = kk[r] * jnp.exp(btot - b_r)
            vv_ref[d, g, slab(r), :] = vs_ref[d, g, r]
            partners[d, g] = list(range(r + 1) if d == 0 else range(r, C))
            level = lambda r2: (r ^ r2).bit_length() - 1
            terms = [q_r * kk[r] if r2 == r else qx_ref[d, g, level(r2), r] * kx_ref[d, g, level(r2), r2]
                     for r2 in partners[d, g]]
            scores[d, g] = jnp.dot(jnp.concatenate(terms, axis=0).astype(bf16), ones,
                                   preferred_element_type=jnp.float32)
        for d, g in chains:
            acc = None
            for m, r2 in enumerate(partners[d, g]):
                term = scores[d, g][m * nb:(m + 1) * nb] * vs_ref[d, g, r2]
                acc = term if acc is None else acc + term
            od_ref[d, g, slab(r), :] = acc

    def state_phase(i):
        blks = [i, nb - 1 - i]
        rows = [pl.ds(blk * C, C) for blk in blks]
        inter, upd = {}, {}
        for d, g in chains:
            qe = qe_rd[d, g, rows[d], :].astype(bf16)
            inter[d, g] = lax.dot_general(qe, state_ref[d, g].astype(bf16), (((1,), (1,)), ((), ())),
                                          preferred_element_type=jnp.float32)
            upd[d, g] = lax.dot_general(vv_rd[d, g, rows[d], :].astype(bf16), ke_rd[d, g, rows[d], :].astype(bf16),
                                        (((0,), (0,)), ((), ())), preferred_element_type=jnp.float32)
        for d, g in chains:
            out_refs[d][0, rows[d], g * LANES:(g + 1) * LANES] = od_rd[d, g, rows[d], :] + inter[d, g]
            state_ref[d, g] = state_ref[d, g] * dec_rd[d, g, pl.ds(blks[d], 1), :] + upd[d, g]

    for n in range(max(C, nb)):
        if n < nb:
            state_phase(n)
        if n < C:
            block_phase(n)
    for src, dst in handoff:
        dst[...] = src[...]


def hgrn2_bidir(proj, log_lb):
    B, S, _ = proj.shape
    H = HGRN_HEADS
    T = min(HGRN_TILE, S)
    G = HGRN_GROUP
    nT, nG = S // T, H // G
    nb = T // HGRN_BLOCK
    fwd_in = lambda t: jnp.minimum(t, nT - 1)
    bwd_in = lambda t: jnp.maximum(nT - 1 - t, 0)
    fwd_out = lambda t: jnp.maximum(t - 1, 0)
    bwd_out = lambda t: jnp.minimum(nT - t, nT - 1)
    blk = lambda tile, off: pl.BlockSpec((1, T, LANES), lambda b, hg, t: (b, tile(t), off + hg * G))
    in_specs, operands = [], []
    for d, tile in enumerate((fwd_in, bwd_in)):
        for off in (0, H, (2 + d) * H):
            for g in range(G):
                in_specs.append(blk(tile, off + g))
                operands.append(proj)
        in_specs.append(pl.BlockSpec((None, 2, G * LANES), lambda b, hg, t, d=d: (d, 0, hg)))
        operands.append(log_lb)
    out_specs = [pl.BlockSpec((1, T, G * LANES), lambda b, hg, t: (b, fwd_out(t), hg)),
                 pl.BlockSpec((1, T, G * LANES), lambda b, hg, t: (b, bwd_out(t), hg))]
    out_sd = jax.ShapeDtypeStruct((B, S, H * LANES), jnp.float32)
    tile_buf = pltpu.VMEM((2, G, T, LANES), jnp.float32)
    dec_buf = pltpu.VMEM((2, G, nb, LANES), jnp.float32)
    slabs = pltpu.VMEM((2, G, HGRN_BLOCK, nb, LANES), jnp.float32)
    factors = pltpu.VMEM((2, G, HGRN_BLOCK.bit_length() - 1, HGRN_BLOCK, nb, LANES), jnp.float32)
    return pl.pallas_call(
        _hgrn_kernel,
        grid=(B, nG, nT + 1),
        in_specs=in_specs,
        out_specs=out_specs,
        out_shape=[out_sd, out_sd],
        scratch_shapes=[pltpu.VMEM((2, G, LANES, LANES), jnp.float32),
                        tile_buf, tile_buf, tile_buf, tile_buf, dec_buf,
                        tile_buf, tile_buf, tile_buf, tile_buf, dec_buf,
                        slabs, slabs, slabs, factors, factors],
        compiler_params=pltpu.CompilerParams(
            dimension_semantics=("parallel", "parallel", "arbitrary"),
            vmem_limit_bytes=48 * 1024 * 1024),
        name="hgrn2_bidir",
    )(*operands)


def hgrn2_mixer(x, gain, layer, w_in, lb_table, norm_g, w_out):
    B, S, _ = x.shape
    H, dF, dI = HGRN_HEADS, HGRN_DF, HGRN_DI
    n_qv = HGRN_F + H * dI
    proj = fused_matmul(x, jnp.concatenate([w_in[:, :n_qv], w_in[:, n_qv + H * dI:]], axis=1), gain=gain)
    gate = fused_matmul(x, w_in[:, n_qv:n_qv + H * dI], gain=gain)
    lb_w = jax.nn.softmax(lb_table.astype(jnp.float32), axis=1)
    lb = (jnp.cumsum(lb_w, axis=1) - lb_w[:, :1])[:, layer]
    log_lb = jnp.stack([jnp.log(lb), jnp.log1p(-lb)], axis=1)
    o_fwd, o_bwd = hgrn2_bidir(proj, log_lb)
    return gated_out_proj(o_fwd, o_bwd, gate, norm_g, w_out, x)


def shortconv_mixer(x, gain, w_in, conv_w, w_out):
    D = x.shape[-1]
    gb, gc, u = (fused_matmul(x, w_in[:, j * D:(j + 1) * D], gain=gain) for j in range(3))
    return fused_matmul(shortconv_gate(gb, gc, u, conv_w), w_out, residual=x)


def rope(x, pos):
    half = x.shape[-1] // 2
    inv = ROPE_THETA ** (-jnp.arange(half, dtype=jnp.float32) / half)
    ang = pos.astype(jnp.float32)[:, :, None, None] * inv
    cos, sin = jnp.cos(ang), jnp.sin(ang)
    xf = x.astype(jnp.float32)
    x1, x2 = xf[..., :half], xf[..., half:]
    return jnp.concatenate([x1 * cos - x2 * sin, x1 * sin + x2 * cos], axis=-1).astype(x.dtype)


MLA_QK_PAD = 128
MLA_TQ = 512
MLA_TK = 2048


def _mla_attn_kernel(q_ref, k_ref, v_ref, o_ref, *, tk):
    tq = q_ref.shape[1]
    nk = k_ref.shape[1] // tk
    head_lanes = [slice(hh * MLA_QK_PAD, (hh + 1) * MLA_QK_PAD) for hh in range(2)]
    qs = [q_ref[0, :, lanes] for lanes in head_lanes]

    def body(j, carry):
        rows = pl.ds(pl.multiple_of(j * tk, tk), tk)
        v = v_ref[0, rows, :]
        new = []
        for hh in range(2):
            m, l, acc = carry[hh]
            k = k_ref[0, rows, head_lanes[hh]]
            s = lax.dot_general(qs[hh], k, (((1,), (1,)), ((), ())), preferred_element_type=jnp.float32)
            m_new = jnp.maximum(m, jnp.max(s, axis=-1, keepdims=True))
            p = jnp.exp(s - m_new)
            alpha = jnp.exp(m - m_new)
            l = alpha * l + jnp.sum(p, axis=-1, keepdims=True)
            acc = alpha * acc + jnp.dot(p.astype(jnp.bfloat16), v, preferred_element_type=jnp.float32)
            new.append((m_new, l, acc))
        return tuple(new)

    init = (jnp.full((tq, 1), -jnp.inf, jnp.float32), jnp.zeros((tq, 1), jnp.float32),
            jnp.zeros((tq, 2 * MLA_V), jnp.float32))
    (_, l0, acc0), (_, l1, acc1) = lax.fori_loop(0, nk, body, (init, init))
    lane = lax.broadcasted_iota(jnp.int32, (tq, 2 * MLA_V), 1)
    o_ref[0] = jnp.where(lane < MLA_V, acc0 / l0, acc1 / l1).astype(o_ref.dtype)


def mla_attention(qf, kf, vf):
    B, S, _ = qf.shape
    H = MLA_HEADS
    tq, tk = min(MLA_TQ, S), min(MLA_TK, S)
    return pl.pallas_call(
        functools.partial(_mla_attn_kernel, tk=tk),
        grid=(B, H // 2, S // tq),
        in_specs=[pl.BlockSpec((1, tq, 2 * MLA_QK_PAD), lambda b, h, i: (b, i, h)),
                  pl.BlockSpec((1, S, 2 * MLA_QK_PAD), lambda b, h, i: (b, 0, h)),
                  pl.BlockSpec((1, S, 2 * MLA_V), lambda b, h, i: (b, 0, h))],
        out_specs=pl.BlockSpec((1, tq, 2 * MLA_V), lambda b, h, i: (b, i, h)),
        out_shape=jax.ShapeDtypeStruct((B, S, H * MLA_V), jnp.float32),
        compiler_params=pltpu.CompilerParams(
            dimension_semantics=("parallel", "parallel", "arbitrary"),
            vmem_limit_bytes=48 * 1024 * 1024),
        name="mla_attention",
    )(qf, kf, vf)


PACK_TM = 512


def _pack_q_kernel(q_ref, c_ref, sp_ref, sm_ref, o_ref, *, scale, half):
    q = q_ref[...]
    n_heads = q.shape[-1] // LANES
    wide = lambda ref: jnp.concatenate([ref[...]] * n_heads, axis=-1)
    up = pltpu.roll(q, q.shape[-1] - half, axis=1)
    down = pltpu.roll(q, half, axis=1)
    o_ref[...] = ((q * wide(c_ref) + up * wide(sp_ref) + down * wide(sm_ref)) * scale).astype(o_ref.dtype)


def _pack_k_kernel(k_ref, kr_ref, o_ref):
    n_heads = k_ref.shape[-1] // LANES
    o_ref[...] = (k_ref[...] + jnp.concatenate([kr_ref[...]] * n_heads, axis=-1)).astype(o_ref.dtype)


def _pack_call(kernel_fn, wide, narrow, name):
    M, N = wide.shape
    tm = min(PACK_TM, M)
    return pl.pallas_call(
        kernel_fn,
        grid=(M // tm,),
        in_specs=[pl.BlockSpec((tm, N), lambda m: (m, 0))] + [pl.BlockSpec((tm, LANES), lambda m: (m, 0))] * len(narrow),
        out_specs=pl.BlockSpec((tm, N), lambda m: (m, 0)),
        out_shape=jax.ShapeDtypeStruct((M, N), jnp.bfloat16),
        compiler_params=pltpu.CompilerParams(dimension_semantics=("parallel",)),
        name=name,
    )(wide, *narrow)


def mla_mixer(x, gain, pos, w_in, q_norm, w_uq, kv_norm, w_ukv, w_o):
    B, S, _ = x.shape
    H, M = MLA_HEADS, B * S
    n_q, n_kv, half = MLA_Q_LORA, MLA_KV_LORA, MLA_ROPE // 2
    pad = MLA_QK_PAD - MLA_NOPE - MLA_ROPE
    f32 = jnp.float32
    cq = fused_matmul(x, w_in[:, :n_q], gain=gain)
    ckv = fused_matmul(x, w_in[:, n_q:n_q + n_kv], gain=gain)
    kr = fused_matmul(x, w_in[:, n_q + n_kv:], gain=gain)
    w_q = jnp.pad(w_uq.reshape(n_q, H, MLA_NOPE + MLA_ROPE), ((0, 0), (0, 0), (0, pad))).reshape(n_q, H * MLA_QK_PAD)
    w_kv = w_ukv.reshape(n_kv, H, MLA_NOPE + MLA_V)
    w_k = jnp.pad(w_kv[..., :MLA_NOPE], ((0, 0), (0, 0), (0, MLA_QK_PAD - MLA_NOPE))).reshape(n_kv, H * MLA_QK_PAD)
    w_v = w_kv[..., MLA_NOPE:].reshape(n_kv, H * MLA_V)
    q_pad = fused_matmul(cq, w_q, gain=q_norm).reshape(M, H * MLA_QK_PAD)
    k_pad = fused_matmul(ckv, w_k, gain=kv_norm).reshape(M, H * MLA_QK_PAD)
    vf = fused_matmul(ckv, w_v, gain=kv_norm, out_dtype=jnp.bfloat16)
    inv = ROPE_THETA ** (-jnp.arange(half, dtype=f32) / half)
    ang = pos.astype(f32).reshape(M, 1) * inv
    cos, sin = jnp.cos(ang), jnp.sin(ang)
    zeros = lambda n: jnp.zeros((M, n), f32)
    c_tab = jnp.concatenate([jnp.ones((M, MLA_NOPE), f32), cos, cos, zeros(pad)], axis=-1)
    sp_tab = jnp.concatenate([zeros(MLA_NOPE), -sin, zeros(half), zeros(pad)], axis=-1)
    sm_tab = jnp.concatenate([zeros(MLA_NOPE), zeros(half), sin, zeros(pad)], axis=-1)
    scale = (MLA_NOPE + MLA_ROPE) ** -0.5
    qf = _pack_call(functools.partial(_pack_q_kernel, scale=scale, half=half), q_pad, [c_tab, sp_tab, sm_tab], "pack_q")
    kr_tile = jnp.concatenate([zeros(MLA_NOPE), rope(kr[:, :, None, :], pos).reshape(M, MLA_ROPE), zeros(pad)], axis=-1)
    kf = _pack_call(_pack_k_kernel, k_pad, [kr_tile], "pack_k")
    o = mla_attention(qf.reshape(B, S, -1), kf.reshape(B, S, -1), vf)
    return fused_matmul(o, w_o, residual=x)


ROUTE_TM = 1024


def _route_kernel(x_ref, g_ref, wr_ref, h_ref, aff_ref):
    x = x_ref[...]
    h = (x * lax.rsqrt(jnp.mean(x * x, axis=-1, keepdims=True) + EPS) * g_ref[...]).astype(jnp.bfloat16)
    h_ref[...] = h
    logits = jnp.dot(h, wr_ref[...].astype(jnp.bfloat16), preferred_element_type=jnp.float32)
    e = jnp.exp(logits - jnp.max(logits, axis=-1, keepdims=True))
    aff_ref[...] = e / jnp.sum(e, axis=-1, keepdims=True)


def route(x, gain, w_router):
    B, S, D = x.shape
    E = w_router.shape[-1]
    M = B * S
    tm = min(ROUTE_TM, M)
    h, aff = pl.pallas_call(
        _route_kernel,
        grid=(M // tm,),
        in_specs=[pl.BlockSpec((tm, D), lambda i: (i, 0)),
                  pl.BlockSpec((1, D), lambda i: (0, 0)),
                  pl.BlockSpec((D, E), lambda i: (0, 0))],
        out_specs=[pl.BlockSpec((tm, D), lambda i: (i, 0)),
                   pl.BlockSpec((tm, E), lambda i: (i, 0))],
        out_shape=[jax.ShapeDtypeStruct((M, D), jnp.bfloat16), jax.ShapeDtypeStruct((M, E), jnp.float32)],
        compiler_params=pltpu.CompilerParams(dimension_semantics=("parallel",)),
        name="route",
    )(x.reshape(M, D), gain.reshape(1, D).astype(jnp.float32), w_router)
    return h.reshape(B, S, D), aff.reshape(B, S, E)


def expert_choice_ffn(x, gain, w_router, w_gate, w_up, w_down):
    B, S, _ = x.shape
    cap = CAPACITY_FACTOR * S // N_EXPERTS
    h, aff = route(x, gain, w_router)
    gate, idx = lax.top_k(jnp.swapaxes(aff, 1, 2), cap)
    bi = jnp.arange(B)[:, None, None]
    ys = expert_ffn(h[bi, idx], gate, w_gate, w_up, w_down)
    return x.at[bi, idx].add(ys)


FFN_BATCH_ROWS = 2
FFN_TF = 512


def _expert_ffn_kernel(x_ref, g_ref, wg_ref, wu_ref, wd_ref, o_ref):
    f = pl.program_id(2)
    bb, _, cap, d = x_ref.shape
    x = x_ref[...].reshape(bb * cap, d)
    bf16 = jnp.bfloat16
    a = jnp.dot(x, wg_ref[0].astype(bf16), preferred_element_type=jnp.float32)
    u = jnp.dot(x, wu_ref[0].astype(bf16), preferred_element_type=jnp.float32)
    hid = (a * jax.nn.sigmoid(a) * u).astype(bf16)
    y = jnp.dot(hid, wd_ref[0].astype(bf16), preferred_element_type=jnp.float32).reshape(bb, 1, cap, d)

    @pl.when(f == 0)
    def _():
        o_ref[...] = y

    @pl.when(f > 0)
    def _():
        o_ref[...] += y

    @pl.when(f == pl.num_programs(2) - 1)
    def _():
        o_ref[...] = o_ref[...] * g_ref[...]


def expert_ffn(xs, gate, w_gate, w_up, w_down):
    B, E, cap, D = xs.shape
    F = w_gate.shape[-1]
    bb = min(FFN_BATCH_ROWS, B)
    tf = min(FFN_TF, F)
    return pl.pallas_call(
        _expert_ffn_kernel,
        grid=(E, B // bb, F // tf),
        in_specs=[pl.BlockSpec((bb, 1, cap, D), lambda e, b, f: (b, e, 0, 0)),
                  pl.BlockSpec((bb, 1, cap, 1), lambda e, b, f: (b, e, 0, 0)),
                  pl.BlockSpec((1, D, tf), lambda e, b, f: (e, 0, f)),
                  pl.BlockSpec((1, D, tf), lambda e, b, f: (e, 0, f)),
                  pl.BlockSpec((1, tf, D), lambda e, b, f: (e, f, 0))],
        out_specs=pl.BlockSpec((bb, 1, cap, D), lambda e, b, f: (b, e, 0, 0)),
        out_shape=jax.ShapeDtypeStruct((B, E, cap, D), jnp.float32),
        compiler_params=pltpu.CompilerParams(
            dimension_semantics=("parallel", "parallel", "arbitrary"),
            vmem_limit_bytes=56 * 1024 * 1024),
        name="expert_ffn",
    )(xs.astype(jnp.bfloat16), gate[..., None].astype(jnp.float32), w_gate, w_up, w_down)


PROJ_TM = 1024
PROJ_TN_CHOICES = (1024, 768, 512, 384, 256, 128)


def _proj_kernel(*refs, normed, with_residual):
    refs = list(refs)
    x = refs.pop(0)[...]
    if normed:
        g = refs.pop(0)[...]
        x = x * lax.rsqrt(jnp.mean(x * x, axis=-1, keepdims=True) + EPS) * g
    w = refs.pop(0)[...]
    y = jnp.dot(x.astype(jnp.bfloat16), w.astype(jnp.bfloat16), preferred_element_type=jnp.float32)
    if with_residual:
        y = y + refs.pop(0)[...]
    o_ref = refs.pop(0)
    o_ref[...] = y.astype(o_ref.dtype)


def fused_matmul(x, w, gain=None, residual=None, out_dtype=jnp.float32):
    lead, K = x.shape[:-1], x.shape[-1]
    N = w.shape[-1]
    M = math.prod(lead)
    tm = min(PROJ_TM, M)
    tn = next((t for t in PROJ_TN_CHOICES if N % t == 0), N)
    operands = [x.reshape(M, K)]
    in_specs = [pl.BlockSpec((tm, K), lambda n, m: (m, 0))]
    if gain is not None:
        operands.append(gain.reshape(1, K).astype(jnp.float32))
        in_specs.append(pl.BlockSpec((1, K), lambda n, m: (0, 0)))
    operands.append(w)
    in_specs.append(pl.BlockSpec((K, tn), lambda n, m: (0, n)))
    if residual is not None:
        operands.append(residual.reshape(M, N))
        in_specs.append(pl.BlockSpec((tm, tn), lambda n, m: (m, n)))
    out = pl.pallas_call(
        functools.partial(_proj_kernel, normed=gain is not None, with_residual=residual is not None),
        grid=(N // tn, M // tm),
        in_specs=in_specs,
        out_specs=pl.BlockSpec((tm, tn), lambda n, m: (m, n)),
        out_shape=jax.ShapeDtypeStruct((M, N), out_dtype),
        compiler_params=pltpu.CompilerParams(
            dimension_semantics=("parallel", "parallel"),
            vmem_limit_bytes=48 * 1024 * 1024),
        name="fused_matmul",
    )(*operands)
    return out.reshape(*lead, N)


OUT_TM = 512


def _gated_out_kernel(of_ref, ob_ref, z_ref, g_ref, w_ref, res_ref, o_ref):
    o = of_ref[...] + ob_ref[...]
    z = z_ref[...]
    g = g_ref[...]
    heads = []
    for h in range(o.shape[-1] // LANES):
        oh = o[:, h * LANES:(h + 1) * LANES]
        heads.append(oh * lax.rsqrt(jnp.mean(oh * oh, axis=-1, keepdims=True) + EPS) * g)
    y = jnp.concatenate(heads, axis=-1) * (z * jax.nn.sigmoid(z))
    o_ref[...] = res_ref[...] + jnp.dot(y.astype(jnp.bfloat16), w_ref[...].astype(jnp.bfloat16),
                                        preferred_element_type=jnp.float32)


def gated_out_proj(o_fwd, o_bwd, z, norm_g, w_out, residual):
    lead, K = o_fwd.shape[:-1], o_fwd.shape[-1]
    N = w_out.shape[-1]
    M = math.prod(lead)
    tm = min(OUT_TM, M)
    row = lambda width: pl.BlockSpec((tm, width), lambda m: (m, 0))
    out = pl.pallas_call(
        _gated_out_kernel,
        grid=(M // tm,),
        in_specs=[row(K), row(K), row(K),
                  pl.BlockSpec((1, LANES), lambda m: (0, 0)),
                  pl.BlockSpec((K, N), lambda m: (0, 0)),
                  row(N)],
        out_specs=row(N),
        out_shape=jax.ShapeDtypeStruct((M, N), jnp.float32),
        compiler_params=pltpu.CompilerParams(
            dimension_semantics=("parallel",),
            vmem_limit_bytes=48 * 1024 * 1024),
        name="gated_out_proj",
    )(o_fwd.reshape(M, K), o_bwd.reshape(M, K), z.reshape(M, K),
      norm_g.reshape(1, LANES).astype(jnp.float32), w_out, residual.reshape(M, N))
    return out.reshape(*lead, N)


def _final_norm_kernel(x_ref, g_ref, o_ref):
    x = x_ref[...]
    y = x * lax.rsqrt(jnp.mean(x * x, axis=-1, keepdims=True) + EPS)
    o_ref[...] = y * g_ref[...]


def final_norm(x, g):
    B, S, D = x.shape
    x2 = x.reshape(B * S, D)
    tm = 1024
    out = pl.pallas_call(
        _final_norm_kernel,
        grid=(B * S // tm,),
        in_specs=[pl.BlockSpec((tm, D), lambda i: (i, 0)),
                  pl.BlockSpec((1, D), lambda i: (0, 0))],
        out_specs=pl.BlockSpec((tm, D), lambda i: (i, 0)),
        out_shape=jax.ShapeDtypeStruct((B * S, D), x.dtype),
        name="final_norm",
    )(x2, g.reshape(1, D))
    return out.reshape(B, S, D)


def kernel(x, positions, norm_mix, norm_ffn, norm_final,
           gdn_w_in, gdn_conv, gdn_A_log, gdn_dt_bias, gdn_norm, gdn_w_out,
           hgrn_w_in, hgrn_lb, hgrn_norm, hgrn_w_out,
           sc_w_in, sc_conv, sc_w_out,
           mla_w_in, mla_q_norm, mla_w_uq, mla_kv_norm, mla_w_ukv, mla_w_o,
           moe_router, moe_w_gate, moe_w_up, moe_w_down):
    for i in range(DEPTH):
        m, j = i % N_MIXERS, i // N_MIXERS
        g = norm_mix[i]
        if m == 0:
            x = gdn_mixer(x, g, gdn_w_in[j], gdn_conv[j], gdn_A_log[j], gdn_dt_bias[j], gdn_norm[j], gdn_w_out[j])
        elif m == 1:
            x = hgrn2_mixer(x, g, i, hgrn_w_in[j], hgrn_lb, hgrn_norm[j], hgrn_w_out[j])
        elif m == 2:
            x = shortconv_mixer(x, g, sc_w_in[j], sc_conv[j], sc_w_out[j])
        else:
            x = mla_mixer(x, g, positions, mla_w_in[j], mla_q_norm[j], mla_w_uq[j],
                          mla_kv_norm[j], mla_w_ukv[j], mla_w_o[j])
        x = expert_choice_ffn(x, norm_ffn[i], moe_router[i], moe_w_gate[i], moe_w_up[i], moe_w_down[i])
    return final_norm(x, norm_final)
```

```python
import functools
import math

import jax
import jax.numpy as jnp
from jax import lax
from jax.experimental import pallas as pl
from jax.experimental.pallas import tpu as pltpu

D_MODEL = 1024
BATCH = 4
SEQ = 8192
DEPTH = 4
N_MIXERS = 4
EPS = 1e-6
GDN_HEADS = 8
GDN_DK = 128
GDN_DV = 128
GDN_CONV = 5
HGRN_EXPAND = 128
HGRN_HEADS = D_MODEL // HGRN_EXPAND
HGRN_DF = HGRN_EXPAND
HGRN_DI = D_MODEL // HGRN_HEADS
HGRN_F = HGRN_HEADS * HGRN_DF
SC_WIDTH = 3
MLA_HEADS = 16
MLA_NOPE = 64
MLA_ROPE = 32
MLA_V = 64
MLA_Q_LORA = 384
MLA_KV_LORA = 256
ROPE_THETA = 10000.0
N_EXPERTS = 16
D_EXPERT = 2048
CAPACITY_FACTOR = 2


def rms_norm(x, g):
    xf = x.astype(jnp.float32)
    y = xf * lax.rsqrt(jnp.mean(xf * xf, axis=-1, keepdims=True) + EPS)
    return (y * g.astype(jnp.float32)).astype(x.dtype)


def l2_normalize(x):
    xf = x.astype(jnp.float32)
    return xf * lax.rsqrt(jnp.sum(xf * xf, axis=-1, keepdims=True) + EPS)


def centred_depthwise_conv(x, w):
    K, C = w.shape
    return lax.conv_general_dilated(
        x, w[:, None, :], window_strides=(1,), padding=[(K // 2, K // 2)],
        dimension_numbers=('NWC', 'WIO', 'NWC'), feature_group_count=C)


LANES = 128
SUBLANES = 8
CONV_TILE = 256
CONV_CHANNELS = 1024


def _row_conv(prev, x, nxt, w):
    T, K = x.shape[0], w.shape[0]
    ext = jnp.concatenate([prev, x, nxt], axis=0)
    acc = None
    for j in range(K):
        start = SUBLANES - K // 2 + j
        term = ext[start:start + T] * w[j:j + 1, :]
        acc = term if acc is None else acc + term
    return acc


def _halo_specs(S, T, ct):
    r = T // SUBLANES
    return [pl.BlockSpec((1, SUBLANES, ct), lambda b, t, c: (b, jnp.maximum(t * r - 1, 0), c)),
            pl.BlockSpec((1, T, ct), lambda b, t, c: (b, t, c)),
            pl.BlockSpec((1, SUBLANES, ct), lambda b, t, c: (b, jnp.minimum((t + 1) * r, S // SUBLANES - 1), c))]


def _gdn_prep_kernel(xp_ref, x_ref, xn_ref, w_ref, o_ref, *, n_norm_chunks):
    t, c = pl.program_id(1), pl.program_id(2)
    prev = jnp.where(t == 0, 0.0, xp_ref[0])
    nxt = jnp.where(t == pl.num_programs(1) - 1, 0.0, xn_ref[0])
    y = _row_conv(prev, x_ref[0], nxt, w_ref[...])
    y = y * jax.nn.sigmoid(y)
    heads = []
    for h in range(y.shape[-1] // LANES):
        yh = y[:, h * LANES:(h + 1) * LANES]
        heads.append(yh * lax.rsqrt(jnp.sum(yh * yh, axis=-1, keepdims=True) + EPS))
    o_ref[0] = jnp.where(c < n_norm_chunks, jnp.concatenate(heads, axis=-1), y)


def gdn_prep(qkv_raw, conv_w, n_norm):
    B, S, C = qkv_raw.shape
    T, ct = min(CONV_TILE, S), min(CONV_CHANNELS, C)
    return pl.pallas_call(
        functools.partial(_gdn_prep_kernel, n_norm_chunks=n_norm // ct),
        grid=(B, S // T, C // ct),
        in_specs=_halo_specs(S, T, ct) + [pl.BlockSpec((conv_w.shape[0], ct), lambda b, t, c: (0, c))],
        out_specs=pl.BlockSpec((1, T, ct), lambda b, t, c: (b, t, c)),
        out_shape=jax.ShapeDtypeStruct((B, S, C), jnp.float32),
        compiler_params=pltpu.CompilerParams(dimension_semantics=("parallel", "parallel", "parallel")),
        name="gdn_prep",
    )(qkv_raw, qkv_raw, qkv_raw, conv_w)


def _shortconv_gate_kernel(gb_ref, cp_ref, c_ref, cn_ref, up_ref, u_ref, un_ref, w_ref, o_ref):
    t = pl.program_id(1)
    first, last = t == 0, t == pl.num_programs(1) - 1
    prev = jnp.where(first, 0.0, cp_ref[0] * up_ref[0])
    nxt = jnp.where(last, 0.0, cn_ref[0] * un_ref[0])
    o_ref[0] = gb_ref[0] * _row_conv(prev, c_ref[0] * u_ref[0], nxt, w_ref[...])


def shortconv_gate(gb, gc, u, conv_w):
    B, S, C = gb.shape
    T, ct = min(CONV_TILE, S), min(CONV_CHANNELS, C)
    halo = _halo_specs(S, T, ct)
    return pl.pallas_call(
        _shortconv_gate_kernel,
        grid=(B, S // T, C // ct),
        in_specs=[halo[1]] + halo + halo + [pl.BlockSpec((conv_w.shape[0], ct), lambda b, t, c: (0, c))],
        out_specs=pl.BlockSpec((1, T, ct), lambda b, t, c: (b, t, c)),
        out_shape=jax.ShapeDtypeStruct((B, S, C), jnp.float32),
        compiler_params=pltpu.CompilerParams(dimension_semantics=("parallel", "parallel", "parallel")),
        name="shortconv_gate",
    )(gb, gc, gc, gc, u, u, u, conv_w)


GDN_BLOCK = 16
GDN_TILE = 256
GDN_GROUP = 4


def _gdn_kernel(*refs):
    C, G, H = GDN_BLOCK, GDN_GROUP, GDN_HEADS
    per_dir = 3 * G + 1
    in_refs, out_refs = refs[:2 * per_dir], refs[2 * per_dir:2 * per_dir + 2]
    scratch = refs[2 * per_dir + 2:]
    state_ref, us_ref, ws_ref, gc_ref, ks_ref = scratch[0], *scratch[13:17]
    u_ref, w_ref, qe_ref, kd_ref, a_ref, dec_ref = scratch[1:7]
    u_rd, w_rd, qe_rd, kd_rd, a_rd, dec_rd = scratch[7:13]
    handoff = tuple(zip(scratch[1:7], scratch[7:13]))
    T = in_refs[0].shape[1]
    nb = T // C
    bf16 = jnp.bfloat16
    scale = GDN_DK ** -0.5
    hg = pl.program_id(1)

    @pl.when(pl.program_id(2) == 0)
    def _():
        state_ref[...] = jnp.zeros_like(state_ref)
        for _, dst in handoff:
            dst[...] = jnp.zeros_like(dst)

    slab = lambda r: pl.ds(r, nb, stride=C)
    ones = jnp.ones((LANES, LANES), bf16)
    lane = lax.broadcasted_iota(jnp.int32, (nb, LANES), 1)
    rowsum = lambda x: jnp.broadcast_to(jnp.sum(x, axis=-1, keepdims=True), (nb, LANES))

    def pick(x, idx):
        return rowsum(jnp.where(lane == idx, x, 0.0))

    orders = (list(range(C)), list(range(C - 1, -1, -1)))
    chains = [(d, g) for d in range(2) for g in range(G)]
    sc_refs = [in_refs[d * per_dir + 3 * G] for d in range(2)]
    qkv_refs = {(d, g): [in_refs[d * per_dir + j * G + g] for j in range(3)] for d, g in chains}

    for d, g in chains:
        run = None
        for r in orders[d]:
            g_r = pick(sc_refs[d][0, slab(r), :], d * H + hg * G + g)
            run = g_r if run is None else run + g_r
            gc_ref[d, g, r] = run
            ks_ref[d, g, r] = qkv_refs[d, g][1][0, slab(r), :]
        dec_ref[d, g] = jnp.exp(run)

    def block_phase(n):
        dots = {}
        for d, g in chains:
            order = orders[d]
            q_ref, k_ref, v_ref = qkv_refs[d, g]
            q_r = q_ref[0, slab(order[n]), :] * scale
            k_r = ks_ref[d, g, order[n]]
            k_prev = [ks_ref[d, g, r2] for r2 in order[:n]]
            prods = [q_r * k_r] + [q_r * k_2 for k_2 in k_prev] + [k_r * k_2 for k_2 in k_prev]
            dots[d, g] = jnp.dot(jnp.concatenate(prods, axis=0).astype(bf16), ones,
                                 preferred_element_type=jnp.float32)
        for d, g in chains:
            order = orders[d]
            r = order[n]
            q_ref, k_ref, v_ref = qkv_refs[d, g]
            gc, us, ws = gc_ref.at[d, g], us_ref.at[d, g], ws_ref.at[d, g]
            dt = dots[d, g]
            q_r = q_ref[0, slab(r), :] * scale
            k_r = ks_ref[d, g, r]
            beta_r = pick(sc_refs[d][0, slab(r), :], 2 * H + d * H + hg * G + g)
            gc_r = gc[r]
            eg_r = jnp.exp(gc_r)
            u_r = beta_r * v_ref[0, slab(r), :]
            w_r = beta_r * eg_r * k_r
            a_row = jnp.where(lane == r, dt[:nb], 0.0)
            for m, r2 in enumerate(order[:n]):
                decay = jnp.exp(gc_r - gc[r2])
                l_rr = beta_r * decay * dt[(1 + n + m) * nb:(2 + n + m) * nb]
                u_r = u_r - l_rr * us[r2]
                w_r = w_r - l_rr * ws[r2]
                a_row = jnp.where(lane == r2, decay * dt[(1 + m) * nb:(2 + m) * nb], a_row)
            us[r] = u_r
            ws[r] = w_r
            u_ref[d, g, slab(r), :] = u_r
            w_ref[d, g, slab(r), :] = w_r
            a_ref[d, g, slab(r), :] = a_row
            qe_ref[d, g, slab(r), :] = q_r * eg_r
            kd_ref[d, g, slab(r), :] = k_r * jnp.exp(gc[order[-1]] - gc_r)

    def state_phase(i):
        blks = [i, nb - 1 - i]
        rows = [pl.ds(blk * C, C) for blk in blks]
        r1s = {}
        for d, g in chains:
            wq = jnp.concatenate([w_rd[d, g, rows[d], :], qe_rd[d, g, rows[d], :]], axis=0).astype(bf16)
            r1s[d, g] = jnp.dot(wq, state_ref[d, g].astype(bf16), preferred_element_type=jnp.float32)
        v_news = {c: (u_rd[c[0], c[1], rows[c[0]], :] - r1s[c][:C]).astype(bf16) for c in chains}
        intra, upd = {}, {}
        for d, g in chains:
            a_blk = a_rd[d, g, rows[d], :][:, :C].astype(bf16)
            intra[d, g] = jnp.dot(a_blk, v_news[d, g], preferred_element_type=jnp.float32)
            upd[d, g] = lax.dot_general(kd_rd[d, g, rows[d], :].astype(bf16), v_news[d, g],
                                        (((0,), (0,)), ((), ())), preferred_element_type=jnp.float32)
        for d, g in chains:
            out_refs[d][0, rows[d], g * LANES:(g + 1) * LANES] = r1s[d, g][C:] + intra[d, g]
            state_ref[d, g] = state_ref[d, g] * dec_rd[d, g, pl.ds(blks[d], 1), :] + upd[d, g]

    for n in range(max(C, nb)):
        if n < nb:
            state_phase(n)
        if n < C:
            block_phase(n)
    for src, dst in handoff:
        dst[...] = src[...]


def gdn_bidir(qkv, scal):
    B, S, _ = qkv.shape
    H, G = GDN_HEADS, GDN_GROUP
    T = min(GDN_TILE, S)
    nT, nG = S // T, H // G
    nb = T // GDN_BLOCK
    fwd_in = lambda t: jnp.minimum(t, nT - 1)
    bwd_in = lambda t: jnp.maximum(nT - 1 - t, 0)
    fwd_out = lambda t: jnp.maximum(t - 1, 0)
    bwd_out = lambda t: jnp.minimum(nT - t, nT - 1)
    blk = lambda tile, off: pl.BlockSpec((1, T, LANES), lambda b, hg, t: (b, tile(t), off + hg * G))
    in_specs, operands = [], []
    for tile in (fwd_in, bwd_in):
        for j in range(3):
            for g in range(G):
                in_specs.append(blk(tile, j * H + g))
                operands.append(qkv)
        in_specs.append(pl.BlockSpec((1, T, LANES), lambda b, hg, t, tile=tile: (b, tile(t), 0)))
        operands.append(scal)
    out_specs = [pl.BlockSpec((1, T, G * LANES), lambda b, hg, t: (b, fwd_out(t), hg)),
                 pl.BlockSpec((1, T, G * LANES), lambda b, hg, t: (b, bwd_out(t), hg))]
    out_sd = jax.ShapeDtypeStruct((B, S, H * LANES), jnp.float32)
    tile_buf = pltpu.VMEM((2, G, T, LANES), jnp.float32)
    dec_buf = pltpu.VMEM((2, G, nb, LANES), jnp.float32)
    slabs = pltpu.VMEM((2, G, GDN_BLOCK, nb, LANES), jnp.float32)
    return pl.pallas_call(
        _gdn_kernel,
        grid=(B, nG, nT + 1),
        in_specs=in_specs,
        out_specs=out_specs,
        out_shape=[out_sd, out_sd],
        scratch_shapes=[pltpu.VMEM((2, G, LANES, LANES), jnp.float32),
                        tile_buf, tile_buf, tile_buf, tile_buf, tile_buf, dec_buf,
                        tile_buf, tile_buf, tile_buf, tile_buf, tile_buf, dec_buf,
                        slabs, slabs, slabs, slabs],
        compiler_params=pltpu.CompilerParams(
            dimension_semantics=("parallel", "parallel", "arbitrary"),
            vmem_limit_bytes=48 * 1024 * 1024),
        name="gdn_bidir",
    )(*operands)


def gdn_mixer(x, gain, w_in, conv_w, A_log, dt_bias, norm_g, w_out):
    B, S, _ = x.shape
    H, dk, dv = GDN_HEADS, GDN_DK, GDN_DV
    n_qkv, n_z = 2 * H * dk + H * dv, H * dv
    qkv = fused_matmul(x, w_in[:, :n_qkv], gain=gain)
    z = fused_matmul(x, w_in[:, n_qkv:n_qkv + n_z], gain=gain)
    ab = fused_matmul(x, w_in[:, n_qkv + n_z:], gain=gain)
    a, b = ab[..., :2 * H], ab[..., 2 * H:]
    qkv = gdn_prep(qkv, conv_w, n_norm=2 * H * dk)
    a = a.astype(jnp.float32).reshape(B, S, 2, H)
    b = b.astype(jnp.float32).reshape(B, S, 2, H)
    g = -jnp.exp(A_log.astype(jnp.float32)) * jax.nn.softplus(a + dt_bias.astype(jnp.float32))
    beta = jax.nn.sigmoid(b)
    scal = jnp.concatenate([g.reshape(B, S, 2 * H), beta.reshape(B, S, 2 * H),
                            jnp.zeros((B, S, LANES - 4 * H), jnp.float32)], axis=-1)
    o_fwd, o_bwd = gdn_bidir(qkv, scal)
    return gated_out_proj(o_fwd, o_bwd, z, norm_g, w_out, x)


HGRN_BLOCK = 16
HGRN_TILE = 256
HGRN_GROUP = 4


def _hgrn_kernel(*refs):
    C, G = HGRN_BLOCK, HGRN_GROUP
    per_dir = 3 * G + 1
    in_refs, out_refs = refs[:2 * per_dir], refs[2 * per_dir:2 * per_dir + 2]
    scratch = refs[2 * per_dir + 2:]
    state_ref, b_ref, kk_ref, vs_ref, qx_ref, kx_ref = scratch[0], *scratch[11:16]
    qe_ref, ke_ref, od_ref, vv_ref, dec_ref = scratch[1:6]
    qe_rd, ke_rd, od_rd, vv_rd, dec_rd = scratch[6:11]
    handoff = tuple(zip(scratch[1:6], scratch[6:11]))
    T = in_refs[0].shape[1]
    nb = T // C
    bf16 = jnp.bfloat16
    chains = [(d, g) for d in range(2) for g in range(G)]
    qvl_refs = {(d, g): [in_refs[d * per_dir + j * G + g] for j in range(3)] for d, g in chains}
    lb_refs = [in_refs[d * per_dir + 3 * G] for d in range(2)]

    @pl.when(pl.program_id(2) == 0)
    def _():
        state_ref[...] = jnp.zeros_like(state_ref)
        for _, dst in handoff:
            dst[...] = jnp.zeros_like(dst)

    slab = lambda r: pl.ds(r, nb, stride=C)
    ones = jnp.ones((LANES, LANES), bf16)

    def log_f(x, log_lb, log_1m_lb):
        log_sig = jnp.minimum(x, 0.0) - jnp.log(1.0 + jnp.exp(-jnp.abs(x)))
        a, b = log_lb, log_1m_lb + log_sig
        return jnp.maximum(a, b) + jnp.log(1.0 + jnp.exp(-jnp.abs(a - b)))

    for d, g in chains:
        order = list(range(C)) if d == 0 else list(range(C - 1, -1, -1))
        v_ref, l_ref = qvl_refs[d, g][1:]
        log_lb = lb_refs[d][0:1, g * LANES:(g + 1) * LANES]
        log_1m_lb = lb_refs[d][1:2, g * LANES:(g + 1) * LANES]
        run = None
        for r in order:
            lf = log_f(l_ref[0, slab(r), :], log_lb, log_1m_lb)
            run = lf if run is None else run + lf
            b_ref[d, g, r] = run
            kk_ref[d, g, r] = 1.0 - jnp.exp(lf)
            vs_ref[d, g, r] = v_ref[0, slab(r), :]
        dec_ref[d, g] = jnp.exp(run)

    n_levels = C.bit_length() - 1
    for d, g in chains:
        q_ref = qvl_refs[d, g][0]
        bs, kk = b_ref.at[d, g], kk_ref.at[d, g]
        for r in range(C):
            q_r = q_ref[0, slab(r), :]
            for lv in range(n_levels):
                upper = (r >> lv) & 1 == 1
                mid = (r >> lv) << lv if upper else ((r >> lv) | 1) << lv
                b_mid = bs[mid - 1] if d == 0 else bs[mid]
                if upper == (d == 0):
                    qx_ref[d, g, lv, r] = q_r * jnp.exp(bs[r] - b_mid)
                else:
                    kx_ref[d, g, lv, r] = kk[r] * jnp.exp(b_mid - bs[r])

    def block_phase(r):
        scores, partners = {}, {}
        for d, g in chains:
            q_ref, v_ref, _ = qvl_refs[d, g]
            bs, kk = b_ref.at[d, g], kk_ref.at[d, g]
            q_r = q_ref[0, slab(r), :]
            b_r = bs[r]
            btot = bs[C - 1 if d == 0 else 0]
            qe_ref[d, g, slab(r), :] = q_r * jnp.exp(b_r)
            ke_ref[d, g, slab(r), :] = kk[r] * jnp.exp(btot - b_r)
            vv_ref[d, g, slab(r), :] = vs_ref[d, g, r]
            partners[d, g] = list(range(r + 1) if d == 0 else range(r, C))
            level = lambda r2: (r ^ r2).bit_length() - 1
            terms = [q_r * kk[r] if r2 == r else qx_ref[d, g, level(r2), r] * kx_ref[d, g, level(r2), r2]
                     for r2 in partners[d, g]]
            scores[d, g] = jnp.dot(jnp.concatenate(terms, axis=0).astype(bf16), ones,
                                   preferred_element_type=jnp.float32)
        for d, g in chains:
            acc = None
            for m, r2 in enumerate(partners[d, g]):
                term = scores[d, g][m * nb:(m + 1) * nb] * vs_ref[d, g, r2]
                acc = term if acc is None else acc + term
            od_ref[d, g, slab(r), :] = acc

    def state_phase(i):
        blks = [i, nb - 1 - i]
        rows = [pl.ds(blk * C, C) for blk in blks]
        inter, upd = {}, {}
        for d, g in chains:
            qe = qe_rd[d, g, rows[d], :].astype(bf16)
            inter[d, g] = lax.dot_general(qe, state_ref[d, g].astype(bf16), (((1,), (1,)), ((), ())),
                                          preferred_element_type=jnp.float32)
            upd[d, g] = lax.dot_general(vv_rd[d, g, rows[d], :].astype(bf16), ke_rd[d, g, rows[d], :].astype(bf16),
                                        (((0,), (0,)), ((), ())), preferred_element_type=jnp.float32)
        for d, g in chains:
            out_refs[d][0, rows[d], g * LANES:(g + 1) * LANES] = od_rd[d, g, rows[d], :] + inter[d, g]
            state_ref[d, g] = state_ref[d, g] * dec_rd[d, g, pl.ds(blks[d], 1), :] + upd[d, g]

    for n in range(max(C, nb)):
        if n < nb:
            state_phase(n)
        if n < C:
            block_phase(n)
    for src, dst in handoff:
        dst[...] = src[...]


def hgrn2_bidir(proj, log_lb):
    B, S, _ = proj.shape
    H = HGRN_HEADS
    T = min(HGRN_TILE, S)
    G = HGRN_GROUP
    nT, nG = S // T, H // G
    nb = T // HGRN_BLOCK
    fwd_in = lambda t: jnp.minimum(t, nT - 1)
    bwd_in = lambda t: jnp.maximum(nT - 1 - t, 0)
    fwd_out = lambda t: jnp.maximum(t - 1, 0)
    bwd_out = lambda t: jnp.minimum(nT - t, nT - 1)
    blk = lambda tile, off: pl.BlockSpec((1, T, LANES), lambda b, hg, t: (b, tile(t), off + hg * G))
    in_specs, operands = [], []
    for d, tile in enumerate((fwd_in, bwd_in)):
        for off in (0, H, (2 + d) * H):
            for g in range(G):
                in_specs.append(blk(tile, off + g))
                operands.append(proj)
        in_specs.append(pl.BlockSpec((None, 2, G * LANES), lambda b, hg, t, d=d: (d, 0, hg)))
        operands.append(log_lb)
    out_specs = [pl.BlockSpec((1, T, G * LANES), lambda b, hg, t: (b, fwd_out(t), hg)),
                 pl.BlockSpec((1, T, G * LANES), lambda b, hg, t: (b, bwd_out(t), hg))]
    out_sd = jax.ShapeDtypeStruct((B, S, H * LANES), jnp.float32)
    tile_buf = pltpu.VMEM((2, G, T, LANES), jnp.float32)
    dec_buf = pltpu.VMEM((2, G, nb, LANES), jnp.float32)
    slabs = pltpu.VMEM((2, G, HGRN_BLOCK, nb, LANES), jnp.float32)
    factors = pltpu.VMEM((2, G, HGRN_BLOCK.bit_length() - 1, HGRN_BLOCK, nb, LANES), jnp.float32)
    return pl.pallas_call(
        _hgrn_kernel,
        grid=(B, nG, nT + 1),
        in_specs=in_specs,
        out_specs=out_specs,
        out_shape=[out_sd, out_sd],
        scratch_shapes=[pltpu.VMEM((2, G, LANES, LANES), jnp.float32),
                        tile_buf, tile_buf, tile_buf, tile_buf, dec_buf,
                        tile_buf, tile_buf, tile_buf, tile_buf, dec_buf,
                        slabs, slabs, slabs, factors, factors],
        compiler_params=pltpu.CompilerParams(
            dimension_semantics=("parallel", "parallel", "arbitrary"),
            vmem_limit_bytes=48 * 1024 * 1024),
        name="hgrn2_bidir",
    )(*operands)


def hgrn2_mixer(x, gain, layer, w_in, lb_table, norm_g, w_out):
    B, S, _ = x.shape
    H, dF, dI = HGRN_HEADS, HGRN_DF, HGRN_DI
    n_qv = HGRN_F + H * dI
    proj = fused_matmul(x, jnp.concatenate([w_in[:, :n_qv], w_in[:, n_qv + H * dI:]], axis=1), gain=gain)
    gate = fused_matmul(x, w_in[:, n_qv:n_qv + H * dI], gain=gain)
    lb_w = jax.nn.softmax(lb_table.astype(jnp.float32), axis=1)
    lb = (jnp.cumsum(lb_w, axis=1) - lb_w[:, :1])[:, layer]
    log_lb = jnp.stack([jnp.log(lb), jnp.log1p(-lb)], axis=1)
    o_fwd, o_bwd = hgrn2_bidir(proj, log_lb)
    return gated_out_proj(o_fwd, o_bwd, gate, norm_g, w_out, x)


def shortconv_mixer(x, gain, w_in, conv_w, w_out):
    D = x.shape[-1]
    gb, gc, u = (fused_matmul(x, w_in[:, j * D:(j + 1) * D], gain=gain) for j in range(3))
    return fused_matmul(shortconv_gate(gb, gc, u, conv_w), w_out, residual=x)


def rope(x, pos):
    half = x.shape[-1] // 2
    inv = ROPE_THETA ** (-jnp.arange(half, dtype=jnp.float32) / half)
    ang = pos.astype(jnp.float32)[:, :, None, None] * inv
    cos, sin = jnp.cos(ang), jnp.sin(ang)
    xf = x.astype(jnp.float32)
    x1, x2 = xf[..., :half], xf[..., half:]
    return jnp.concatenate([x1 * cos - x2 * sin, x1 * sin + x2 * cos], axis=-1).astype(x.dtype)


MLA_QK_PAD = 128
MLA_TQ = 512
MLA_TK = 2048


def _mla_attn_kernel(q_ref, k_ref, v_ref, o_ref, *, tk):
    tq = q_ref.shape[1]
    nk = k_ref.shape[1] // tk
    head_lanes = [slice(hh * MLA_QK_PAD, (hh + 1) * MLA_QK_PAD) for hh in range(2)]
    qs = [q_ref[0, :, lanes] for lanes in head_lanes]

    def body(j, carry):
        rows = pl.ds(pl.multiple_of(j * tk, tk), tk)
        v = v_ref[0, rows, :]
        new = []
        for hh in range(2):
            m, l, acc = carry[hh]
            k = k_ref[0, rows, head_lanes[hh]]
            s = lax.dot_general(qs[hh], k, (((1,), (1,)), ((), ())), preferred_element_type=jnp.float32)
            m_new = jnp.maximum(m, jnp.max(s, axis=-1, keepdims=True))
            p = jnp.exp2(s - m_new)
            alpha = jnp.exp2(m - m_new)
            l = alpha * l + jnp.sum(p, axis=-1, keepdims=True)
            acc = alpha * acc + jnp.dot(p.astype(jnp.bfloat16), v, preferred_element_type=jnp.float32)
            new.append((m_new, l, acc))
        return tuple(new)

    init = (jnp.full((tq, 1), -jnp.inf, jnp.float32), jnp.zeros((tq, 1), jnp.float32),
            jnp.zeros((tq, 2 * MLA_V), jnp.float32))
    (_, l0, acc0), (_, l1, acc1) = lax.fori_loop(0, nk, body, (init, init))
    lane = lax.broadcasted_iota(jnp.int32, (tq, 2 * MLA_V), 1)
    o_ref[0] = jnp.where(lane < MLA_V, acc0 / l0, acc1 / l1).astype(o_ref.dtype)


def mla_attention(qf, kf, vf):
    B, S, _ = qf.shape
    H = MLA_HEADS
    tq, tk = min(MLA_TQ, S), min(MLA_TK, S)
    return pl.pallas_call(
        functools.partial(_mla_attn_kernel, tk=tk),
        grid=(B, H // 2, S // tq),
        in_specs=[pl.BlockSpec((1, tq, 2 * MLA_QK_PAD), lambda b, h, i: (b, i, h)),
                  pl.BlockSpec((1, S, 2 * MLA_QK_PAD), lambda b, h, i: (b, 0, h)),
                  pl.BlockSpec((1, S, 2 * MLA_V), lambda b, h, i: (b, 0, h))],
        out_specs=pl.BlockSpec((1, tq, 2 * MLA_V), lambda b, h, i: (b, i, h)),
        out_shape=jax.ShapeDtypeStruct((B, S, H * MLA_V), jnp.float32),
        compiler_params=pltpu.CompilerParams(
            dimension_semantics=("parallel", "parallel", "arbitrary"),
            vmem_limit_bytes=48 * 1024 * 1024),
        name="mla_attention",
    )(qf, kf, vf)


PACK_TM = 512


def _pack_q_kernel(q_ref, c_ref, sp_ref, sm_ref, o_ref, *, scale, half):
    q = q_ref[...]
    n_heads = q.shape[-1] // LANES
    wide = lambda ref: jnp.concatenate([ref[...]] * n_heads, axis=-1)
    up = pltpu.roll(q, q.shape[-1] - half, axis=1)
    down = pltpu.roll(q, half, axis=1)
    o_ref[...] = ((q * wide(c_ref) + up * wide(sp_ref) + down * wide(sm_ref)) * scale).astype(o_ref.dtype)


def _pack_k_kernel(k_ref, kr_ref, o_ref):
    n_heads = k_ref.shape[-1] // LANES
    o_ref[...] = (k_ref[...] + jnp.concatenate([kr_ref[...]] * n_heads, axis=-1)).astype(o_ref.dtype)


def _pack_call(kernel_fn, wide, narrow, name):
    M, N = wide.shape
    tm = min(PACK_TM, M)
    return pl.pallas_call(
        kernel_fn,
        grid=(M // tm,),
        in_specs=[pl.BlockSpec((tm, N), lambda m: (m, 0))] + [pl.BlockSpec((tm, LANES), lambda m: (m, 0))] * len(narrow),
        out_specs=pl.BlockSpec((tm, N), lambda m: (m, 0)),
        out_shape=jax.ShapeDtypeStruct((M, N), jnp.bfloat16),
        compiler_params=pltpu.CompilerParams(dimension_semantics=("parallel",)),
        name=name,
    )(wide, *narrow)


def mla_mixer(x, gain, pos, w_in, q_norm, w_uq, kv_norm, w_ukv, w_o):
    B, S, _ = x.shape
    H, M = MLA_HEADS, B * S
    n_q, n_kv, half = MLA_Q_LORA, MLA_KV_LORA, MLA_ROPE // 2
    pad = MLA_QK_PAD - MLA_NOPE - MLA_ROPE
    f32 = jnp.float32
    cq = fused_matmul(x, w_in[:, :n_q], gain=gain)
    ckv = fused_matmul(x, w_in[:, n_q:n_q + n_kv], gain=gain)
    kr = fused_matmul(x, w_in[:, n_q + n_kv:], gain=gain)
    w_q = jnp.pad(w_uq.reshape(n_q, H, MLA_NOPE + MLA_ROPE), ((0, 0), (0, 0), (0, pad))).reshape(n_q, H * MLA_QK_PAD)
    w_kv = w_ukv.reshape(n_kv, H, MLA_NOPE + MLA_V)
    w_k = jnp.pad(w_kv[..., :MLA_NOPE], ((0, 0), (0, 0), (0, MLA_QK_PAD - MLA_NOPE))).reshape(n_kv, H * MLA_QK_PAD)
    w_v = w_kv[..., MLA_NOPE:].reshape(n_kv, H * MLA_V)
    q_pad = fused_matmul(cq, w_q, gain=q_norm).reshape(M, H * MLA_QK_PAD)
    k_pad = fused_matmul(ckv, w_k, gain=kv_norm).reshape(M, H * MLA_QK_PAD)
    vf = fused_matmul(ckv, w_v, gain=kv_norm, out_dtype=jnp.bfloat16)
    inv = ROPE_THETA ** (-jnp.arange(half, dtype=f32) / half)
    ang = pos.astype(f32).reshape(M, 1) * inv
    cos, sin = jnp.cos(ang), jnp.sin(ang)
    zeros = lambda n: jnp.zeros((M, n), f32)
    c_tab = jnp.concatenate([jnp.ones((M, MLA_NOPE), f32), cos, cos, zeros(pad)], axis=-1)
    sp_tab = jnp.concatenate([zeros(MLA_NOPE), -sin, zeros(half), zeros(pad)], axis=-1)
    sm_tab = jnp.concatenate([zeros(MLA_NOPE), zeros(half), sin, zeros(pad)], axis=-1)
    scale = (MLA_NOPE + MLA_ROPE) ** -0.5 * math.log2(math.e)
    qf = _pack_call(functools.partial(_pack_q_kernel, scale=scale, half=half), q_pad, [c_tab, sp_tab, sm_tab], "pack_q")
    kr_tile = jnp.concatenate([zeros(MLA_NOPE), rope(kr[:, :, None, :], pos).reshape(M, MLA_ROPE), zeros(pad)], axis=-1)
    kf = _pack_call(_pack_k_kernel, k_pad, [kr_tile], "pack_k")
    o = mla_attention(qf.reshape(B, S, -1), kf.reshape(B, S, -1), vf)
    return fused_matmul(o, w_o, residual=x)


ROUTE_TM = 1024


def _route_kernel(x_ref, g_ref, wr_ref, h_ref, aff_ref):
    x = x_ref[...]
    h = (x * lax.rsqrt(jnp.mean(x * x, axis=-1, keepdims=True) + EPS) * g_ref[...]).astype(jnp.bfloat16)
    h_ref[...] = h
    logits = jnp.dot(h, wr_ref[...].astype(jnp.bfloat16), preferred_element_type=jnp.float32)
    e = jnp.exp(logits - jnp.max(logits, axis=-1, keepdims=True))
    aff_ref[...] = e / jnp.sum(e, axis=-1, keepdims=True)


def route(x, gain, w_router):
    B, S, D = x.shape
    E = w_router.shape[-1]
    M = B * S
    tm = min(ROUTE_TM, M)
    h, aff = pl.pallas_call(
        _route_kernel,
        grid=(M // tm,),
        in_specs=[pl.BlockSpec((tm, D), lambda i: (i, 0)),
                  pl.BlockSpec((1, D), lambda i: (0, 0)),
                  pl.BlockSpec((D, E), lambda i: (0, 0))],
        out_specs=[pl.BlockSpec((tm, D), lambda i: (i, 0)),
                   pl.BlockSpec((tm, E), lambda i: (i, 0))],
        out_shape=[jax.ShapeDtypeStruct((M, D), jnp.bfloat16), jax.ShapeDtypeStruct((M, E), jnp.float32)],
        compiler_params=pltpu.CompilerParams(dimension_semantics=("parallel",)),
        name="route",
    )(x.reshape(M, D), gain.reshape(1, D).astype(jnp.float32), w_router)
    return h.reshape(B, S, D), aff.reshape(B, S, E)


def expert_choice_ffn(x, gain, w_router, w_gate, w_up, w_down):
    B, S, _ = x.shape
    cap = CAPACITY_FACTOR * S // N_EXPERTS
    h, aff = route(x, gain, w_router)
    gate, idx = lax.top_k(jnp.swapaxes(aff, 1, 2), cap)
    bi = jnp.arange(B)[:, None, None]
    ys = expert_ffn(h[bi, idx], gate, w_gate, w_up, w_down)
    return x.at[bi, idx].add(ys)


FFN_BATCH_ROWS = 2
FFN_TF = 512


def _expert_ffn_kernel(x_ref, g_ref, wg_ref, wu_ref, wd_ref, o_ref):
    f = pl.program_id(2)
    bb, _, cap, d = x_ref.shape
    x = x_ref[...].reshape(bb * cap, d)
    bf16 = jnp.bfloat16
    a = jnp.dot(x, wg_ref[0].astype(bf16), preferred_element_type=jnp.float32)
    u = jnp.dot(x, wu_ref[0].astype(bf16), preferred_element_type=jnp.float32)
    hid = (a * jax.nn.sigmoid(a) * u).astype(bf16)
    y = jnp.dot(hid, wd_ref[0].astype(bf16), preferred_element_type=jnp.float32).reshape(bb, 1, cap, d)

    @pl.when(f == 0)
    def _():
        o_ref[...] = y

    @pl.when(f > 0)
    def _():
        o_ref[...] += y

    @pl.when(f == pl.num_programs(2) - 1)
    def _():
        o_ref[...] = o_ref[...] * g_ref[...]


def expert_ffn(xs, gate, w_gate, w_up, w_down):
    B, E, cap, D = xs.shape
    F = w_gate.shape[-1]
    bb = min(FFN_BATCH_ROWS, B)
    tf = min(FFN_TF, F)
    return pl.pallas_call(
        _expert_ffn_kernel,
        grid=(E, B // bb, F // tf),
        in_specs=[pl.BlockSpec((bb, 1, cap, D), lambda e, b, f: (b, e, 0, 0)),
                  pl.BlockSpec((bb, 1, cap, 1), lambda e, b, f: (b, e, 0, 0)),
                  pl.BlockSpec((1, D, tf), lambda e, b, f: (e, 0, f)),
                  pl.BlockSpec((1, D, tf), lambda e, b, f: (e, 0, f)),
                  pl.BlockSpec((1, tf, D), lambda e, b, f: (e, f, 0))],
        out_specs=pl.BlockSpec((bb, 1, cap, D), lambda e, b, f: (b, e, 0, 0)),
        out_shape=jax.ShapeDtypeStruct((B, E, cap, D), jnp.float32),
        compiler_params=pltpu.CompilerParams(
            dimension_semantics=("parallel", "parallel", "arbitrary"),
            vmem_limit_bytes=56 * 1024 * 1024),
        name="expert_ffn",
    )(xs.astype(jnp.bfloat16), gate[..., None].astype(jnp.float32), w_gate, w_up, w_down)


PROJ_TM = 1024
PROJ_TN_CHOICES = (1024, 768, 512, 384, 256, 128)


def _proj_kernel(*refs, normed, with_residual):
    refs = list(refs)
    x = refs.pop(0)[...]
    if normed:
        g = refs.pop(0)[...]
        x = x * lax.rsqrt(jnp.mean(x * x, axis=-1, keepdims=True) + EPS) * g
    w = refs.pop(0)[...]
    y = jnp.dot(x.astype(jnp.bfloat16), w.astype(jnp.bfloat16), preferred_element_type=jnp.float32)
    if with_residual:
        y = y + refs.pop(0)[...]
    o_ref = refs.pop(0)
    o_ref[...] = y.astype(o_ref.dtype)


def fused_matmul(x, w, gain=None, residual=None, out_dtype=jnp.float32):
    lead, K = x.shape[:-1], x.shape[-1]
    N = w.shape[-1]
    M = math.prod(lead)
    tm = min(PROJ_TM, M)
    tn = next((t for t in PROJ_TN_CHOICES if N % t == 0), N)
    operands = [x.reshape(M, K)]
    in_specs = [pl.BlockSpec((tm, K), lambda n, m: (m, 0))]
    if gain is not None:
        operands.append(gain.reshape(1, K).astype(jnp.float32))
        in_specs.append(pl.BlockSpec((1, K), lambda n, m: (0, 0)))
    operands.append(w)
    in_specs.append(pl.BlockSpec((K, tn), lambda n, m: (0, n)))
    if residual is not None:
        operands.append(residual.reshape(M, N))
        in_specs.append(pl.BlockSpec((tm, tn), lambda n, m: (m, n)))
    out = pl.pallas_call(
        functools.partial(_proj_kernel, normed=gain is not None, with_residual=residual is not None),
        grid=(N // tn, M // tm),
        in_specs=in_specs,
        out_specs=pl.BlockSpec((tm, tn), lambda n, m: (m, n)),
        out_shape=jax.ShapeDtypeStruct((M, N), out_dtype),
        compiler_params=pltpu.CompilerParams(
            dimension_semantics=("parallel", "parallel"),
            vmem_limit_bytes=48 * 1024 * 1024),
        name="fused_matmul",
    )(*operands)
    return out.reshape(*lead, N)


OUT_TM = 512


def _gated_out_kernel(of_ref, ob_ref, z_ref, g_ref, w_ref, res_ref, o_ref):
    o = of_ref[...] + ob_ref[...]
    z = z_ref[...]
    g = g_ref[...]
    heads = []
    for h in range(o.shape[-1] // LANES):
        oh = o[:, h * LANES:(h + 1) * LANES]
        heads.append(oh * lax.rsqrt(jnp.mean(oh * oh, axis=-1, keepdims=True) + EPS) * g)
    y = jnp.concatenate(heads, axis=-1) * (z * jax.nn.sigmoid(z))
    o_ref[...] = res_ref[...] + jnp.dot(y.astype(jnp.bfloat16), w_ref[...].astype(jnp.bfloat16),
                                        preferred_element_type=jnp.float32)


def gated_out_proj(o_fwd, o_bwd, z, norm_g, w_out, residual):
    lead, K = o_fwd.shape[:-1], o_fwd.shape[-1]
    N = w_out.shape[-1]
    M = math.prod(lead)
    tm = min(OUT_TM, M)
    row = lambda width: pl.BlockSpec((tm, width), lambda m: (m, 0))
    out = pl.pallas_call(
        _gated_out_kernel,
        grid=(M // tm,),
        in_specs=[row(K), row(K), row(K),
                  pl.BlockSpec((1, LANES), lambda m: (0, 0)),
                  pl.BlockSpec((K, N), lambda m: (0, 0)),
                  row(N)],
        out_specs=row(N),
        out_shape=jax.ShapeDtypeStruct((M, N), jnp.float32),
        compiler_params=pltpu.CompilerParams(
            dimension_semantics=("parallel",),
            vmem_limit_bytes=48 * 1024 * 1024),
        name="gated_out_proj",
    )(o_fwd.reshape(M, K), o_bwd.reshape(M, K), z.reshape(M, K),
      norm_g.reshape(1, LANES).astype(jnp.float32), w_out, residual.reshape(M, N))
    return out.reshape(*lead, N)


def _final_norm_kernel(x_ref, g_ref, o_ref):
    x = x_ref[...]
    y = x * lax.rsqrt(jnp.mean(x * x, axis=-1, keepdims=True) + EPS)
    o_ref[...] = y * g_ref[...]


def final_norm(x, g):
    B, S, D = x.shape
    x2 = x.reshape(B * S, D)
    tm = 1024
    out = pl.pallas_call(
        _final_norm_kernel,
        grid=(B * S // tm,),
        in_specs=[pl.BlockSpec((tm, D), lambda i: (i, 0)),
                  pl.BlockSpec((1, D), lambda i: (0, 0))],
        out_specs=pl.BlockSpec((tm, D), lambda i: (i, 0)),
        out_shape=jax.ShapeDtypeStruct((B * S, D), x.dtype),
        name="final_norm",
    )(x2, g.reshape(1, D))
    return out.reshape(B, S, D)


def kernel(x, positions, norm_mix, norm_ffn, norm_final,
           gdn_w_in, gdn_conv, gdn_A_log, gdn_dt_bias, gdn_norm, gdn_w_out,
           hgrn_w_in, hgrn_lb, hgrn_norm, hgrn_w_out,
           sc_w_in, sc_conv, sc_w_out,
           mla_w_in, mla_q_norm, mla_w_uq, mla_kv_norm, mla_w_ukv, mla_w_o,
           moe_router, moe_w_gate, moe_w_up, moe_w_down):
    for i in range(DEPTH):
        m, j = i % N_MIXERS, i // N_MIXERS
        g = norm_mix[i]
        if m == 0:
            x = gdn_mixer(x, g, gdn_w_in[j], gdn_conv[j], gdn_A_log[j], gdn_dt_bias[j], gdn_norm[j], gdn_w_out[j])
        elif m == 1:
            x = hgrn2_mixer(x, g, i, hgrn_w_in[j], hgrn_lb, hgrn_norm[j], hgrn_w_out[j])
        elif m == 2:
            x = shortconv_mixer(x, g, sc_w_in[j], sc_conv[j], sc_w_out[j])
        else:
            x = mla_mixer(x, g, positions, mla_w_in[j], mla_q_norm[j], mla_w_uq[j],
                          mla_kv_norm[j], mla_w_ukv[j], mla_w_o[j])
        x = expert_choice_ffn(x, norm_ffn[i], moe_router[i], moe_w_gate[i], moe_w_up[i], moe_w_down[i])
    return final_norm(x, norm_final)
```

```python
import functools
import math

import jax
import jax.numpy as jnp
from jax import lax
from jax.experimental import pallas as pl
from jax.experimental.pallas import tpu as pltpu

D_MODEL = 1024
BATCH = 4
SEQ = 8192
DEPTH = 4
N_MIXERS = 4
EPS = 1e-6
GDN_HEADS = 8
GDN_DK = 128
GDN_DV = 128
GDN_CONV = 5
HGRN_EXPAND = 128
HGRN_HEADS = D_MODEL // HGRN_EXPAND
HGRN_DF = HGRN_EXPAND
HGRN_DI = D_MODEL // HGRN_HEADS
HGRN_F = HGRN_HEADS * HGRN_DF
SC_WIDTH = 3
MLA_HEADS = 16
MLA_NOPE = 64
MLA_ROPE = 32
MLA_V = 64
MLA_Q_LORA = 384
MLA_KV_LORA = 256
ROPE_THETA = 10000.0
N_EXPERTS = 16
D_EXPERT = 2048
CAPACITY_FACTOR = 2


def rms_norm(x, g):
    xf = x.astype(jnp.float32)
    y = xf * lax.rsqrt(jnp.mean(xf * xf, axis=-1, keepdims=True) + EPS)
    return (y * g.astype(jnp.float32)).astype(x.dtype)


def l2_normalize(x):
    xf = x.astype(jnp.float32)
    return xf * lax.rsqrt(jnp.sum(xf * xf, axis=-1, keepdims=True) + EPS)


def centred_depthwise_conv(x, w):
    K, C = w.shape
    return lax.conv_general_dilated(
        x, w[:, None, :], window_strides=(1,), padding=[(K // 2, K // 2)],
        dimension_numbers=('NWC', 'WIO', 'NWC'), feature_group_count=C)


LANES = 128
SUBLANES = 8
CONV_TILE = 256
CONV_CHANNELS = 1024


def _row_conv(prev, x, nxt, w):
    T, K = x.shape[0], w.shape[0]
    ext = jnp.concatenate([prev, x, nxt], axis=0)
    acc = None
    for j in range(K):
        start = SUBLANES - K // 2 + j
        term = ext[start:start + T] * w[j:j + 1, :]
        acc = term if acc is None else acc + term
    return acc


def _halo_specs(S, T, ct):
    r = T // SUBLANES
    return [pl.BlockSpec((1, SUBLANES, ct), lambda b, t, c: (b, jnp.maximum(t * r - 1, 0), c)),
            pl.BlockSpec((1, T, ct), lambda b, t, c: (b, t, c)),
            pl.BlockSpec((1, SUBLANES, ct), lambda b, t, c: (b, jnp.minimum((t + 1) * r, S // SUBLANES - 1), c))]


def _gdn_prep_kernel(xp_ref, x_ref, xn_ref, w_ref, o_ref, *, n_norm_chunks):
    t, c = pl.program_id(1), pl.program_id(2)
    prev = jnp.where(t == 0, 0.0, xp_ref[0])
    nxt = jnp.where(t == pl.num_programs(1) - 1, 0.0, xn_ref[0])
    y = _row_conv(prev, x_ref[0], nxt, w_ref[...])
    y = y * jax.nn.sigmoid(y)
    heads = []
    for h in range(y.shape[-1] // LANES):
        yh = y[:, h * LANES:(h + 1) * LANES]
        heads.append(yh * lax.rsqrt(jnp.sum(yh * yh, axis=-1, keepdims=True) + EPS))
    o_ref[0] = jnp.where(c < n_norm_chunks, jnp.concatenate(heads, axis=-1), y)


def gdn_prep(qkv_raw, conv_w, n_norm):
    B, S, C = qkv_raw.shape
    T, ct = min(CONV_TILE, S), min(CONV_CHANNELS, C)
    return pl.pallas_call(
        functools.partial(_gdn_prep_kernel, n_norm_chunks=n_norm // ct),
        grid=(B, S // T, C // ct),
        in_specs=_halo_specs(S, T, ct) + [pl.BlockSpec((conv_w.shape[0], ct), lambda b, t, c: (0, c))],
        out_specs=pl.BlockSpec((1, T, ct), lambda b, t, c: (b, t, c)),
        out_shape=jax.ShapeDtypeStruct((B, S, C), jnp.float32),
        compiler_params=pltpu.CompilerParams(dimension_semantics=("parallel", "parallel", "parallel")),
        name="gdn_prep",
    )(qkv_raw, qkv_raw, qkv_raw, conv_w)


def _shortconv_gate_kernel(gb_ref, cp_ref, c_ref, cn_ref, up_ref, u_ref, un_ref, w_ref, o_ref):
    t = pl.program_id(1)
    first, last = t == 0, t == pl.num_programs(1) - 1
    prev = jnp.where(first, 0.0, cp_ref[0] * up_ref[0])
    nxt = jnp.where(last, 0.0, cn_ref[0] * un_ref[0])
    o_ref[0] = gb_ref[0] * _row_conv(prev, c_ref[0] * u_ref[0], nxt, w_ref[...])


def shortconv_gate(gb, gc, u, conv_w):
    B, S, C = gb.shape
    T, ct = min(CONV_TILE, S), min(CONV_CHANNELS, C)
    halo = _halo_specs(S, T, ct)
    return pl.pallas_call(
        _shortconv_gate_kernel,
        grid=(B, S // T, C // ct),
        in_specs=[halo[1]] + halo + halo + [pl.BlockSpec((conv_w.shape[0], ct), lambda b, t, c: (0, c))],
        out_specs=pl.BlockSpec((1, T, ct), lambda b, t, c: (b, t, c)),
        out_shape=jax.ShapeDtypeStruct((B, S, C), jnp.float32),
        compiler_params=pltpu.CompilerParams(dimension_semantics=("parallel", "parallel", "parallel")),
        name="shortconv_gate",
    )(gb, gc, gc, gc, u, u, u, conv_w)


GDN_BLOCK = 16
GDN_TILE = 256
GDN_GROUP = 4


def _gdn_kernel(*refs):
    C, G, H = GDN_BLOCK, GDN_GROUP, GDN_HEADS
    per_dir = 3 * G + 1
    in_refs, out_refs = refs[:2 * per_dir], refs[2 * per_dir:2 * per_dir + 2]
    scratch = refs[2 * per_dir + 2:]
    state_ref, us_ref, ws_ref, gc_ref, ks_ref = scratch[0], *scratch[13:17]
    u_ref, w_ref, qe_ref, kd_ref, a_ref, dec_ref = scratch[1:7]
    u_rd, w_rd, qe_rd, kd_rd, a_rd, dec_rd = scratch[7:13]
    handoff = tuple(zip(scratch[1:7], scratch[7:13]))
    T = in_refs[0].shape[1]
    nb = T // C
    bf16 = jnp.bfloat16
    scale = GDN_DK ** -0.5
    hg = pl.program_id(1)

    @pl.when(pl.program_id(2) == 0)
    def _():
        state_ref[...] = jnp.zeros_like(state_ref)
        for _, dst in handoff:
            dst[...] = jnp.zeros_like(dst)

    slab = lambda r: pl.ds(r, nb, stride=C)
    ones = jnp.ones((LANES, LANES), bf16)
    lane = lax.broadcasted_iota(jnp.int32, (nb, LANES), 1)
    rowsum = lambda x: jnp.broadcast_to(jnp.sum(x, axis=-1, keepdims=True), (nb, LANES))

    def pick(x, idx):
        return rowsum(jnp.where(lane == idx, x, 0.0))

    orders = (list(range(C)), list(range(C - 1, -1, -1)))
    chains = [(d, g) for d in range(2) for g in range(G)]
    sc_refs = [in_refs[d * per_dir + 3 * G] for d in range(2)]
    qkv_refs = {(d, g): [in_refs[d * per_dir + j * G + g] for j in range(3)] for d, g in chains}

    for d, g in chains:
        run = None
        for r in orders[d]:
            g_r = pick(sc_refs[d][0, slab(r), :], d * H + hg * G + g)
            run = g_r if run is None else run + g_r
            gc_ref[d, g, r] = run
            ks_ref[d, g, r] = qkv_refs[d, g][1][0, slab(r), :]
        dec_ref[d, g] = jnp.exp(run)

    def block_phase(n):
        dots = {}
        for d, g in chains:
            order = orders[d]
            q_ref, k_ref, v_ref = qkv_refs[d, g]
            q_r = q_ref[0, slab(order[n]), :] * scale
            k_r = ks_ref[d, g, order[n]]
            k_prev = [ks_ref[d, g, r2] for r2 in order[:n]]
            prods = [q_r * k_r] + [q_r * k_2 for k_2 in k_prev] + [k_r * k_2 for k_2 in k_prev]
            dots[d, g] = jnp.dot(jnp.concatenate(prods, axis=0).astype(bf16), ones,
                                 preferred_element_type=jnp.float32)
        for d, g in chains:
            order = orders[d]
            r = order[n]
            q_ref, k_ref, v_ref = qkv_refs[d, g]
            gc, us, ws = gc_ref.at[d, g], us_ref.at[d, g], ws_ref.at[d, g]
            dt = dots[d, g]
            q_r = q_ref[0, slab(r), :] * scale
            k_r = ks_ref[d, g, r]
            beta_r = pick(sc_refs[d][0, slab(r), :], 2 * H + d * H + hg * G + g)
            gc_r = gc[r]
            eg_r = jnp.exp(gc_r)
            u_r = beta_r * v_ref[0, slab(r), :]
            w_r = beta_r * eg_r * k_r
            a_row = jnp.where(lane == r, dt[:nb], 0.0)
            for m, r2 in enumerate(order[:n]):
                decay = jnp.exp(gc_r - gc[r2])
                l_rr = beta_r * decay * dt[(1 + n + m) * nb:(2 + n + m) * nb]
                u_r = u_r - l_rr * us[r2]
                w_r = w_r - l_rr * ws[r2]
                a_row = jnp.where(lane == r2, decay * dt[(1 + m) * nb:(2 + m) * nb], a_row)
            us[r] = u_r
            ws[r] = w_r
            u_ref[d, g, slab(r), :] = u_r
            w_ref[d, g, slab(r), :] = w_r
            a_ref[d, g, slab(r), :] = a_row
            qe_ref[d, g, slab(r), :] = q_r * eg_r
            kd_ref[d, g, slab(r), :] = k_r * jnp.exp(gc[order[-1]] - gc_r)

    def state_phase(i):
        blks = [i, nb - 1 - i]
        rows = [pl.ds(blk * C, C) for blk in blks]
        r1s = {}
        for d, g in chains:
            wq = jnp.concatenate([w_rd[d, g, rows[d], :], qe_rd[d, g, rows[d], :]], axis=0).astype(bf16)
            r1s[d, g] = jnp.dot(wq, state_ref[d, g].astype(bf16), preferred_element_type=jnp.float32)
        v_news = {c: (u_rd[c[0], c[1], rows[c[0]], :] - r1s[c][:C]).astype(bf16) for c in chains}
        intra, upd = {}, {}
        for d, g in chains:
            a_blk = a_rd[d, g, rows[d], :][:, :C].astype(bf16)
            intra[d, g] = jnp.dot(a_blk, v_news[d, g], preferred_element_type=jnp.float32)
            upd[d, g] = lax.dot_general(kd_rd[d, g, rows[d], :].astype(bf16), v_news[d, g],
                                        (((0,), (0,)), ((), ())), preferred_element_type=jnp.float32)
        for d, g in chains:
            out_refs[d][0, rows[d], g * LANES:(g + 1) * LANES] = r1s[d, g][C:] + intra[d, g]
            state_ref[d, g] = state_ref[d, g] * dec_rd[d, g, pl.ds(blks[d], 1), :] + upd[d, g]

    for n in range(max(C, nb)):
        if n < nb:
            state_phase(n)
        if n < C:
            block_phase(n)
    for src, dst in handoff:
        dst[...] = src[...]


def gdn_bidir(qkv, scal):
    B, S, _ = qkv.shape
    H, G = GDN_HEADS, GDN_GROUP
    T = min(GDN_TILE, S)
    nT, nG = S // T, H // G
    nb = T // GDN_BLOCK
    fwd_in = lambda t: jnp.minimum(t, nT - 1)
    bwd_in = lambda t: jnp.maximum(nT - 1 - t, 0)
    fwd_out = lambda t: jnp.maximum(t - 1, 0)
    bwd_out = lambda t: jnp.minimum(nT - t, nT - 1)
    blk = lambda tile, off: pl.BlockSpec((1, T, LANES), lambda b, hg, t: (b, tile(t), off + hg * G))
    in_specs, operands = [], []
    for tile in (fwd_in, bwd_in):
        for j in range(3):
            for g in range(G):
                in_specs.append(blk(tile, j * H + g))
                operands.append(qkv)
        in_specs.append(pl.BlockSpec((1, T, LANES), lambda b, hg, t, tile=tile: (b, tile(t), 0)))
        operands.append(scal)
    out_specs = [pl.BlockSpec((1, T, G * LANES), lambda b, hg, t: (b, fwd_out(t), hg)),
                 pl.BlockSpec((1, T, G * LANES), lambda b, hg, t: (b, bwd_out(t), hg))]
    out_sd = jax.ShapeDtypeStruct((B, S, H * LANES), jnp.float32)
    tile_buf = pltpu.VMEM((2, G, T, LANES), jnp.float32)
    dec_buf = pltpu.VMEM((2, G, nb, LANES), jnp.float32)
    slabs = pltpu.VMEM((2, G, GDN_BLOCK, nb, LANES), jnp.float32)
    return pl.pallas_call(
        _gdn_kernel,
        grid=(B, nG, nT + 1),
        in_specs=in_specs,
        out_specs=out_specs,
        out_shape=[out_sd, out_sd],
        scratch_shapes=[pltpu.VMEM((2, G, LANES, LANES), jnp.float32),
                        tile_buf, tile_buf, tile_buf, tile_buf, tile_buf, dec_buf,
                        tile_buf, tile_buf, tile_buf, tile_buf, tile_buf, dec_buf,
                        slabs, slabs, slabs, slabs],
        compiler_params=pltpu.CompilerParams(
            dimension_semantics=("parallel", "parallel", "arbitrary"),
            vmem_limit_bytes=48 * 1024 * 1024),
        name="gdn_bidir",
    )(*operands)


def gdn_mixer(x, gain, w_in, conv_w, A_log, dt_bias, norm_g, w_out):
    B, S, _ = x.shape
    H, dk, dv = GDN_HEADS, GDN_DK, GDN_DV
    n_qkv, n_z = 2 * H * dk + H * dv, H * dv
    qkv = fused_matmul(x, w_in[:, :n_qkv], gain=gain)
    z = fused_matmul(x, w_in[:, n_qkv:n_qkv + n_z], gain=gain)
    ab = fused_matmul(x, w_in[:, n_qkv + n_z:], gain=gain)
    a, b = ab[..., :2 * H], ab[..., 2 * H:]
    qkv = gdn_prep(qkv, conv_w, n_norm=2 * H * dk)
    a = a.astype(jnp.float32).reshape(B, S, 2, H)
    b = b.astype(jnp.float32).reshape(B, S, 2, H)
    g = -jnp.exp(A_log.astype(jnp.float32)) * jax.nn.softplus(a + dt_bias.astype(jnp.float32))
    beta = jax.nn.sigmoid(b)
    scal = jnp.concatenate([g.reshape(B, S, 2 * H), beta.reshape(B, S, 2 * H),
                            jnp.zeros((B, S, LANES - 4 * H), jnp.float32)], axis=-1)
    o_fwd, o_bwd = gdn_bidir(qkv, scal)
    return gated_out_proj(o_fwd, o_bwd, z, norm_g, w_out, x)


HGRN_BLOCK = 16
HGRN_TILE = 256
HGRN_GROUP = 4


def _hgrn_kernel(*refs):
    C, G = HGRN_BLOCK, HGRN_GROUP
    per_dir = 3 * G + 1
    in_refs, out_refs = refs[:2 * per_dir], refs[2 * per_dir:2 * per_dir + 2]
    scratch = refs[2 * per_dir + 2:]
    state_ref, b_ref, kk_ref, vs_ref, qx_ref, kx_ref = scratch[0], *scratch[11:16]
    qe_ref, ke_ref, od_ref, vv_ref, dec_ref = scratch[1:6]
    qe_rd, ke_rd, od_rd, vv_rd, dec_rd = scratch[6:11]
    handoff = tuple(zip(scratch[1:6], scratch[6:11]))
    T = in_refs[0].shape[1]
    nb = T // C
    bf16 = jnp.bfloat16
    chains = [(d, g) for d in range(2) for g in range(G)]
    qvl_refs = {(d, g): [in_refs[d * per_dir + j * G + g] for j in range(3)] for d, g in chains}
    lb_refs = [in_refs[d * per_dir + 3 * G] for d in range(2)]

    @pl.when(pl.program_id(2) == 0)
    def _():
        state_ref[...] = jnp.zeros_like(state_ref)
        for _, dst in handoff:
            dst[...] = jnp.zeros_like(dst)

    slab = lambda r: pl.ds(r, nb, stride=C)
    ones = jnp.ones((LANES, LANES), bf16)

    def log_f(x, log_lb, log_1m_lb):
        log_sig = jnp.minimum(x, 0.0) - jnp.log(1.0 + jnp.exp(-jnp.abs(x)))
        a, b = log_lb, log_1m_lb + log_sig
        return jnp.maximum(a, b) + jnp.log(1.0 + jnp.exp(-jnp.abs(a - b)))

    for d, g in chains:
        order = list(range(C)) if d == 0 else list(range(C - 1, -1, -1))
        v_ref, l_ref = qvl_refs[d, g][1:]
        log_lb = lb_refs[d][0:1, g * LANES:(g + 1) * LANES]
        log_1m_lb = lb_refs[d][1:2, g * LANES:(g + 1) * LANES]
        run = None
        for r in order:
            lf = log_f(l_ref[0, slab(r), :], log_lb, log_1m_lb)
            run = lf if run is None else run + lf
            b_ref[d, g, r] = run
            kk_ref[d, g, r] = 1.0 - jnp.exp(lf)
            vs_ref[d, g, r] = v_ref[0, slab(r), :]
        dec_ref[d, g] = jnp.exp(run)

    n_levels = C.bit_length() - 1
    for d, g in chains:
        q_ref = qvl_refs[d, g][0]
        bs, kk = b_ref.at[d, g], kk_ref.at[d, g]
        for r in range(C):
            q_r = q_ref[0, slab(r), :]
            for lv in range(n_levels):
                upper = (r >> lv) & 1 == 1
                mid = (r >> lv) << lv if upper else ((r >> lv) | 1) << lv
                b_mid = bs[mid - 1] if d == 0 else bs[mid]
                if upper == (d == 0):
                    qx_ref[d, g, lv, r] = q_r * jnp.exp(bs[r] - b_mid)
                else:
                    kx_ref[d, g, lv, r] = kk[r] * jnp.exp(b_mid - bs[r])

    def block_phase(r):
        scores, partners = {}, {}
        for d, g in chains:
            q_ref, v_ref, _ = qvl_refs[d, g]
            bs, kk = b_ref.at[d, g], kk_ref.at[d, g]
            q_r = q_ref[0, slab(r), :]
            b_r = bs[r]
            btot = bs[C - 1 if d == 0 else 0]
            qe_ref[d, g, slab(r), :] = q_r * jnp.exp(b_r)
            ke_ref[d, g, slab(r), :] = kk[r] * jnp.exp(btot - b_r)
            vv_ref[d, g, slab(r), :] = vs_ref[d, g, r]
            partners[d, g] = list(range(r + 1) if d == 0 else range(r, C))
            level = lambda r2: (r ^ r2).bit_length() - 1
            terms = [q_r * kk[r] if r2 == r else qx_ref[d, g, level(r2), r] * kx_ref[d, g, level(r2), r2]
                     for r2 in partners[d, g]]
            scores[d, g] = jnp.dot(jnp.concatenate(terms, axis=0).astype(bf16), ones,
                                   preferred_element_type=jnp.float32)
        for d, g in chains:
            acc = None
            for m, r2 in enumerate(partners[d, g]):
                term = scores[d, g][m * nb:(m + 1) * nb] * vs_ref[d, g, r2]
                acc = term if acc is None else acc + term
            od_ref[d, g, slab(r), :] = acc

    def state_phase(i):
        blks = [i, nb - 1 - i]
        rows = [pl.ds(blk * C, C) for blk in blks]
        inter, upd = {}, {}
        for d, g in chains:
            qe = qe_rd[d, g, rows[d], :].astype(bf16)
            inter[d, g] = lax.dot_general(qe, state_ref[d, g].astype(bf16), (((1,), (1,)), ((), ())),
                                          preferred_element_type=jnp.float32)
            upd[d, g] = lax.dot_general(vv_rd[d, g, rows[d], :].astype(bf16), ke_rd[d, g, rows[d], :].astype(bf16),
                                        (((0,), (0,)), ((), ())), preferred_element_type=jnp.float32)
        for d, g in chains:
            out_refs[d][0, rows[d], g * LANES:(g + 1) * LANES] = od_rd[d, g, rows[d], :] + inter[d, g]
            state_ref[d, g] = state_ref[d, g] * dec_rd[d, g, pl.ds(blks[d], 1), :] + upd[d, g]

    for n in range(max(C, nb)):
        if n < nb:
            state_phase(n)
        if n < C:
            block_phase(n)
    for src, dst in handoff:
        dst[...] = src[...]


def hgrn2_bidir(proj, log_lb):
    B, S, _ = proj.shape
    H = HGRN_HEADS
    T = min(HGRN_TILE, S)
    G = HGRN_GROUP
    nT, nG = S // T, H // G
    nb = T // HGRN_BLOCK
    fwd_in = lambda t: jnp.minimum(t, nT - 1)
    bwd_in = lambda t: jnp.maximum(nT - 1 - t, 0)
    fwd_out = lambda t: jnp.maximum(t - 1, 0)
    bwd_out = lambda t: jnp.minimum(nT - t, nT - 1)
    blk = lambda tile, off: pl.BlockSpec((1, T, LANES), lambda b, hg, t: (b, tile(t), off + hg * G))
    in_specs, operands = [], []
    for d, tile in enumerate((fwd_in, bwd_in)):
        for off in (0, H, (2 + d) * H):
            for g in range(G):
                in_specs.append(blk(tile, off + g))
                operands.append(proj)
        in_specs.append(pl.BlockSpec((None, 2, G * LANES), lambda b, hg, t, d=d: (d, 0, hg)))
        operands.append(log_lb)
    out_specs = [pl.BlockSpec((1, T, G * LANES), lambda b, hg, t: (b, fwd_out(t), hg)),
                 pl.BlockSpec((1, T, G * LANES), lambda b, hg, t: (b, bwd_out(t), hg))]
    out_sd = jax.ShapeDtypeStruct((B, S, H * LANES), jnp.float32)
    tile_buf = pltpu.VMEM((2, G, T, LANES), jnp.float32)
    dec_buf = pltpu.VMEM((2, G, nb, LANES), jnp.float32)
    slabs = pltpu.VMEM((2, G, HGRN_BLOCK, nb, LANES), jnp.float32)
    factors = pltpu.VMEM((2, G, HGRN_BLOCK.bit_length() - 1, HGRN_BLOCK, nb, LANES), jnp.float32)
    return pl.pallas_call(
        _hgrn_kernel,
        grid=(B, nG, nT + 1),
        in_specs=in_specs,
        out_specs=out_specs,
        out_shape=[out_sd, out_sd],
        scratch_shapes=[pltpu.VMEM((2, G, LANES, LANES), jnp.float32),
                        tile_buf, tile_buf, tile_buf, tile_buf, dec_buf,
                        tile_buf, tile_buf, tile_buf, tile_buf, dec_buf,
                        slabs, slabs, slabs, factors, factors],
        compiler_params=pltpu.CompilerParams(
            dimension_semantics=("parallel", "parallel", "arbitrary"),
            vmem_limit_bytes=48 * 1024 * 1024),
        name="hgrn2_bidir",
    )(*operands)


def hgrn2_mixer(x, gain, layer, w_in, lb_table, norm_g, w_out):
    B, S, _ = x.shape
    H, dF, dI = HGRN_HEADS, HGRN_DF, HGRN_DI
    n_qv = HGRN_F + H * dI
    proj = fused_matmul(x, jnp.concatenate([w_in[:, :n_qv], w_in[:, n_qv + H * dI:]], axis=1), gain=gain)
    gate = fused_matmul(x, w_in[:, n_qv:n_qv + H * dI], gain=gain)
    lb_w = jax.nn.softmax(lb_table.astype(jnp.float32), axis=1)
    lb = (jnp.cumsum(lb_w, axis=1) - lb_w[:, :1])[:, layer]
    log_lb = jnp.stack([jnp.log(lb), jnp.log1p(-lb)], axis=1)
    o_fwd, o_bwd = hgrn2_bidir(proj, log_lb)
    return gated_out_proj(o_fwd, o_bwd, gate, norm_g, w_out, x)


def shortconv_mixer(x, gain, w_in, conv_w, w_out):
    D = x.shape[-1]
    gb, gc, u = (fused_matmul(x, w_in[:, j * D:(j + 1) * D], gain=gain) for j in range(3))
    return fused_matmul(shortconv_gate(gb, gc, u, conv_w), w_out, residual=x)


def rope(x, pos):
    half = x.shape[-1] // 2
    inv = ROPE_THETA ** (-jnp.arange(half, dtype=jnp.float32) / half)
    ang = pos.astype(jnp.float32)[:, :, None, None] * inv
    cos, sin = jnp.cos(ang), jnp.sin(ang)
    xf = x.astype(jnp.float32)
    x1, x2 = xf[..., :half], xf[..., half:]
    return jnp.concatenate([x1 * cos - x2 * sin, x1 * sin + x2 * cos], axis=-1).astype(x.dtype)


MLA_QK_PAD = 128
MLA_TQ = 512
MLA_TK = 8192


def _mla_attn_kernel(q_ref, k_ref, v_ref, o_ref, *, tk):
    tq = q_ref.shape[1]
    nk = k_ref.shape[1] // tk
    head_lanes = [slice(hh * MLA_QK_PAD, (hh + 1) * MLA_QK_PAD) for hh in range(2)]
    qs = [q_ref[0, :, lanes] for lanes in head_lanes]

    def body(j, carry):
        rows = pl.ds(pl.multiple_of(j * tk, tk), tk)
        v = v_ref[0, rows, :]
        new = []
        for hh in range(2):
            m, l, acc = carry[hh]
            k = k_ref[0, rows, head_lanes[hh]]
            s = lax.dot_general(qs[hh], k, (((1,), (1,)), ((), ())), preferred_element_type=jnp.float32)
            m_new = jnp.maximum(m, jnp.max(s, axis=-1, keepdims=True))
            p = jnp.exp2(s - m_new)
            alpha = jnp.exp2(m - m_new)
            l = alpha * l + jnp.sum(p, axis=-1, keepdims=True)
            acc = alpha * acc + jnp.dot(p.astype(jnp.bfloat16), v, preferred_element_type=jnp.float32)
            new.append((m_new, l, acc))
        return tuple(new)

    init = (jnp.full((tq, 1), -jnp.inf, jnp.float32), jnp.zeros((tq, 1), jnp.float32),
            jnp.zeros((tq, 2 * MLA_V), jnp.float32))
    (_, l0, acc0), (_, l1, acc1) = lax.fori_loop(0, nk, body, (init, init))
    lane = lax.broadcasted_iota(jnp.int32, (tq, 2 * MLA_V), 1)
    o_ref[0] = jnp.where(lane < MLA_V, acc0 / l0, acc1 / l1).astype(o_ref.dtype)


def mla_attention(qf, kf, vf):
    B, S, _ = qf.shape
    H = MLA_HEADS
    tq, tk = min(MLA_TQ, S), min(MLA_TK, S)
    return pl.pallas_call(
        functools.partial(_mla_attn_kernel, tk=tk),
        grid=(B, H // 2, S // tq),
        in_specs=[pl.BlockSpec((1, tq, 2 * MLA_QK_PAD), lambda b, h, i: (b, i, h)),
                  pl.BlockSpec((1, S, 2 * MLA_QK_PAD), lambda b, h, i: (b, 0, h)),
                  pl.BlockSpec((1, S, 2 * MLA_V), lambda b, h, i: (b, 0, h))],
        out_specs=pl.BlockSpec((1, tq, 2 * MLA_V), lambda b, h, i: (b, i, h)),
        out_shape=jax.ShapeDtypeStruct((B, S, H * MLA_V), jnp.float32),
        compiler_params=pltpu.CompilerParams(
            dimension_semantics=("parallel", "parallel", "arbitrary"),
            vmem_limit_bytes=48 * 1024 * 1024),
        name="mla_attention",
    )(qf, kf, vf)


PACK_TM = 512


def _pack_q_kernel(q_ref, c_ref, sp_ref, sm_ref, o_ref, *, scale, half):
    q = q_ref[...]
    n_heads = q.shape[-1] // LANES
    wide = lambda ref: jnp.concatenate([ref[...]] * n_heads, axis=-1)
    up = pltpu.roll(q, q.shape[-1] - half, axis=1)
    down = pltpu.roll(q, half, axis=1)
    o_ref[...] = ((q * wide(c_ref) + up * wide(sp_ref) + down * wide(sm_ref)) * scale).astype(o_ref.dtype)


def _pack_k_kernel(k_ref, kr_ref, o_ref):
    n_heads = k_ref.shape[-1] // LANES
    o_ref[...] = (k_ref[...] + jnp.concatenate([kr_ref[...]] * n_heads, axis=-1)).astype(o_ref.dtype)


def _pack_call(kernel_fn, wide, narrow, name):
    M, N = wide.shape
    tm = min(PACK_TM, M)
    return pl.pallas_call(
        kernel_fn,
        grid=(M // tm,),
        in_specs=[pl.BlockSpec((tm, N), lambda m: (m, 0))] + [pl.BlockSpec((tm, LANES), lambda m: (m, 0))] * len(narrow),
        out_specs=pl.BlockSpec((tm, N), lambda m: (m, 0)),
        out_shape=jax.ShapeDtypeStruct((M, N), jnp.bfloat16),
        compiler_params=pltpu.CompilerParams(dimension_semantics=("parallel",)),
        name=name,
    )(wide, *narrow)


def mla_mixer(x, gain, pos, w_in, q_norm, w_uq, kv_norm, w_ukv, w_o):
    B, S, _ = x.shape
    H, M = MLA_HEADS, B * S
    n_q, n_kv, half = MLA_Q_LORA, MLA_KV_LORA, MLA_ROPE // 2
    pad = MLA_QK_PAD - MLA_NOPE - MLA_ROPE
    f32 = jnp.float32
    cq = fused_matmul(x, w_in[:, :n_q], gain=gain)
    ckv = fused_matmul(x, w_in[:, n_q:n_q + n_kv], gain=gain)
    kr = fused_matmul(x, w_in[:, n_q + n_kv:], gain=gain)
    w_q = jnp.pad(w_uq.reshape(n_q, H, MLA_NOPE + MLA_ROPE), ((0, 0), (0, 0), (0, pad))).reshape(n_q, H * MLA_QK_PAD)
    w_kv = w_ukv.reshape(n_kv, H, MLA_NOPE + MLA_V)
    w_k = jnp.pad(w_kv[..., :MLA_NOPE], ((0, 0), (0, 0), (0, MLA_QK_PAD - MLA_NOPE))).reshape(n_kv, H * MLA_QK_PAD)
    w_v = w_kv[..., MLA_NOPE:].reshape(n_kv, H * MLA_V)
    q_pad = fused_matmul(cq, w_q, gain=q_norm).reshape(M, H * MLA_QK_PAD)
    k_pad = fused_matmul(ckv, w_k, gain=kv_norm).reshape(M, H * MLA_QK_PAD)
    vf = fused_matmul(ckv, w_v, gain=kv_norm, out_dtype=jnp.bfloat16)
    inv = ROPE_THETA ** (-jnp.arange(half, dtype=f32) / half)
    ang = pos.astype(f32).reshape(M, 1) * inv
    cos, sin = jnp.cos(ang), jnp.sin(ang)
    zeros = lambda n: jnp.zeros((M, n), f32)
    c_tab = jnp.concatenate([jnp.ones((M, MLA_NOPE), f32), cos, cos, zeros(pad)], axis=-1)
    sp_tab = jnp.concatenate([zeros(MLA_NOPE), -sin, zeros(half), zeros(pad)], axis=-1)
    sm_tab = jnp.concatenate([zeros(MLA_NOPE), zeros(half), sin, zeros(pad)], axis=-1)
    scale = (MLA_NOPE + MLA_ROPE) ** -0.5 * math.log2(math.e)
    qf = _pack_call(functools.partial(_pack_q_kernel, scale=scale, half=half), q_pad, [c_tab, sp_tab, sm_tab], "pack_q")
    kr_tile = jnp.concatenate([zeros(MLA_NOPE), rope(kr[:, :, None, :], pos).reshape(M, MLA_ROPE), zeros(pad)], axis=-1)
    kf = _pack_call(_pack_k_kernel, k_pad, [kr_tile], "pack_k")
    o = mla_attention(qf.reshape(B, S, -1), kf.reshape(B, S, -1), vf)
    return fused_matmul(o, w_o, residual=x)


ROUTE_TM = 1024


def _route_kernel(x_ref, g_ref, wr_ref, h_ref, aff_ref):
    x = x_ref[...]
    h = (x * lax.rsqrt(jnp.mean(x * x, axis=-1, keepdims=True) + EPS) * g_ref[...]).astype(jnp.bfloat16)
    h_ref[...] = h
    logits = jnp.dot(h, wr_ref[...].astype(jnp.bfloat16), preferred_element_type=jnp.float32)
    e = jnp.exp(logits - jnp.max(logits, axis=-1, keepdims=True))
    aff_ref[...] = e / jnp.sum(e, axis=-1, keepdims=True)


def route(x, gain, w_router):
    B, S, D = x.shape
    E = w_router.shape[-1]
    M = B * S
    tm = min(ROUTE_TM, M)
    h, aff = pl.pallas_call(
        _route_kernel,
        grid=(M // tm,),
        in_specs=[pl.BlockSpec((tm, D), lambda i: (i, 0)),
                  pl.BlockSpec((1, D), lambda i: (0, 0)),
                  pl.BlockSpec((D, E), lambda i: (0, 0))],
        out_specs=[pl.BlockSpec((tm, D), lambda i: (i, 0)),
                   pl.BlockSpec((tm, E), lambda i: (i, 0))],
        out_shape=[jax.ShapeDtypeStruct((M, D), jnp.bfloat16), jax.ShapeDtypeStruct((M, E), jnp.float32)],
        compiler_params=pltpu.CompilerParams(dimension_semantics=("parallel",)),
        name="route",
    )(x.reshape(M, D), gain.reshape(1, D).astype(jnp.float32), w_router)
    return h.reshape(B, S, D), aff.reshape(B, S, E)


def expert_choice_ffn(x, gain, w_router, w_gate, w_up, w_down):
    B, S, _ = x.shape
    cap = CAPACITY_FACTOR * S // N_EXPERTS
    h, aff = route(x, gain, w_router)
    gate, idx = lax.top_k(jnp.swapaxes(aff, 1, 2), cap)
    bi = jnp.arange(B)[:, None, None]
    ys = expert_ffn(h[bi, idx], gate, w_gate, w_up, w_down)
    return x.at[bi, idx].add(ys)


FFN_BATCH_ROWS = 2
FFN_TF = 512


def _expert_ffn_kernel(x_ref, g_ref, wg_ref, wu_ref, wd_ref, o_ref):
    f = pl.program_id(2)
    bb, _, cap, d = x_ref.shape
    x = x_ref[...].reshape(bb * cap, d)
    bf16 = jnp.bfloat16
    a = jnp.dot(x, wg_ref[0].astype(bf16), preferred_element_type=jnp.float32)
    u = jnp.dot(x, wu_ref[0].astype(bf16), preferred_element_type=jnp.float32)
    hid = (a * jax.nn.sigmoid(a) * u).astype(bf16)
    y = jnp.dot(hid, wd_ref[0].astype(bf16), preferred_element_type=jnp.float32).reshape(bb, 1, cap, d)

    @pl.when(f == 0)
    def _():
        o_ref[...] = y

    @pl.when(f > 0)
    def _():
        o_ref[...] += y

    @pl.when(f == pl.num_programs(2) - 1)
    def _():
        o_ref[...] = o_ref[...] * g_ref[...]


def expert_ffn(xs, gate, w_gate, w_up, w_down):
    B, E, cap, D = xs.shape
    F = w_gate.shape[-1]
    bb = min(FFN_BATCH_ROWS, B)
    tf = min(FFN_TF, F)
    return pl.pallas_call(
        _expert_ffn_kernel,
        grid=(E, B // bb, F // tf),
        in_specs=[pl.BlockSpec((bb, 1, cap, D), lambda e, b, f: (b, e, 0, 0)),
                  pl.BlockSpec((bb, 1, cap, 1), lambda e, b, f: (b, e, 0, 0)),
                  pl.BlockSpec((1, D, tf), lambda e, b, f: (e, 0, f)),
                  pl.BlockSpec((1, D, tf), lambda e, b, f: (e, 0, f)),
                  pl.BlockSpec((1, tf, D), lambda e, b, f: (e, f, 0))],
        out_specs=pl.BlockSpec((bb, 1, cap, D), lambda e, b, f: (b, e, 0, 0)),
        out_shape=jax.ShapeDtypeStruct((B, E, cap, D), jnp.float32),
        compiler_params=pltpu.CompilerParams(
            dimension_semantics=("parallel", "parallel", "arbitrary"),
            vmem_limit_bytes=56 * 1024 * 1024),
        name="expert_ffn",
    )(xs.astype(jnp.bfloat16), gate[..., None].astype(jnp.float32), w_gate, w_up, w_down)


PROJ_TM = 1024
PROJ_TN_CHOICES = (1024, 768, 512, 384, 256, 128)


def _proj_kernel(*refs, normed, with_residual):
    refs = list(refs)
    x = refs.pop(0)[...]
    if normed:
        g = refs.pop(0)[...]
        x = x * lax.rsqrt(jnp.mean(x * x, axis=-1, keepdims=True) + EPS) * g
    w = refs.pop(0)[...]
    y = jnp.dot(x.astype(jnp.bfloat16), w.astype(jnp.bfloat16), preferred_element_type=jnp.float32)
    if with_residual:
        y = y + refs.pop(0)[...]
    o_ref = refs.pop(0)
    o_ref[...] = y.astype(o_ref.dtype)


def fused_matmul(x, w, gain=None, residual=None, out_dtype=jnp.float32):
    lead, K = x.shape[:-1], x.shape[-1]
    N = w.shape[-1]
    M = math.prod(lead)
    tm = min(PROJ_TM, M)
    tn = next((t for t in PROJ_TN_CHOICES if N % t == 0), N)
    operands = [x.reshape(M, K)]
    in_specs = [pl.BlockSpec((tm, K), lambda n, m: (m, 0))]
    if gain is not None:
        operands.append(gain.reshape(1, K).astype(jnp.float32))
        in_specs.append(pl.BlockSpec((1, K), lambda n, m: (0, 0)))
    operands.append(w)
    in_specs.append(pl.BlockSpec((K, tn), lambda n, m: (0, n)))
    if residual is not None:
        operands.append(residual.reshape(M, N))
        in_specs.append(pl.BlockSpec((tm, tn), lambda n, m: (m, n)))
    out = pl.pallas_call(
        functools.partial(_proj_kernel, normed=gain is not None, with_residual=residual is not None),
        grid=(N // tn, M // tm),
        in_specs=in_specs,
        out_specs=pl.BlockSpec((tm, tn), lambda n, m: (m, n)),
        out_shape=jax.ShapeDtypeStruct((M, N), out_dtype),
        compiler_params=pltpu.CompilerParams(
            dimension_semantics=("parallel", "parallel"),
            vmem_limit_bytes=48 * 1024 * 1024),
        name="fused_matmul",
    )(*operands)
    return out.reshape(*lead, N)


OUT_TM = 512


def _gated_out_kernel(of_ref, ob_ref, z_ref, g_ref, w_ref, res_ref, o_ref):
    o = of_ref[...] + ob_ref[...]
    z = z_ref[...]
    g = g_ref[...]
    heads = []
    for h in range(o.shape[-1] // LANES):
        oh = o[:, h * LANES:(h + 1) * LANES]
        heads.append(oh * lax.rsqrt(jnp.mean(oh * oh, axis=-1, keepdims=True) + EPS) * g)
    y = jnp.concatenate(heads, axis=-1) * (z * jax.nn.sigmoid(z))
    o_ref[...] = res_ref[...] + jnp.dot(y.astype(jnp.bfloat16), w_ref[...].astype(jnp.bfloat16),
                                        preferred_element_type=jnp.float32)


def gated_out_proj(o_fwd, o_bwd, z, norm_g, w_out, residual):
    lead, K = o_fwd.shape[:-1], o_fwd.shape[-1]
    N = w_out.shape[-1]
    M = math.prod(lead)
    tm = min(OUT_TM, M)
    row = lambda width: pl.BlockSpec((tm, width), lambda m: (m, 0))
    out = pl.pallas_call(
        _gated_out_kernel,
        grid=(M // tm,),
        in_specs=[row(K), row(K), row(K),
                  pl.BlockSpec((1, LANES), lambda m: (0, 0)),
                  pl.BlockSpec((K, N), lambda m: (0, 0)),
                  row(N)],
        out_specs=row(N),
        out_shape=jax.ShapeDtypeStruct((M, N), jnp.float32),
        compiler_params=pltpu.CompilerParams(
            dimension_semantics=("parallel",),
            vmem_limit_bytes=48 * 1024 * 1024),
        name="gated_out_proj",
    )(o_fwd.reshape(M, K), o_bwd.reshape(M, K), z.reshape(M, K),
      norm_g.reshape(1, LANES).astype(jnp.float32), w_out, residual.reshape(M, N))
    return out.reshape(*lead, N)


def _final_norm_kernel(x_ref, g_ref, o_ref):
    x = x_ref[...]
    y = x * lax.rsqrt(jnp.mean(x * x, axis=-1, keepdims=True) + EPS)
    o_ref[...] = y * g_ref[...]


def final_norm(x, g):
    B, S, D = x.shape
    x2 = x.reshape(B * S, D)
    tm = 1024
    out = pl.pallas_call(
        _final_norm_kernel,
        grid=(B * S // tm,),
        in_specs=[pl.BlockSpec((tm, D), lambda i: (i, 0)),
                  pl.BlockSpec((1, D), lambda i: (0, 0))],
        out_specs=pl.BlockSpec((tm, D), lambda i: (i, 0)),
        out_shape=jax.ShapeDtypeStruct((B * S, D), x.dtype),
        name="final_norm",
    )(x2, g.reshape(1, D))
    return out.reshape(B, S, D)


def kernel(x, positions, norm_mix, norm_ffn, norm_final,
           gdn_w_in, gdn_conv, gdn_A_log, gdn_dt_bias, gdn_norm, gdn_w_out,
           hgrn_w_in, hgrn_lb, hgrn_norm, hgrn_w_out,
           sc_w_in, sc_conv, sc_w_out,
           mla_w_in, mla_q_norm, mla_w_uq, mla_kv_norm, mla_w_ukv, mla_w_o,
           moe_router, moe_w_gate, moe_w_up, moe_w_down):
    for i in range(DEPTH):
        m, j = i % N_MIXERS, i // N_MIXERS
        g = norm_mix[i]
        if m == 0:
            x = gdn_mixer(x, g, gdn_w_in[j], gdn_conv[j], gdn_A_log[j], gdn_dt_bias[j], gdn_norm[j], gdn_w_out[j])
        elif m == 1:
            x = hgrn2_mixer(x, g, i, hgrn_w_in[j], hgrn_lb, hgrn_norm[j], hgrn_w_out[j])
        elif m == 2:
            x = shortconv_mixer(x, g, sc_w_in[j], sc_conv[j], sc_w_out[j])
        else:
            x = mla_mixer(x, g, positions, mla_w_in[j], mla_q_norm[j], mla_w_uq[j],
                          mla_kv_norm[j], mla_w_ukv[j], mla_w_o[j])
        x = expert_choice_ffn(x, norm_ffn[i], moe_router[i], moe_w_gate[i], moe_w_up[i], moe_w_down[i])
    return final_norm(x, norm_final)
```

```python
import functools
import math

import jax
import jax.numpy as jnp
from jax import lax
from jax.experimental import pallas as pl
from jax.experimental.pallas import tpu as pltpu

D_MODEL = 1024
BATCH = 4
SEQ = 8192
DEPTH = 4
N_MIXERS = 4
EPS = 1e-6
GDN_HEADS = 8
GDN_DK = 128
GDN_DV = 128
GDN_CONV = 5
HGRN_EXPAND = 128
HGRN_HEADS = D_MODEL // HGRN_EXPAND
HGRN_DF = HGRN_EXPAND
HGRN_DI = D_MODEL // HGRN_HEADS
HGRN_F = HGRN_HEADS * HGRN_DF
SC_WIDTH = 3
MLA_HEADS = 16
MLA_NOPE = 64
MLA_ROPE = 32
MLA_V = 64
MLA_Q_LORA = 384
MLA_KV_LORA = 256
ROPE_THETA = 10000.0
N_EXPERTS = 16
D_EXPERT = 2048
CAPACITY_FACTOR = 2


def rms_norm(x, g):
    xf = x.astype(jnp.float32)
    y = xf * lax.rsqrt(jnp.mean(xf * xf, axis=-1, keepdims=True) + EPS)
    return (y * g.astype(jnp.float32)).astype(x.dtype)


def l2_normalize(x):
    xf = x.astype(jnp.float32)
    return xf * lax.rsqrt(jnp.sum(xf * xf, axis=-1, keepdims=True) + EPS)


def centred_depthwise_conv(x, w):
    K, C = w.shape
    return lax.conv_general_dilated(
        x, w[:, None, :], window_strides=(1,), padding=[(K // 2, K // 2)],
        dimension_numbers=('NWC', 'WIO', 'NWC'), feature_group_count=C)


LANES = 128
SUBLANES = 8
CONV_TILE = 256
CONV_CHANNELS = 1024


def _row_conv(prev, x, nxt, w):
    T, K = x.shape[0], w.shape[0]
    ext = jnp.concatenate([prev, x, nxt], axis=0)
    acc = None
    for j in range(K):
        start = SUBLANES - K // 2 + j
        term = ext[start:start + T] * w[j:j + 1, :]
        acc = term if acc is None else acc + term
    return acc


def _halo_specs(S, T, ct):
    r = T // SUBLANES
    return [pl.BlockSpec((1, SUBLANES, ct), lambda b, t, c: (b, jnp.maximum(t * r - 1, 0), c)),
            pl.BlockSpec((1, T, ct), lambda b, t, c: (b, t, c)),
            pl.BlockSpec((1, SUBLANES, ct), lambda b, t, c: (b, jnp.minimum((t + 1) * r, S // SUBLANES - 1), c))]


def _gdn_prep_kernel(xp_ref, x_ref, xn_ref, w_ref, o_ref, *, n_norm_chunks):
    t, c = pl.program_id(1), pl.program_id(2)
    prev = jnp.where(t == 0, 0.0, xp_ref[0])
    nxt = jnp.where(t == pl.num_programs(1) - 1, 0.0, xn_ref[0])
    y = _row_conv(prev, x_ref[0], nxt, w_ref[...])
    y = y * jax.nn.sigmoid(y)
    heads = []
    for h in range(y.shape[-1] // LANES):
        yh = y[:, h * LANES:(h + 1) * LANES]
        heads.append(yh * lax.rsqrt(jnp.sum(yh * yh, axis=-1, keepdims=True) + EPS))
    o_ref[0] = jnp.where(c < n_norm_chunks, jnp.concatenate(heads, axis=-1), y)


def gdn_prep(qkv_raw, conv_w, n_norm):
    B, S, C = qkv_raw.shape
    T, ct = min(CONV_TILE, S), min(CONV_CHANNELS, C)
    return pl.pallas_call(
        functools.partial(_gdn_prep_kernel, n_norm_chunks=n_norm // ct),
        grid=(B, S // T, C // ct),
        in_specs=_halo_specs(S, T, ct) + [pl.BlockSpec((conv_w.shape[0], ct), lambda b, t, c: (0, c))],
        out_specs=pl.BlockSpec((1, T, ct), lambda b, t, c: (b, t, c)),
        out_shape=jax.ShapeDtypeStruct((B, S, C), jnp.float32),
        compiler_params=pltpu.CompilerParams(dimension_semantics=("parallel", "parallel", "parallel")),
        name="gdn_prep",
    )(qkv_raw, qkv_raw, qkv_raw, conv_w)


def _shortconv_gate_kernel(gb_ref, cp_ref, c_ref, cn_ref, up_ref, u_ref, un_ref, w_ref, o_ref):
    t = pl.program_id(1)
    first, last = t == 0, t == pl.num_programs(1) - 1
    prev = jnp.where(first, 0.0, cp_ref[0] * up_ref[0])
    nxt = jnp.where(last, 0.0, cn_ref[0] * un_ref[0])
    o_ref[0] = gb_ref[0] * _row_conv(prev, c_ref[0] * u_ref[0], nxt, w_ref[...])


def shortconv_gate(gb, gc, u, conv_w):
    B, S, C = gb.shape
    T, ct = min(CONV_TILE, S), min(CONV_CHANNELS, C)
    halo = _halo_specs(S, T, ct)
    return pl.pallas_call(
        _shortconv_gate_kernel,
        grid=(B, S // T, C // ct),
        in_specs=[halo[1]] + halo + halo + [pl.BlockSpec((conv_w.shape[0], ct), lambda b, t, c: (0, c))],
        out_specs=pl.BlockSpec((1, T, ct), lambda b, t, c: (b, t, c)),
        out_shape=jax.ShapeDtypeStruct((B, S, C), jnp.float32),
        compiler_params=pltpu.CompilerParams(dimension_semantics=("parallel", "parallel", "parallel")),
        name="shortconv_gate",
    )(gb, gc, gc, gc, u, u, u, conv_w)


GDN_BLOCK = 16
GDN_TILE = 256
GDN_GROUP = 4


def _gdn_kernel(*refs):
    C, G, H = GDN_BLOCK, GDN_GROUP, GDN_HEADS
    per_dir = 3 * G + 1
    in_refs, out_refs = refs[:2 * per_dir], refs[2 * per_dir:2 * per_dir + 2]
    scratch = refs[2 * per_dir + 2:]
    state_ref, us_ref, ws_ref, gc_ref, ks_ref = scratch[0], *scratch[13:17]
    u_ref, w_ref, qe_ref, kd_ref, a_ref, dec_ref = scratch[1:7]
    u_rd, w_rd, qe_rd, kd_rd, a_rd, dec_rd = scratch[7:13]
    handoff = tuple(zip(scratch[1:7], scratch[7:13]))
    T = in_refs[0].shape[1]
    nb = T // C
    bf16 = jnp.bfloat16
    scale = GDN_DK ** -0.5
    hg = pl.program_id(1)

    @pl.when(pl.program_id(2) == 0)
    def _():
        state_ref[...] = jnp.zeros_like(state_ref)
        for _, dst in handoff:
            dst[...] = jnp.zeros_like(dst)

    slab = lambda r: pl.ds(r, nb, stride=C)
    ones = jnp.ones((LANES, LANES), bf16)
    lane = lax.broadcasted_iota(jnp.int32, (nb, LANES), 1)
    rowsum = lambda x: jnp.broadcast_to(jnp.sum(x, axis=-1, keepdims=True), (nb, LANES))

    def pick(x, idx):
        return rowsum(jnp.where(lane == idx, x, 0.0))

    orders = (list(range(C)), list(range(C - 1, -1, -1)))
    chains = [(d, g) for d in range(2) for g in range(G)]
    sc_refs = [in_refs[d * per_dir + 3 * G] for d in range(2)]
    qkv_refs = {(d, g): [in_refs[d * per_dir + j * G + g] for j in range(3)] for d, g in chains}

    for d, g in chains:
        run = None
        for r in orders[d]:
            g_r = pick(sc_refs[d][0, slab(r), :], d * H + hg * G + g)
            run = g_r if run is None else run + g_r
            gc_ref[d, g, r] = run
            ks_ref[d, g, r] = qkv_refs[d, g][1][0, slab(r), :]
        dec_ref[d, g] = jnp.exp(run)

    def block_phase(n):
        dots = {}
        for d, g in chains:
            order = orders[d]
            q_ref, k_ref, v_ref = qkv_refs[d, g]
            q_r = q_ref[0, slab(order[n]), :] * scale
            k_r = ks_ref[d, g, order[n]]
            k_prev = [ks_ref[d, g, r2] for r2 in order[:n]]
            prods = [q_r * k_r] + [q_r * k_2 for k_2 in k_prev] + [k_r * k_2 for k_2 in k_prev]
            dots[d, g] = jnp.dot(jnp.concatenate(prods, axis=0).astype(bf16), ones,
                                 preferred_element_type=jnp.float32)
        for d, g in chains:
            order = orders[d]
            r = order[n]
            q_ref, k_ref, v_ref = qkv_refs[d, g]
            gc, us, ws = gc_ref.at[d, g], us_ref.at[d, g], ws_ref.at[d, g]
            dt = dots[d, g]
            q_r = q_ref[0, slab(r), :] * scale
            k_r = ks_ref[d, g, r]
            beta_r = pick(sc_refs[d][0, slab(r), :], 2 * H + d * H + hg * G + g)
            gc_r = gc[r]
            eg_r = jnp.exp(gc_r)
            u_r = beta_r * v_ref[0, slab(r), :]
            w_r = beta_r * eg_r * k_r
            a_row = jnp.where(lane == r, dt[:nb], 0.0)
            for m, r2 in enumerate(order[:n]):
                decay = jnp.exp(gc_r - gc[r2])
                l_rr = beta_r * decay * dt[(1 + n + m) * nb:(2 + n + m) * nb]
                u_r = u_r - l_rr * us[r2]
                w_r = w_r - l_rr * ws[r2]
                a_row = jnp.where(lane == r2, decay * dt[(1 + m) * nb:(2 + m) * nb], a_row)
            us[r] = u_r
            ws[r] = w_r
            u_ref[d, g, slab(r), :] = u_r
            w_ref[d, g, slab(r), :] = w_r
            a_ref[d, g, slab(r), :] = a_row
            qe_ref[d, g, slab(r), :] = q_r * eg_r
            kd_ref[d, g, slab(r), :] = k_r * jnp.exp(gc[order[-1]] - gc_r)

    def state_phase(i):
        blks = [i, nb - 1 - i]
        rows = [pl.ds(blk * C, C) for blk in blks]
        r1s = {}
        for d, g in chains:
            wq = jnp.concatenate([w_rd[d, g, rows[d], :], qe_rd[d, g, rows[d], :]], axis=0).astype(bf16)
            r1s[d, g] = jnp.dot(wq, state_ref[d, g].astype(bf16), preferred_element_type=jnp.float32)
        v_news = {c: (u_rd[c[0], c[1], rows[c[0]], :] - r1s[c][:C]).astype(bf16) for c in chains}
        intra, upd = {}, {}
        for d, g in chains:
            a_blk = a_rd[d, g, rows[d], :][:, :C].astype(bf16)
            intra[d, g] = jnp.dot(a_blk, v_news[d, g], preferred_element_type=jnp.float32)
            upd[d, g] = lax.dot_general(kd_rd[d, g, rows[d], :].astype(bf16), v_news[d, g],
                                        (((0,), (0,)), ((), ())), preferred_element_type=jnp.float32)
        for d, g in chains:
            out_refs[d][0, rows[d], g * LANES:(g + 1) * LANES] = r1s[d, g][C:] + intra[d, g]
            state_ref[d, g] = state_ref[d, g] * dec_rd[d, g, pl.ds(blks[d], 1), :] + upd[d, g]

    for n in range(max(C, nb)):
        if n < nb:
            state_phase(n)
        if n < C:
            block_phase(n)
    for src, dst in handoff:
        dst[...] = src[...]


def gdn_bidir(qkv, scal):
    B, S, _ = qkv.shape
    H, G = GDN_HEADS, GDN_GROUP
    T = min(GDN_TILE, S)
    nT, nG = S // T, H // G
    nb = T // GDN_BLOCK
    fwd_in = lambda t: jnp.minimum(t, nT - 1)
    bwd_in = lambda t: jnp.maximum(nT - 1 - t, 0)
    fwd_out = lambda t: jnp.maximum(t - 1, 0)
    bwd_out = lambda t: jnp.minimum(nT - t, nT - 1)
    blk = lambda tile, off: pl.BlockSpec((1, T, LANES), lambda b, hg, t: (b, tile(t), off + hg * G))
    in_specs, operands = [], []
    for tile in (fwd_in, bwd_in):
        for j in range(3):
            for g in range(G):
                in_specs.append(blk(tile, j * H + g))
                operands.append(qkv)
        in_specs.append(pl.BlockSpec((1, T, LANES), lambda b, hg, t, tile=tile: (b, tile(t), 0)))
        operands.append(scal)
    out_specs = [pl.BlockSpec((1, T, G * LANES), lambda b, hg, t: (b, fwd_out(t), hg)),
                 pl.BlockSpec((1, T, G * LANES), lambda b, hg, t: (b, bwd_out(t), hg))]
    out_sd = jax.ShapeDtypeStruct((B, S, H * LANES), jnp.float32)
    tile_buf = pltpu.VMEM((2, G, T, LANES), jnp.float32)
    dec_buf = pltpu.VMEM((2, G, nb, LANES), jnp.float32)
    slabs = pltpu.VMEM((2, G, GDN_BLOCK, nb, LANES), jnp.float32)
    return pl.pallas_call(
        _gdn_kernel,
        grid=(B, nG, nT + 1),
        in_specs=in_specs,
        out_specs=out_specs,
        out_shape=[out_sd, out_sd],
        scratch_shapes=[pltpu.VMEM((2, G, LANES, LANES), jnp.float32),
                        tile_buf, tile_buf, tile_buf, tile_buf, tile_buf, dec_buf,
                        tile_buf, tile_buf, tile_buf, tile_buf, tile_buf, dec_buf,
                        slabs, slabs, slabs, slabs],
        compiler_params=pltpu.CompilerParams(
            dimension_semantics=("parallel", "parallel", "arbitrary"),
            vmem_limit_bytes=48 * 1024 * 1024),
        name="gdn_bidir",
    )(*operands)


def gdn_mixer(x, gain, w_in, conv_w, A_log, dt_bias, norm_g, w_out):
    B, S, _ = x.shape
    H, dk, dv = GDN_HEADS, GDN_DK, GDN_DV
    n_qkv, n_z = 2 * H * dk + H * dv, H * dv
    qkv = fused_matmul(x, w_in[:, :n_qkv], gain=gain)
    z = fused_matmul(x, w_in[:, n_qkv:n_qkv + n_z], gain=gain)
    ab = fused_matmul(x, w_in[:, n_qkv + n_z:], gain=gain)
    a, b = ab[..., :2 * H], ab[..., 2 * H:]
    qkv = gdn_prep(qkv, conv_w, n_norm=2 * H * dk)
    a = a.astype(jnp.float32).reshape(B, S, 2, H)
    b = b.astype(jnp.float32).reshape(B, S, 2, H)
    g = -jnp.exp(A_log.astype(jnp.float32)) * jax.nn.softplus(a + dt_bias.astype(jnp.float32))
    beta = jax.nn.sigmoid(b)
    scal = jnp.concatenate([g.reshape(B, S, 2 * H), beta.reshape(B, S, 2 * H),
                            jnp.zeros((B, S, LANES - 4 * H), jnp.float32)], axis=-1)
    o_fwd, o_bwd = gdn_bidir(qkv, scal)
    return gated_out_proj(o_fwd, o_bwd, z, norm_g, w_out, x)


HGRN_BLOCK = 16
HGRN_TILE = 256
HGRN_GROUP = 4


def _hgrn_kernel(*refs):
    C, G = HGRN_BLOCK, HGRN_GROUP
    per_dir = 3 * G + 1
    in_refs, out_refs = refs[:2 * per_dir], refs[2 * per_dir:2 * per_dir + 2]
    scratch = refs[2 * per_dir + 2:]
    state_ref, b_ref, kk_ref, vs_ref, qx_ref, kx_ref = scratch[0], *scratch[11:16]
    qe_ref, ke_ref, od_ref, vv_ref, dec_ref = scratch[1:6]
    qe_rd, ke_rd, od_rd, vv_rd, dec_rd = scratch[6:11]
    handoff = tuple(zip(scratch[1:6], scratch[6:11]))
    T = in_refs[0].shape[1]
    nb = T // C
    bf16 = jnp.bfloat16
    chains = [(d, g) for d in range(2) for g in range(G)]
    qvl_refs = {(d, g): [in_refs[d * per_dir + j * G + g] for j in range(3)] for d, g in chains}
    lb_refs = [in_refs[d * per_dir + 3 * G] for d in range(2)]

    @pl.when(pl.program_id(2) == 0)
    def _():
        state_ref[...] = jnp.zeros_like(state_ref)
        for _, dst in handoff:
            dst[...] = jnp.zeros_like(dst)

    slab = lambda r: pl.ds(r, nb, stride=C)
    ones = jnp.ones((LANES, LANES), bf16)

    def log_f(x, log_lb, log_1m_lb):
        log_sig = jnp.minimum(x, 0.0) - jnp.log(1.0 + jnp.exp(-jnp.abs(x)))
        a, b = log_lb, log_1m_lb + log_sig
        return jnp.maximum(a, b) + jnp.log(1.0 + jnp.exp(-jnp.abs(a - b)))

    for d, g in chains:
        order = list(range(C)) if d == 0 else list(range(C - 1, -1, -1))
        v_ref, l_ref = qvl_refs[d, g][1:]
        log_lb = lb_refs[d][0:1, g * LANES:(g + 1) * LANES]
        log_1m_lb = lb_refs[d][1:2, g * LANES:(g + 1) * LANES]
        run = None
        for r in order:
            lf = log_f(l_ref[0, slab(r), :], log_lb, log_1m_lb)
            run = lf if run is None else run + lf
            b_ref[d, g, r] = run
            kk_ref[d, g, r] = 1.0 - jnp.exp(lf)
            vs_ref[d, g, r] = v_ref[0, slab(r), :]
        dec_ref[d, g] = jnp.exp(run)

    n_levels = C.bit_length() - 1
    for d, g in chains:
        q_ref = qvl_refs[d, g][0]
        bs, kk = b_ref.at[d, g], kk_ref.at[d, g]
        for r in range(C):
            q_r = q_ref[0, slab(r), :]
            for lv in range(n_levels):
                upper = (r >> lv) & 1 == 1
                mid = (r >> lv) << lv if upper else ((r >> lv) | 1) << lv
                b_mid = bs[mid - 1] if d == 0 else bs[mid]
                if upper == (d == 0):
                    qx_ref[d, g, lv, r] = q_r * jnp.exp(bs[r] - b_mid)
                else:
                    kx_ref[d, g, lv, r] = kk[r] * jnp.exp(b_mid - bs[r])

    def block_phase(r):
        scores, partners = {}, {}
        for d, g in chains:
            q_ref, v_ref, _ = qvl_refs[d, g]
            bs, kk = b_ref.at[d, g], kk_ref.at[d, g]
            q_r = q_ref[0, slab(r), :]
            b_r = bs[r]
            btot = bs[C - 1 if d == 0 else 0]
            qe_ref[d, g, slab(r), :] = q_r * jnp.exp(b_r)
            ke_ref[d, g, slab(r), :] = kk[r] * jnp.exp(btot - b_r)
            vv_ref[d, g, slab(r), :] = vs_ref[d, g, r]
            partners[d, g] = list(range(r + 1) if d == 0 else range(r, C))
            level = lambda r2: (r ^ r2).bit_length() - 1
            terms = [q_r * kk[r] if r2 == r else qx_ref[d, g, level(r2), r] * kx_ref[d, g, level(r2), r2]
                     for r2 in partners[d, g]]
            scores[d, g] = jnp.dot(jnp.concatenate(terms, axis=0).astype(bf16), ones,
                                   preferred_element_type=jnp.float32)
        for d, g in chains:
            acc = None
            for m, r2 in enumerate(partners[d, g]):
                term = scores[d, g][m * nb:(m + 1) * nb] * vs_ref[d, g, r2]
                acc = term if acc is None else acc + term
            od_ref[d, g, slab(r), :] = acc

    def state_phase(i):
        blks = [i, nb - 1 - i]
        rows = [pl.ds(blk * C, C) for blk in blks]
        inter, upd = {}, {}
        for d, g in chains:
            qe = qe_rd[d, g, rows[d], :].astype(bf16)
            inter[d, g] = lax.dot_general(qe, state_ref[d, g].astype(bf16), (((1,), (1,)), ((), ())),
                                          preferred_element_type=jnp.float32)
            upd[d, g] = lax.dot_general(vv_rd[d, g, rows[d], :].astype(bf16), ke_rd[d, g, rows[d], :].astype(bf16),
                                        (((0,), (0,)), ((), ())), preferred_element_type=jnp.float32)
        for d, g in chains:
            out_refs[d][0, rows[d], g * LANES:(g + 1) * LANES] = od_rd[d, g, rows[d], :] + inter[d, g]
            state_ref[d, g] = state_ref[d, g] * dec_rd[d, g, pl.ds(blks[d], 1), :] + upd[d, g]

    for n in range(max(C, nb)):
        if n < nb:
            state_phase(n)
        if n < C:
            block_phase(n)
    for src, dst in handoff:
        dst[...] = src[...]


def hgrn2_bidir(proj, log_lb):
    B, S, _ = proj.shape
    H = HGRN_HEADS
    T = min(HGRN_TILE, S)
    G = HGRN_GROUP
    nT, nG = S // T, H // G
    nb = T // HGRN_BLOCK
    fwd_in = lambda t: jnp.minimum(t, nT - 1)
    bwd_in = lambda t: jnp.maximum(nT - 1 - t, 0)
    fwd_out = lambda t: jnp.maximum(t - 1, 0)
    bwd_out = lambda t: jnp.minimum(nT - t, nT - 1)
    blk = lambda tile, off: pl.BlockSpec((1, T, LANES), lambda b, hg, t: (b, tile(t), off + hg * G))
    in_specs, operands = [], []
    for d, tile in enumerate((fwd_in, bwd_in)):
        for off in (0, H, (2 + d) * H):
            for g in range(G):
                in_specs.append(blk(tile, off + g))
                operands.append(proj)
        in_specs.append(pl.BlockSpec((None, 2, G * LANES), lambda b, hg, t, d=d: (d, 0, hg)))
        operands.append(log_lb)
    out_specs = [pl.BlockSpec((1, T, G * LANES), lambda b, hg, t: (b, fwd_out(t), hg)),
                 pl.BlockSpec((1, T, G * LANES), lambda b, hg, t: (b, bwd_out(t), hg))]
    out_sd = jax.ShapeDtypeStruct((B, S, H * LANES), jnp.float32)
    tile_buf = pltpu.VMEM((2, G, T, LANES), jnp.float32)
    dec_buf = pltpu.VMEM((2, G, nb, LANES), jnp.float32)
    slabs = pltpu.VMEM((2, G, HGRN_BLOCK, nb, LANES), jnp.float32)
    factors = pltpu.VMEM((2, G, HGRN_BLOCK.bit_length() - 1, HGRN_BLOCK, nb, LANES), jnp.float32)
    return pl.pallas_call(
        _hgrn_kernel,
        grid=(B, nG, nT + 1),
        in_specs=in_specs,
        out_specs=out_specs,
        out_shape=[out_sd, out_sd],
        scratch_shapes=[pltpu.VMEM((2, G, LANES, LANES), jnp.float32),
                        tile_buf, tile_buf, tile_buf, tile_buf, dec_buf,
                        tile_buf, tile_buf, tile_buf, tile_buf, dec_buf,
                        slabs, slabs, slabs, factors, factors],
        compiler_params=pltpu.CompilerParams(
            dimension_semantics=("parallel", "parallel", "arbitrary"),
            vmem_limit_bytes=48 * 1024 * 1024),
        name="hgrn2_bidir",
    )(*operands)


def hgrn2_mixer(x, gain, layer, w_in, lb_table, norm_g, w_out):
    B, S, _ = x.shape
    H, dF, dI = HGRN_HEADS, HGRN_DF, HGRN_DI
    n_qv = HGRN_F + H * dI
    proj = fused_matmul(x, jnp.concatenate([w_in[:, :n_qv], w_in[:, n_qv + H * dI:]], axis=1), gain=gain)
    gate = fused_matmul(x, w_in[:, n_qv:n_qv + H * dI], gain=gain)
    lb_w = jax.nn.softmax(lb_table.astype(jnp.float32), axis=1)
    lb = (jnp.cumsum(lb_w, axis=1) - lb_w[:, :1])[:, layer]
    log_lb = jnp.stack([jnp.log(lb), jnp.log1p(-lb)], axis=1)
    o_fwd, o_bwd = hgrn2_bidir(proj, log_lb)
    return gated_out_proj(o_fwd, o_bwd, gate, norm_g, w_out, x)


def shortconv_mixer(x, gain, w_in, conv_w, w_out):
    D = x.shape[-1]
    gb, gc, u = (fused_matmul(x, w_in[:, j * D:(j + 1) * D], gain=gain) for j in range(3))
    return fused_matmul(shortconv_gate(gb, gc, u, conv_w), w_out, residual=x)


def rope(x, pos):
    half = x.shape[-1] // 2
    inv = ROPE_THETA ** (-jnp.arange(half, dtype=jnp.float32) / half)
    ang = pos.astype(jnp.float32)[:, :, None, None] * inv
    cos, sin = jnp.cos(ang), jnp.sin(ang)
    xf = x.astype(jnp.float32)
    x1, x2 = xf[..., :half], xf[..., half:]
    return jnp.concatenate([x1 * cos - x2 * sin, x1 * sin + x2 * cos], axis=-1).astype(x.dtype)


MLA_QK_PAD = 128
MLA_TQ = 512
MLA_TK = 8192


def _mla_attn_kernel(q_ref, k_ref, v_ref, o_ref, *, tk):
    tq = q_ref.shape[1]
    nk = k_ref.shape[1] // tk
    head_lanes = [slice(hh * MLA_QK_PAD, (hh + 1) * MLA_QK_PAD) for hh in range(2)]
    qs = [q_ref[0, :, lanes] for lanes in head_lanes]

    def body(j, carry):
        rows = pl.ds(pl.multiple_of(j * tk, tk), tk)
        v = v_ref[0, rows, :]
        new = []
        for hh in range(2):
            m, l, acc = carry[hh]
            k = k_ref[0, rows, head_lanes[hh]]
            s = lax.dot_general(qs[hh], k, (((1,), (1,)), ((), ())), preferred_element_type=jnp.float32)
            m_new = jnp.maximum(m, jnp.max(s, axis=-1, keepdims=True))
            p = jnp.exp2(s - m_new)
            alpha = jnp.exp2(m - m_new)
            l = alpha * l + jnp.sum(p, axis=-1, keepdims=True)
            acc = alpha * acc + jnp.dot(p.astype(jnp.bfloat16), v, preferred_element_type=jnp.float32)
            new.append((m_new, l, acc))
        return tuple(new)

    init = (jnp.full((tq, 1), -jnp.inf, jnp.float32), jnp.zeros((tq, 1), jnp.float32),
            jnp.zeros((tq, 2 * MLA_V), jnp.float32))
    (_, l0, acc0), (_, l1, acc1) = lax.fori_loop(0, nk, body, (init, init))
    lane = lax.broadcasted_iota(jnp.int32, (tq, 2 * MLA_V), 1)
    o_ref[0] = jnp.where(lane < MLA_V, acc0 / l0, acc1 / l1).astype(o_ref.dtype)


def mla_attention(qf, kf, vf):
    B, S, _ = qf.shape
    H = MLA_HEADS
    tq, tk = min(MLA_TQ, S), min(MLA_TK, S)
    return pl.pallas_call(
        functools.partial(_mla_attn_kernel, tk=tk),
        grid=(B, H // 2, S // tq),
        in_specs=[pl.BlockSpec((1, tq, 2 * MLA_QK_PAD), lambda b, h, i: (b, i, h)),
                  pl.BlockSpec((1, S, 2 * MLA_QK_PAD), lambda b, h, i: (b, 0, h)),
                  pl.BlockSpec((1, S, 2 * MLA_V), lambda b, h, i: (b, 0, h))],
        out_specs=pl.BlockSpec((1, tq, 2 * MLA_V), lambda b, h, i: (b, i, h)),
        out_shape=jax.ShapeDtypeStruct((B, S, H * MLA_V), jnp.float32),
        compiler_params=pltpu.CompilerParams(
            dimension_semantics=("parallel", "parallel", "arbitrary"),
            vmem_limit_bytes=48 * 1024 * 1024),
        name="mla_attention",
    )(qf, kf, vf)


PACK_TM = 512


def _pack_q_kernel(q_ref, c_ref, sp_ref, sm_ref, o_ref, *, scale, half):
    q = q_ref[...]
    n_heads = q.shape[-1] // LANES
    wide = lambda ref: jnp.concatenate([ref[...]] * n_heads, axis=-1)
    up = pltpu.roll(q, q.shape[-1] - half, axis=1)
    down = pltpu.roll(q, half, axis=1)
    o_ref[...] = ((q * wide(c_ref) + up * wide(sp_ref) + down * wide(sm_ref)) * scale).astype(o_ref.dtype)


def _pack_k_kernel(k_ref, kr_ref, o_ref):
    n_heads = k_ref.shape[-1] // LANES
    o_ref[...] = (k_ref[...] + jnp.concatenate([kr_ref[...]] * n_heads, axis=-1)).astype(o_ref.dtype)


def _pack_call(kernel_fn, wide, narrow, name):
    M, N = wide.shape
    tm = min(PACK_TM, M)
    return pl.pallas_call(
        kernel_fn,
        grid=(M // tm,),
        in_specs=[pl.BlockSpec((tm, N), lambda m: (m, 0))] + [pl.BlockSpec((tm, LANES), lambda m: (m, 0))] * len(narrow),
        out_specs=pl.BlockSpec((tm, N), lambda m: (m, 0)),
        out_shape=jax.ShapeDtypeStruct((M, N), jnp.bfloat16),
        compiler_params=pltpu.CompilerParams(dimension_semantics=("parallel",)),
        name=name,
    )(wide, *narrow)


def mla_mixer(x, gain, pos, w_in, q_norm, w_uq, kv_norm, w_ukv, w_o):
    B, S, _ = x.shape
    H, M = MLA_HEADS, B * S
    n_q, n_kv, half = MLA_Q_LORA, MLA_KV_LORA, MLA_ROPE // 2
    pad = MLA_QK_PAD - MLA_NOPE - MLA_ROPE
    f32 = jnp.float32
    cq = fused_matmul(x, w_in[:, :n_q], gain=gain)
    ckv = fused_matmul(x, w_in[:, n_q:n_q + n_kv], gain=gain)
    kr = fused_matmul(x, w_in[:, n_q + n_kv:], gain=gain)
    w_q = jnp.pad(w_uq.reshape(n_q, H, MLA_NOPE + MLA_ROPE), ((0, 0), (0, 0), (0, pad))).reshape(n_q, H * MLA_QK_PAD)
    w_kv = w_ukv.reshape(n_kv, H, MLA_NOPE + MLA_V)
    w_k = jnp.pad(w_kv[..., :MLA_NOPE], ((0, 0), (0, 0), (0, MLA_QK_PAD - MLA_NOPE))).reshape(n_kv, H * MLA_QK_PAD)
    w_v = w_kv[..., MLA_NOPE:].reshape(n_kv, H * MLA_V)
    q_pad = fused_matmul(cq, w_q, gain=q_norm).reshape(M, H * MLA_QK_PAD)
    k_pad = fused_matmul(ckv, w_k, gain=kv_norm).reshape(M, H * MLA_QK_PAD)
    vf = fused_matmul(ckv, w_v, gain=kv_norm, out_dtype=jnp.bfloat16)
    inv = ROPE_THETA ** (-jnp.arange(half, dtype=f32) / half)
    ang = pos.astype(f32).reshape(M, 1) * inv
    cos, sin = jnp.cos(ang), jnp.sin(ang)
    zeros = lambda n: jnp.zeros((M, n), f32)
    c_tab = jnp.concatenate([jnp.ones((M, MLA_NOPE), f32), cos, cos, zeros(pad)], axis=-1)
    sp_tab = jnp.concatenate([zeros(MLA_NOPE), -sin, zeros(half), zeros(pad)], axis=-1)
    sm_tab = jnp.concatenate([zeros(MLA_NOPE), zeros(half), sin, zeros(pad)], axis=-1)
    scale = (MLA_NOPE + MLA_ROPE) ** -0.5 * math.log2(math.e)
    qf = _pack_call(functools.partial(_pack_q_kernel, scale=scale, half=half), q_pad, [c_tab, sp_tab, sm_tab], "pack_q")
    kr_tile = jnp.concatenate([zeros(MLA_NOPE), rope(kr[:, :, None, :], pos).reshape(M, MLA_ROPE), zeros(pad)], axis=-1)
    kf = _pack_call(_pack_k_kernel, k_pad, [kr_tile], "pack_k")
    o = mla_attention(qf.reshape(B, S, -1), kf.reshape(B, S, -1), vf)
    return fused_matmul(o, w_o, residual=x)


ROUTE_TM = 1024


def _route_kernel(x_ref, g_ref, wr_ref, h_ref, aff_ref):
    x = x_ref[...]
    h = (x * lax.rsqrt(jnp.mean(x * x, axis=-1, keepdims=True) + EPS) * g_ref[...]).astype(jnp.bfloat16)
    h_ref[...] = h
    logits = jnp.dot(h, wr_ref[...].astype(jnp.bfloat16), preferred_element_type=jnp.float32)
    e = jnp.exp(logits - jnp.max(logits, axis=-1, keepdims=True))
    aff_ref[...] = e / jnp.sum(e, axis=-1, keepdims=True)


def route(x, gain, w_router):
    B, S, D = x.shape
    E = w_router.shape[-1]
    M = B * S
    tm = min(ROUTE_TM, M)
    h, aff = pl.pallas_call(
        _route_kernel,
        grid=(M // tm,),
        in_specs=[pl.BlockSpec((tm, D), lambda i: (i, 0)),
                  pl.BlockSpec((1, D), lambda i: (0, 0)),
                  pl.BlockSpec((D, E), lambda i: (0, 0))],
        out_specs=[pl.BlockSpec((tm, D), lambda i: (i, 0)),
                   pl.BlockSpec((tm, E), lambda i: (i, 0))],
        out_shape=[jax.ShapeDtypeStruct((M, D), jnp.bfloat16), jax.ShapeDtypeStruct((M, E), jnp.float32)],
        compiler_params=pltpu.CompilerParams(dimension_semantics=("parallel",)),
        name="route",
    )(x.reshape(M, D), gain.reshape(1, D).astype(jnp.float32), w_router)
    return h.reshape(B, S, D), aff.reshape(B, S, E)


def expert_choice_ffn(x, gain, w_router, w_gate, w_up, w_down):
    B, S, _ = x.shape
    cap = CAPACITY_FACTOR * S // N_EXPERTS
    h, aff = route(x, gain, w_router)
    gate, idx = lax.top_k(jnp.swapaxes(aff, 1, 2), cap)
    bi = jnp.arange(B)[:, None, None]
    ys = expert_ffn(h[bi, idx], gate, w_gate, w_up, w_down)
    return x.at[bi, idx].add(ys)


FFN_BATCH_ROWS = 2
FFN_TF = 512


def _expert_ffn_kernel(x_ref, g_ref, wg_ref, wu_ref, wd_ref, o_ref):
    f = pl.program_id(2)
    bb, _, cap, d = x_ref.shape
    x = x_ref[...].reshape(bb * cap, d)
    bf16 = jnp.bfloat16
    half = wg_ref.shape[-1] // 2
    y = None
    for lo in (0, half):
        a = jnp.dot(x, wg_ref[0, :, lo:lo + half].astype(bf16), preferred_element_type=jnp.float32)
        u = jnp.dot(x, wu_ref[0, :, lo:lo + half].astype(bf16), preferred_element_type=jnp.float32)
        hid = (a * jax.nn.sigmoid(a) * u).astype(bf16)
        part = jnp.dot(hid, wd_ref[0, lo:lo + half, :].astype(bf16), preferred_element_type=jnp.float32)
        y = part if y is None else y + part
    y = y.reshape(bb, 1, cap, d)

    @pl.when(f == 0)
    def _():
        o_ref[...] = y

    @pl.when(f > 0)
    def _():
        o_ref[...] += y

    @pl.when(f == pl.num_programs(2) - 1)
    def _():
        o_ref[...] = o_ref[...] * g_ref[...]


def expert_ffn(xs, gate, w_gate, w_up, w_down):
    B, E, cap, D = xs.shape
    F = w_gate.shape[-1]
    bb = min(FFN_BATCH_ROWS, B)
    tf = min(FFN_TF, F)
    return pl.pallas_call(
        _expert_ffn_kernel,
        grid=(E, B // bb, F // tf),
        in_specs=[pl.BlockSpec((bb, 1, cap, D), lambda e, b, f: (b, e, 0, 0)),
                  pl.BlockSpec((bb, 1, cap, 1), lambda e, b, f: (b, e, 0, 0)),
                  pl.BlockSpec((1, D, tf), lambda e, b, f: (e, 0, f)),
                  pl.BlockSpec((1, D, tf), lambda e, b, f: (e, 0, f)),
                  pl.BlockSpec((1, tf, D), lambda e, b, f: (e, f, 0))],
        out_specs=pl.BlockSpec((bb, 1, cap, D), lambda e, b, f: (b, e, 0, 0)),
        out_shape=jax.ShapeDtypeStruct((B, E, cap, D), jnp.float32),
        compiler_params=pltpu.CompilerParams(
            dimension_semantics=("parallel", "parallel", "arbitrary"),
            vmem_limit_bytes=56 * 1024 * 1024),
        name="expert_ffn",
    )(xs.astype(jnp.bfloat16), gate[..., None].astype(jnp.float32), w_gate, w_up, w_down)


PROJ_TM = 1024
PROJ_TN_CHOICES = (1024, 768, 512, 384, 256, 128)


def _proj_kernel(*refs, normed, with_residual):
    refs = list(refs)
    x = refs.pop(0)[...]
    if normed:
        g = refs.pop(0)[...]
        x = x * lax.rsqrt(jnp.mean(x * x, axis=-1, keepdims=True) + EPS) * g
    w = refs.pop(0)[...]
    y = jnp.dot(x.astype(jnp.bfloat16), w.astype(jnp.bfloat16), preferred_element_type=jnp.float32)
    if with_residual:
        y = y + refs.pop(0)[...]
    o_ref = refs.pop(0)
    o_ref[...] = y.astype(o_ref.dtype)


def fused_matmul(x, w, gain=None, residual=None, out_dtype=jnp.float32):
    lead, K = x.shape[:-1], x.shape[-1]
    N = w.shape[-1]
    M = math.prod(lead)
    tm = min(PROJ_TM, M)
    tn = next((t for t in PROJ_TN_CHOICES if N % t == 0), N)
    operands = [x.reshape(M, K)]
    in_specs = [pl.BlockSpec((tm, K), lambda n, m: (m, 0))]
    if gain is not None:
        operands.append(gain.reshape(1, K).astype(jnp.float32))
        in_specs.append(pl.BlockSpec((1, K), lambda n, m: (0, 0)))
    operands.append(w)
    in_specs.append(pl.BlockSpec((K, tn), lambda n, m: (0, n)))
    if residual is not None:
        operands.append(residual.reshape(M, N))
        in_specs.append(pl.BlockSpec((tm, tn), lambda n, m: (m, n)))
    out = pl.pallas_call(
        functools.partial(_proj_kernel, normed=gain is not None, with_residual=residual is not None),
        grid=(N // tn, M // tm),
        in_specs=in_specs,
        out_specs=pl.BlockSpec((tm, tn), lambda n, m: (m, n)),
        out_shape=jax.ShapeDtypeStruct((M, N), out_dtype),
        compiler_params=pltpu.CompilerParams(
            dimension_semantics=("parallel", "parallel"),
            vmem_limit_bytes=48 * 1024 * 1024),
        name="fused_matmul",
    )(*operands)
    return out.reshape(*lead, N)


OUT_TM = 512


def _gated_out_kernel(of_ref, ob_ref, z_ref, g_ref, w_ref, res_ref, o_ref):
    o = of_ref[...] + ob_ref[...]
    z = z_ref[...]
    g = g_ref[...]
    heads = []
    for h in range(o.shape[-1] // LANES):
        oh = o[:, h * LANES:(h + 1) * LANES]
        heads.append(oh * lax.rsqrt(jnp.mean(oh * oh, axis=-1, keepdims=True) + EPS) * g)
    y = jnp.concatenate(heads, axis=-1) * (z * jax.nn.sigmoid(z))
    o_ref[...] = res_ref[...] + jnp.dot(y.astype(jnp.bfloat16), w_ref[...].astype(jnp.bfloat16),
                                        preferred_element_type=jnp.float32)


def gated_out_proj(o_fwd, o_bwd, z, norm_g, w_out, residual):
    lead, K = o_fwd.shape[:-1], o_fwd.shape[-1]
    N = w_out.shape[-1]
    M = math.prod(lead)
    tm = min(OUT_TM, M)
    row = lambda width: pl.BlockSpec((tm, width), lambda m: (m, 0))
    out = pl.pallas_call(
        _gated_out_kernel,
        grid=(M // tm,),
        in_specs=[row(K), row(K), row(K),
                  pl.BlockSpec((1, LANES), lambda m: (0, 0)),
                  pl.BlockSpec((K, N), lambda m: (0, 0)),
                  row(N)],
        out_specs=row(N),
        out_shape=jax.ShapeDtypeStruct((M, N), jnp.float32),
        compiler_params=pltpu.CompilerParams(
            dimension_semantics=("parallel",),
            vmem_limit_bytes=48 * 1024 * 1024),
        name="gated_out_proj",
    )(o_fwd.reshape(M, K), o_bwd.reshape(M, K), z.reshape(M, K),
      norm_g.reshape(1, LANES).astype(jnp.float32), w_out, residual.reshape(M, N))
    return out.reshape(*lead, N)


def _final_norm_kernel(x_ref, g_ref, o_ref):
    x = x_ref[...]
    y = x * lax.rsqrt(jnp.mean(x * x, axis=-1, keepdims=True) + EPS)
    o_ref[...] = y * g_ref[...]


def final_norm(x, g):
    B, S, D = x.shape
    x2 = x.reshape(B * S, D)
    tm = 1024
    out = pl.pallas_call(
        _final_norm_kernel,
        grid=(B * S // tm,),
        in_specs=[pl.BlockSpec((tm, D), lambda i: (i, 0)),
                  pl.BlockSpec((1, D), lambda i: (0, 0))],
        out_specs=pl.BlockSpec((tm, D), lambda i: (i, 0)),
        out_shape=jax.ShapeDtypeStruct((B * S, D), x.dtype),
        name="final_norm",
    )(x2, g.reshape(1, D))
    return out.reshape(B, S, D)


def kernel(x, positions, norm_mix, norm_ffn, norm_final,
           gdn_w_in, gdn_conv, gdn_A_log, gdn_dt_bias, gdn_norm, gdn_w_out,
           hgrn_w_in, hgrn_lb, hgrn_norm, hgrn_w_out,
           sc_w_in, sc_conv, sc_w_out,
           mla_w_in, mla_q_norm, mla_w_uq, mla_kv_norm, mla_w_ukv, mla_w_o,
           moe_router, moe_w_gate, moe_w_up, moe_w_down):
    for i in range(DEPTH):
        m, j = i % N_MIXERS, i // N_MIXERS
        g = norm_mix[i]
        if m == 0:
            x = gdn_mixer(x, g, gdn_w_in[j], gdn_conv[j], gdn_A_log[j], gdn_dt_bias[j], gdn_norm[j], gdn_w_out[j])
        elif m == 1:
            x = hgrn2_mixer(x, g, i, hgrn_w_in[j], hgrn_lb, hgrn_norm[j], hgrn_w_out[j])
        elif m == 2:
            x = shortconv_mixer(x, g, sc_w_in[j], sc_conv[j], sc_w_out[j])
        else:
            x = mla_mixer(x, g, positions, mla_w_in[j], mla_q_norm[j], mla_w_uq[j],
                          mla_kv_norm[j], mla_w_ukv[j], mla_w_o[j])
        x = expert_choice_ffn(x, norm_ffn[i], moe_router[i], moe_w_gate[i], moe_w_up[i], moe_w_down[i])
    return final_norm(x, norm_final)
```
